```python
import math
import jax, jax.numpy as jnp
from jax import lax
import numpy as np

D_MODEL = 2048
BATCH = 4
SEQ = 2048
DEPTH = 1
DEC_BATCH = 128
DEC_SEQ = 4
PAST_LEN = 16384
PAGE_SIZE = 128

SSM_D_INNER = 2 * D_MODEL
SSM_HEAD_DIM = 64
SSM_N_HEADS = SSM_D_INNER // SSM_HEAD_DIM
SSM_N_GROUPS = 8
SSM_D_STATE = 128
SSM_CONV_WIDTH = 4
SSM_BC = SSM_N_GROUPS * SSM_D_STATE
SSM_CONV_DIM = SSM_D_INNER + 2 * SSM_BC
SSM_CHUNK = 128
HG_KEY_DIM = 128
HG_N_HEADS = D_MODEL // HG_KEY_DIM
HG_VAL_DIM = D_MODEL // HG_N_HEADS
HG_WIDTH_K = HG_N_HEADS * HG_KEY_DIM
HG_WIDTH_V = HG_N_HEADS * HG_VAL_DIM
HG_CHUNK = 64
FFN_HIDDEN = 4 * D_MODEL
PLE_DIM = 256
NORM_EPS = 1e-6
OFF_Z = 0
OFF_XBC = OFF_Z + SSM_D_INNER
OFF_DT = OFF_XBC + SSM_CONV_DIM
OFF_HQ = OFF_DT + SSM_N_HEADS
OFF_HF = OFF_HQ + HG_WIDTH_K
OFF_HI = OFF_HF + HG_WIDTH_K
OFF_HG = OFF_HI + HG_WIDTH_V
OFF_GATE = OFF_HG + HG_WIDTH_V
IN_DIM = OFF_GATE + 2 * D_MODEL

kernel_name = 'ssd_hgrn2_gated_parallel_decoder_step'


def _rmsnorm(x, w):
    xf = x.astype(jnp.float32)
    xf = xf * lax.rsqrt(jnp.mean(xf * xf, axis=-1, keepdims=True) + NORM_EPS)
    return xf * w.astype(jnp.float32)


def _pad_time(a, lp):
    return jnp.pad(a, [(0, 0), (0, lp - a.shape[1])] + [(0, 0)] * (a.ndim - 2))


def _chunks(a, c):
    b, lp = a.shape[:2]
    return jnp.moveaxis(a.reshape((b, lp // c, c) + a.shape[2:]), 1, 0)


def _ssd_chunked(x, dt, a_neg, b_mat, c_mat, h0):
    bsz, L = x.shape[:2]
    G, J = SSM_N_GROUPS, SSM_N_HEADS // SSM_N_GROUPS
    C = min(SSM_CHUNK, L)
    lp = -(-L // C) * C
    xs = _chunks(_pad_time(x, lp).reshape(bsz, lp, G, J, SSM_HEAD_DIM), C)
    dts = _chunks(_pad_time(dt, lp).reshape(bsz, lp, G, J), C)
    bs = _chunks(_pad_time(b_mat, lp), C)
    cs = _chunks(_pad_time(c_mat, lp), C)
    a_g = a_neg.reshape(G, J)
    mask = jnp.tril(jnp.ones((C, C), dtype=bool))[None, :, :, None, None]

    def step(h, inp):
        xc, dtc, bc, cc = inp
        cum = jnp.cumsum(dtc * a_g, axis=1)
        seg = cum[:, :, None] - cum[:, None]
        decay = jnp.exp(jnp.where(mask, seg, -jnp.inf))
        cb = jnp.einsum('btgn,bsgn->btsg', cc, bc)
        m = cb[..., None] * decay * dtc[:, None]
        y = jnp.einsum('btsgj,bsgjp->btgjp', m, xc)
        y = y + jnp.einsum('btgn,bgjpn->btgjp', cc, h) * jnp.exp(cum)[..., None]
        last = cum[:, -1]
        w = dtc * jnp.exp(last[:, None] - cum)
        h = jnp.exp(last)[..., None, None] * h + jnp.einsum('bsgn,bsgj,bsgjp->bgjpn', bc, w, xc)
        return h, y

    h, ys = lax.scan(step, h0.reshape(bsz, G, J, SSM_HEAD_DIM, SSM_D_STATE), (xs, dts, bs, cs))
    y = jnp.moveaxis(ys, 0, 1).reshape(bsz, lp, SSM_N_HEADS, SSM_HEAD_DIM)[:, :L]
    return y, h.reshape(bsz, SSM_N_HEADS, SSM_HEAD_DIM, SSM_D_STATE)


def _hgrn2_chunked(q, k, log_f, v, s0):
    bsz, L = q.shape[:2]
    C = min(HG_CHUNK, L)
    lp = -(-L // C) * C
    qs = _chunks(_pad_time(q, lp), C)
    kks = _chunks(_pad_time(k, lp), C)
    lfs = _chunks(_pad_time(log_f, lp), C)
    vs = _chunks(_pad_time(v, lp), C)
    mask = jnp.tril(jnp.ones((C, C), dtype=bool))[None, :, :, None, None]

    def step(s, inp):
        qc, kc, lfc, vc = inp
        cum = jnp.cumsum(lfc, axis=1)
        seg = cum[:, :, None] - cum[:, None]
        decay = jnp.exp(jnp.where(mask, seg, -jnp.inf))
        att = jnp.einsum('bthk,btshk,bshk->btsh', qc, decay, kc)
        o = jnp.einsum('btsh,bshv->bthv', att, vc)
        o = o + jnp.einsum('bthk,bhkv->bthv', qc * jnp.exp(cum), s)
        last = cum[:, -1]
        s = jnp.exp(last)[..., None] * s + jnp.einsum('bshk,bshv->bhkv', kc * jnp.exp(last[:, None] - cum), vc)
        return s, o

    s, os_ = lax.scan(step, s0, (qs, kks, lfs, vs))
    o = jnp.moveaxis(os_, 0, 1).reshape(bsz, lp, HG_N_HEADS, HG_VAL_DIM)[:, :L]
    return o, s


def _layer(x, p, ssm0, conv0, hg0, lb, norm_mix_pre, w_in, conv_w, conv_b, dt_bias, a_log, d_skip,
           ssm_norm, w_br_a, hg_norm, w_br_b, w_out, norm_mix_post, norm_ffn_pre, w_up, w_down,
           norm_ffn_post, norm_ple, w_ple_gate, w_ple_proj):
    f32 = jnp.float32
    bsz, L, _ = x.shape
    h = _rmsnorm(x, norm_mix_pre)
    proj = jnp.einsum('bld,de->ble', h, w_in.astype(f32))
    z = proj[..., OFF_Z:OFF_XBC]
    xbc = proj[..., OFF_XBC:OFF_DT]
    dt_raw = proj[..., OFF_DT:OFF_HQ]
    hq = proj[..., OFF_HQ:OFF_HF].reshape(bsz, L, HG_N_HEADS, HG_KEY_DIM)
    hf = proj[..., OFF_HF:OFF_HI].reshape(bsz, L, HG_N_HEADS, HG_KEY_DIM)
    hi = proj[..., OFF_HI:OFF_HG].reshape(bsz, L, HG_N_HEADS, HG_VAL_DIM)
    hg = proj[..., OFF_HG:OFF_GATE]
    gates = jax.nn.sigmoid(proj[..., OFF_GATE:].reshape(bsz, L, 2, D_MODEL))

    xpad = jnp.concatenate([conv0.astype(f32), xbc], axis=1)
    new_conv = xpad[:, -(SSM_CONV_WIDTH - 1):]
    xc = lax.conv_general_dilated(xpad, conv_w.astype(f32)[:, None, :], window_strides=(1,), padding='VALID',
                                  dimension_numbers=('NWC', 'WIO', 'NWC'), feature_group_count=SSM_CONV_DIM)
    xc = jax.nn.silu(xc + conv_b.astype(f32))
    xs = xc[..., :SSM_D_INNER].reshape(bsz, L, SSM_N_HEADS, SSM_HEAD_DIM)
    bm = xc[..., SSM_D_INNER:SSM_D_INNER + SSM_BC].reshape(bsz, L, SSM_N_GROUPS, SSM_D_STATE)
    cm = xc[..., SSM_D_INNER + SSM_BC:].reshape(bsz, L, SSM_N_GROUPS, SSM_D_STATE)
    dt = jax.nn.softplus(dt_raw + dt_bias.astype(f32))
    a_neg = -jnp.exp(a_log.astype(f32))
    y, new_ssm = _ssd_chunked(xs, dt, a_neg, bm, cm, ssm0.astype(f32))
    y = y + d_skip.astype(f32)[:, None] * xs
    y = (y.reshape(bsz, L, SSM_D_INNER) * jax.nn.silu(z)).reshape(bsz, L, SSM_N_GROUPS, SSM_D_INNER // SSM_N_GROUPS)
    y = _rmsnorm(y, ssm_norm.reshape(SSM_N_GROUPS, SSM_D_INNER // SSM_N_GROUPS)).reshape(bsz, L, SSM_D_INNER)
    y_a = jnp.einsum('ble,ed->bld', y, w_br_a.astype(f32))

    lb = lb.astype(f32).reshape(HG_N_HEADS, HG_KEY_DIM)
    log_f = jnp.log(lb + (1.0 - lb) * jax.nn.sigmoid(hf))
    k = (1.0 - lb) * jax.nn.sigmoid(-hf)
    o, new_hg = _hgrn2_chunked(hq, k, log_f, hi, hg0.astype(f32))
    o = _rmsnorm(o, hg_norm).reshape(bsz, L, HG_WIDTH_V) * jax.nn.silu(hg)
    y_b = jnp.einsum('ble,ed->bld', o, w_br_b.astype(f32))

    mixed = gates[:, :, 0] * y_a + gates[:, :, 1] * y_b
    mix_out = jnp.einsum('bld,de->ble', mixed, w_out.astype(f32))
    x = x + _rmsnorm(mix_out, norm_mix_post).astype(x.dtype)

    h2 = _rmsnorm(x, norm_ffn_pre)
    u = jnp.square(jax.nn.relu(jnp.einsum('bld,df->blf', h2, w_up.astype(f32))))
    ffn = jnp.einsum('blf,fd->bld', u, w_down.astype(f32))
    x = x + _rmsnorm(ffn, norm_ffn_post).astype(x.dtype)

    g_ple = jax.nn.sigmoid(jnp.einsum('bld,de->ble', _rmsnorm(x, norm_ple), w_ple_gate.astype(f32)))
    e = jnp.einsum('blp,pd->bld', p.astype(f32), w_ple_proj.astype(f32))
    x = x + (g_ple * e).astype(x.dtype)
    return x, new_ssm.astype(ssm0.dtype), new_conv.astype(conv0.dtype), new_hg.astype(hg0.dtype)


def setup_inputs(seed: int = 0) -> dict:
    key = jax.random.key(seed)
    ks = iter(jax.random.split(key, 40))

    def nrm(shape, scale):
        return scale * jax.random.normal(next(ks), shape, jnp.float32)

    def gain(shape):
        return 1.0 + nrm(shape, 0.05)

    u = jax.random.uniform(next(ks), (DEPTH, SSM_N_HEADS), jnp.float32)
    dt0 = jnp.exp(u * (math.log(0.1) - math.log(0.001)) + math.log(0.001))
    dt_bias = dt0 + jnp.log(-jnp.expm1(-dt0))
    a_log = jnp.log(jax.random.uniform(next(ks), (DEPTH, SSM_N_HEADS), jnp.float32, 1.0, 16.0))
    return {
        'x_prompt': nrm((BATCH, SEQ, D_MODEL), 1.0),
        'x_sample': nrm((DEC_BATCH, DEC_SEQ, D_MODEL), 1.0),
        'p_prompt': nrm((DEPTH, BATCH, SEQ, PLE_DIM), 1.0),
        'p_sample': nrm((DEPTH, DEC_BATCH, DEC_SEQ, PLE_DIM), 1.0),
        'state_ssm': nrm((DEPTH, DEC_BATCH, SSM_N_HEADS, SSM_HEAD_DIM, SSM_D_STATE), 0.3),
        'state_conv': nrm((DEPTH, DEC_BATCH, SSM_CONV_WIDTH - 1, SSM_CONV_DIM), 1.0),
        'state_hgrn': nrm((DEPTH, DEC_BATCH, HG_N_HEADS, HG_KEY_DIM, HG_VAL_DIM), 0.3),
        'norm_mix_pre': gain((DEPTH, D_MODEL)),
        'w_in': nrm((DEPTH, D_MODEL, IN_DIM), D_MODEL ** -0.5),
        'conv_w': nrm((DEPTH, SSM_CONV_WIDTH, SSM_CONV_DIM), SSM_CONV_WIDTH ** -0.5),
        'conv_b': nrm((DEPTH, SSM_CONV_DIM), 0.02),
        'dt_bias': dt_bias,
        'a_log': a_log,
        'd_skip': gain((DEPTH, SSM_N_HEADS)),
        'ssm_norm': gain((DEPTH, SSM_D_INNER)),
        'w_br_a': nrm((DEPTH, SSM_D_INNER, D_MODEL), SSM_D_INNER ** -0.5),
        'hg_lb': nrm((DEPTH + 1, HG_WIDTH_K), 0.1),
        'hg_norm': gain((DEPTH, HG_VAL_DIM)),
        'w_br_b': nrm((DEPTH, HG_WIDTH_V, D_MODEL), HG_WIDTH_V ** -0.5),
        'w_out': nrm((DEPTH, D_MODEL, D_MODEL), D_MODEL ** -0.5),
        'norm_mix_post': gain((DEPTH, D_MODEL)),
        'norm_ffn_pre': gain((DEPTH, D_MODEL)),
        'w_up': nrm((DEPTH, D_MODEL, FFN_HIDDEN), D_MODEL ** -0.5),
        'w_down': nrm((DEPTH, FFN_HIDDEN, D_MODEL), FFN_HIDDEN ** -0.5),
        'norm_ffn_post': gain((DEPTH, D_MODEL)),
        'norm_ple': gain((DEPTH, D_MODEL)),
        'w_ple_gate': nrm((DEPTH, D_MODEL, D_MODEL), D_MODEL ** -0.5),
        'w_ple_proj': nrm((DEPTH, PLE_DIM, D_MODEL), PLE_DIM ** -0.5),
    }


def reference(x_prompt, x_sample, p_prompt, p_sample, state_ssm, state_conv, state_hgrn,
              norm_mix_pre, w_in, conv_w, conv_b, dt_bias, a_log, d_skip, ssm_norm, w_br_a,
              hg_lb, hg_norm, w_br_b, w_out, norm_mix_post, norm_ffn_pre, w_up, w_down,
              norm_ffn_post, norm_ple, w_ple_gate, w_ple_proj):
    lbs = jnp.cumsum(jax.nn.softmax(hg_lb.astype(jnp.float32), axis=0), axis=0)
    bp = x_prompt.shape[0]
    ssm_p0 = jnp.zeros((bp, SSM_N_HEADS, SSM_HEAD_DIM, SSM_D_STATE), x_prompt.dtype)
    conv_p0 = jnp.zeros((bp, SSM_CONV_WIDTH - 1, SSM_CONV_DIM), x_prompt.dtype)
    hg_p0 = jnp.zeros((bp, HG_N_HEADS, HG_KEY_DIM, HG_VAL_DIM), x_prompt.dtype)
    yp, ys = x_prompt, x_sample
    ssm_p, conv_p, hg_p, ssm_s, conv_s, hg_s = [], [], [], [], [], []
    for li in range(DEPTH):
        lw = (norm_mix_pre[li], w_in[li], conv_w[li], conv_b[li], dt_bias[li], a_log[li], d_skip[li],
              ssm_norm[li], w_br_a[li], hg_norm[li], w_br_b[li], w_out[li], norm_mix_post[li],
              norm_ffn_pre[li], w_up[li], w_down[li], norm_ffn_post[li], norm_ple[li],
              w_ple_gate[li], w_ple_proj[li])
        yp, s1, c1, g1 = _layer(yp, p_prompt[li], ssm_p0, conv_p0, hg_p0, lbs[li], *lw)
        ys, s2, c2, g2 = _layer(ys, p_sample[li], state_ssm[li], state_conv[li], state_hgrn[li], lbs[li], *lw)
        ssm_p.append(s1)
        conv_p.append(c1)
        hg_p.append(g1)
        ssm_s.append(s2)
        conv_s.append(c2)
        hg_s.append(g2)
    return (yp, ys, jnp.stack(ssm_p), jnp.stack(conv_p), jnp.stack(hg_p),
            jnp.stack(ssm_s), jnp.stack(conv_s), jnp.stack(hg_s))
```

```python
import functools

import numpy as np
import jax
import jax.numpy as jnp
from jax import lax
from jax.experimental import pallas as pl
from jax.experimental.pallas import tpu as pltpu

F32 = jnp.float32
BF16 = jnp.bfloat16

D_MODEL = 2048
SSM_D_INNER = 2 * D_MODEL
SSM_HEAD_DIM = 64
SSM_N_HEADS = SSM_D_INNER // SSM_HEAD_DIM
SSM_N_GROUPS = 8
SSM_D_STATE = 128
SSM_CONV_WIDTH = 4
SSM_BC = SSM_N_GROUPS * SSM_D_STATE
SSM_CONV_DIM = SSM_D_INNER + 2 * SSM_BC
HG_KEY_DIM = 128
HG_N_HEADS = D_MODEL // HG_KEY_DIM
HG_VAL_DIM = D_MODEL // HG_N_HEADS
FFN_HIDDEN = 4 * D_MODEL
NORM_EPS = 1e-6

OFF_XBC = SSM_D_INNER
OFF_DT = OFF_XBC + SSM_CONV_DIM
OFF_HQ = OFF_DT + SSM_N_HEADS

LANES = 128
ROWS = 128
GROUP_W = SSM_D_INNER // SSM_N_GROUPS
HEADS_PER_GROUP = SSM_N_HEADS // SSM_N_GROUPS

COL_Z = 0
COL_X = SSM_D_INNER
COL_B = COL_X + SSM_D_INNER
COL_C = COL_B + SSM_BC
COL_Q = COL_C + SSM_BC
COL_F = COL_Q + D_MODEL
COL_I = COL_F + D_MODEL
COL_G = COL_I + D_MODEL
COL_GA = COL_G + D_MODEL
COL_GB = COL_GA + D_MODEL
COL_DT = COL_GB + D_MODEL
PROJ_W = COL_DT + SSM_N_GROUPS * LANES

VMEM_LIMIT = 52 * 1024 * 1024


def _params(*sem):
    return pltpu.CompilerParams(dimension_semantics=sem, vmem_limit_bytes=VMEM_LIMIT)


def _pick(n, cands):
    for c in cands:
        if n % c == 0:
            return c
    raise ValueError(f"no tile for {n} in {cands}")


def _dot(a, b):
    return jnp.dot(a, b, preferred_element_type=F32)


def _dot_nt(a, b):
    return lax.dot_general(a, b, (((1,), (1,)), ((), ())), preferred_element_type=F32)


def _dot_tn(a, b):
    return lax.dot_general(a, b, (((0,), (0,)), ((), ())), preferred_element_type=F32)


def _split3(x):
    hi = x.astype(BF16)
    r1 = x - hi.astype(F32)
    mid = r1.astype(BF16)
    lo = (r1 - mid.astype(F32)).astype(BF16)
    return hi, mid, lo


def _dot3_l(a01, x):
    hi, mid, lo = _split3(x)
    return _dot(a01, hi) + _dot(a01, mid) + _dot(a01, lo)


def _dot3_r(x, b01):
    hi, mid, lo = _split3(x)
    return _dot(hi, b01) + _dot(mid, b01) + _dot(lo, b01)


def _sigmoid(x):
    return 1.0 / (1.0 + jnp.exp(-x))


def _silu(x):
    return x * _sigmoid(x)


def _softplus(x):
    return jnp.maximum(x, 0.0) + jnp.log1p(jnp.exp(-jnp.abs(x)))


def _rms(x, w):
    ms = jnp.mean(x * x, axis=-1, keepdims=True)
    return x * lax.rsqrt(ms + NORM_EPS) * w


def _seq_masks(seq):
    t = np.arange(ROWS)
    same = (t[:, None] // seq) == (t[None, :] // seq)
    lower = same & (t[None, :] <= t[:, None])
    upper = same & (t[None, :] > t[:, None])
    return lower, upper


def _ssd_consts(seq):
    lower, upper = _seq_masks(seq)
    lu = np.concatenate([lower, upper], axis=0).astype(np.float32)
    rep = np.zeros((LANES, GROUP_W), np.float32)
    for j in range(HEADS_PER_GROUP):
        rep[j, j * SSM_HEAD_DIM:(j + 1) * SSM_HEAD_DIM] = 1.0
    return jnp.asarray(lu, BF16), jnp.asarray(rep, BF16)


def _hgrn_consts(seq):
    lower, upper = _seq_masks(seq)
    t = np.arange(ROWS)
    mats, masks = [], []
    b = seq // 2
    while b >= 1:
        r = (t // (2 * b)) * (2 * b) + b - 1
        is_q = t > r
        u = t[None, :]
        a = np.where(is_q[:, None], (u > r[:, None]) & (u <= t[:, None]), (u > t[:, None]) & (u <= r[:, None]))
        pair = (t[:, None] // (2 * b)) == (t[None, :] // (2 * b))
        mats.append(a)
        masks.append(pair & is_q[:, None] & (~is_q)[None, :])
        b //= 2
    masks.append(np.eye(ROWS, dtype=bool))
    a_all = np.concatenate(mats + [lower, upper], axis=0).astype(np.float32)
    m_all = np.stack(masks).astype(np.float32)
    assert np.array_equal(m_all.sum(0) > 0, lower)
    return jnp.asarray(a_all, BF16), jnp.asarray(m_all, F32), len(mats)


def _rmsnorm_body(x_ref, w_ref, o_ref):
    o_ref[...] = _rms(x_ref[...], w_ref[...]).astype(o_ref.dtype)


def _rmsnorm_cast(x, w):
    t, d = x.shape
    tm = _pick(t, (512, 256, 128))
    return pl.pallas_call(
        _rmsnorm_body,
        grid=(t // tm,),
        in_specs=[pl.BlockSpec((tm, d), lambda i: (i, 0)), pl.BlockSpec((1, d), lambda i: (0, 0))],
        out_specs=pl.BlockSpec((tm, d), lambda i: (i, 0)),
        out_shape=jax.ShapeDtypeStruct((t, d), BF16),
        compiler_params=_params("parallel"),
        name="rmsnorm_cast",
    )(x, w)


def _matmul_body(a_ref, b_ref, o_ref, *, act):
    acc = _dot(a_ref[...], b_ref[...])
    if act == "relu2":
        acc = jnp.square(jnp.maximum(acc, 0.0))
    o_ref[...] = acc.astype(o_ref.dtype)


def _matmul(a, b, *, out_dtype, act=None, name):
    m, k = a.shape
    n = b.shape[1]
    tm = _pick(m, (512, 256, 128))
    tn = _pick(n, (1024, 512))
    return pl.pallas_call(
        functools.partial(_matmul_body, act=act),
        grid=(m // tm, n // tn),
        in_specs=[pl.BlockSpec((tm, k), lambda i, j: (i, 0)), pl.BlockSpec((k, tn), lambda i, j: (0, j))],
        out_specs=pl.BlockSpec((tm, tn), lambda i, j: (i, j)),
        out_shape=jax.ShapeDtypeStruct((m, n), out_dtype),
        compiler_params=_params("parallel", "parallel"),
        name=name,
    )(a, b)


def _merge_body(ya_ref, wa_ref, yb_ref, wb_ref, ga_ref, gb_ref, o_ref):
    a = _dot(ya_ref[...], wa_ref[...])
    b = _dot(yb_ref[...], wb_ref[...])
    o_ref[...] = (_sigmoid(ga_ref[...]) * a + _sigmoid(gb_ref[...]) * b).astype(o_ref.dtype)


def _merge(ya, wa, yb, wb, proj):
    t = ya.shape[0]
    tm = _pick(t, (512, 256, 128))
    tn = 512
    return pl.pallas_call(
        _merge_body,
        grid=(t // tm, D_MODEL // tn),
        in_specs=[
            pl.BlockSpec((tm, SSM_D_INNER), lambda i, j: (i, 0)),
            pl.BlockSpec((SSM_D_INNER, tn), lambda i, j: (0, j)),
            pl.BlockSpec((tm, D_MODEL), lambda i, j: (i, 0)),
            pl.BlockSpec((D_MODEL, tn), lambda i, j: (0, j)),
            pl.BlockSpec((tm, tn), lambda i, j: (i, COL_GA // tn + j)),
            pl.BlockSpec((tm, tn), lambda i, j: (i, COL_GB // tn + j)),
        ],
        out_specs=pl.BlockSpec((tm, tn), lambda i, j: (i, j)),
        out_shape=jax.ShapeDtypeStruct((t, D_MODEL), BF16),
        compiler_params=_params("parallel", "parallel"),
        name="merge_branches",
    )(ya, wa, yb, wb, proj, proj)


def _out_body(a_ref, w_ref, x_ref, npost_ref, nnext_ref, x1_ref, h_ref):
    acc = _dot(a_ref[...], w_ref[...])
    x1 = x_ref[...] + _rms(acc, npost_ref[...])
    x1_ref[...] = x1
    h_ref[...] = _rms(x1, nnext_ref[...]).astype(h_ref.dtype)


def _out_proj(a, w, x, npost, nnext):
    t = a.shape[0]
    tm = _pick(t, (256, 128))
    row = lambda i: (i, 0)
    fixed = lambda i: (0, 0)
    return pl.pallas_call(
        _out_body,
        grid=(t // tm,),
        in_specs=[
            pl.BlockSpec((tm, D_MODEL), row),
            pl.BlockSpec((D_MODEL, D_MODEL), fixed),
            pl.BlockSpec((tm, D_MODEL), row),
            pl.BlockSpec((1, D_MODEL), fixed),
            pl.BlockSpec((1, D_MODEL), fixed),
        ],
        out_specs=[pl.BlockSpec((tm, D_MODEL), row), pl.BlockSpec((tm, D_MODEL), row)],
        out_shape=[jax.ShapeDtypeStruct((t, D_MODEL), F32), jax.ShapeDtypeStruct((t, D_MODEL), BF16)],
        compiler_params=_params("parallel"),
        name="out_proj_norm",
    )(a, w, x, npost, nnext)


def _down_body(u_ref, w_ref, x_ref, npost_ref, nnext_ref, x2_ref, h_ref, acc_ref):
    k = pl.program_id(1)

    @pl.when(k == 0)
    def _():
        acc_ref[...] = jnp.zeros_like(acc_ref)

    acc_ref[...] += _dot(u_ref[...], w_ref[...])

    @pl.when(k == pl.num_programs(1) - 1)
    def _():
        x2 = x_ref[...] + _rms(acc_ref[...], npost_ref[...])
        x2_ref[...] = x2
        h_ref[...] = _rms(x2, nnext_ref[...]).astype(h_ref.dtype)


def _ffn_down(u, w, x, npost, nnext):
    t, f = u.shape
    tm = _pick(t, (512, 256, 128))
    tk = 1024
    row = lambda i, k: (i, 0)
    fixed = lambda i, k: (0, 0)
    return pl.pallas_call(
        _down_body,
        grid=(t // tm, f // tk),
        in_specs=[
            pl.BlockSpec((tm, tk), lambda i, k: (i, k)),
            pl.BlockSpec((tk, D_MODEL), lambda i, k: (k, 0)),
            pl.BlockSpec((tm, D_MODEL), row),
            pl.BlockSpec((1, D_MODEL), fixed),
            pl.BlockSpec((1, D_MODEL), fixed),
        ],
        out_specs=[pl.BlockSpec((tm, D_MODEL), row), pl.BlockSpec((tm, D_MODEL), row)],
        out_shape=[jax.ShapeDtypeStruct((t, D_MODEL), F32), jax.ShapeDtypeStruct((t, D_MODEL), BF16)],
        scratch_shapes=[pltpu.VMEM((tm, D_MODEL), F32)],
        compiler_params=_params("parallel", "arbitrary"),
        name="ffn_down_norm",
    )(u, w, x, npost, nnext)


def _ple_body(h_ref, wg_ref, p_ref, wp_ref, x_ref, o_ref):
    g = _sigmoid(_dot(h_ref[...], wg_ref[...]))
    e = _dot(p_ref[...].astype(BF16), wp_ref[...])
    o_ref[...] = x_ref[...] + g * e


def _ple(h, wg, p, wp, x):
    t = h.shape[0]
    pd = p.shape[1]
    tm = _pick(t, (512, 256, 128))
    tn = 512
    return pl.pallas_call(
        _ple_body,
        grid=(t // tm, D_MODEL // tn),
        in_specs=[
            pl.BlockSpec((tm, D_MODEL), lambda i, j: (i, 0)),
            pl.BlockSpec((D_MODEL, tn), lambda i, j: (0, j)),
            pl.BlockSpec((tm, pd), lambda i, j: (i, 0)),
            pl.BlockSpec((pd, tn), lambda i, j: (0, j)),
            pl.BlockSpec((tm, tn), lambda i, j: (i, j)),
        ],
        out_specs=pl.BlockSpec((tm, tn), lambda i, j: (i, j)),
        out_shape=jax.ShapeDtypeStruct((t, D_MODEL), F32),
        compiler_params=_params("parallel", "parallel"),
        name="ple_gate",
    )(h, wg, p, wp, x)


def _conv_taps(cur, shifted, cw, cb):
    acc = cw[SSM_CONV_WIDTH - 1:SSM_CONV_WIDTH, :] * cur
    for d in range(1, SSM_CONV_WIDTH):
        acc = acc + cw[SSM_CONV_WIDTH - 1 - d:SSM_CONV_WIDTH - d, :] * shifted[d]
    return _silu(acc + cb)


def _conv_prompt(cur_ref, halo_sc, cw_ref, cb_ref):
    cur = cur_ref[...]
    top = cur[0:8]
    halo = halo_sc[...]
    row8 = lax.broadcasted_iota(jnp.int32, top.shape, 0)
    shifted, shifted_top = {}, {}
    for d in range(1, SSM_CONV_WIDTH):
        shifted[d] = pltpu.roll(cur, d, 0)
        shifted_top[d] = jnp.where(row8 < d, pltpu.roll(halo, d, 0), pltpu.roll(top, d, 0))
    cw = cw_ref[...]
    cb = cb_ref[...]
    out = _conv_taps(cur, shifted, cw, cb)
    out_top = _conv_taps(top, shifted_top, cw, cb)
    halo_sc[...] = cur[ROWS - 8:ROWS]
    return jnp.concatenate([out_top, out[8:]], axis=0)


def _conv_sample(cur_ref, halo_ref, cw_ref, cb_ref, seq):
    cur = cur_ref[...]
    pos = lax.broadcasted_iota(jnp.int32, cur.shape, 0) & (seq - 1)
    shifted = {}
    for d in range(1, SSM_CONV_WIDTH):
        shifted[d] = jnp.where(pos >= d, pltpu.roll(cur, d, 0), halo_ref[d - 1])
    return _conv_taps(cur, shifted, cw_ref[...], cb_ref[...])


def _ssd_block(xg, bg, cg, dt_ref, dtb_ref, alog_ref, lu_ref, rep_ref, y_sc):
    lu = lu_ref[...]
    mask = lu[0:ROWS].astype(F32) > 0.5
    dt = _softplus(dt_ref[...] + dtb_ref[...])
    da = dt * (-jnp.exp(alog_ref[...]))
    cr = _dot3_l(lu, da)
    cum, rev = cr[0:ROWS], cr[ROWS:2 * ROWS]
    cum_t = cum.T
    dt_t = dt.T
    x_bf = xg.astype(BF16)
    cb = _dot_nt(cg.astype(BF16), bg.astype(BF16))
    for j in range(HEADS_PER_GROUP):
        seg = cum[:, j:j + 1] - cum_t[j:j + 1, :]
        decay = jnp.where(mask, jnp.exp(seg), 0.0)
        m = (cb * decay * dt_t[j:j + 1, :]).astype(BF16)
        y_sc[:, j * SSM_HEAD_DIM:(j + 1) * SSM_HEAD_DIM] = _dot(m, x_bf[:, j * SSM_HEAD_DIM:(j + 1) * SSM_HEAD_DIM])
    slabs = _dot3_r(jnp.concatenate([dt, cum, rev], axis=0), rep_ref[...])
    return cum, slabs[0:ROWS], slabs[ROWS:2 * ROWS], slabs[2 * ROWS:3 * ROWS]


def _head_rows(e_row):
    return jnp.concatenate(
        [jnp.broadcast_to(e_row[0:1, j:j + 1], (SSM_HEAD_DIM, SSM_D_STATE)) for j in range(HEADS_PER_GROUP)], axis=0)


def _gated_group_norm(y, z, nw):
    return _rms(y * _silu(z), nw)


def _ssd_prompt_body(x_ref, b_ref, c_ref, z_ref, dt_ref, cwx_ref, cwb_ref, cwc_ref, cbx_ref, cbb_ref, cbc_ref,
                     dtb_ref, alog_ref, dsk_ref, nw_ref, lu_ref, rep_ref,
                     yn_ref, hout_ref, h_sc, px_sc, pb_sc, pc_sc, y_sc):
    c = pl.program_id(2)

    @pl.when(c == 0)
    def _():
        h_sc[...] = jnp.zeros_like(h_sc)
        px_sc[...] = jnp.zeros_like(px_sc)
        pb_sc[...] = jnp.zeros_like(pb_sc)
        pc_sc[...] = jnp.zeros_like(pc_sc)

    xg = _conv_prompt(x_ref, px_sc, cwx_ref, cbx_ref)
    bg = _conv_prompt(b_ref, pb_sc, cwb_ref, cbb_ref)
    cg = _conv_prompt(c_ref, pc_sc, cwc_ref, cbc_ref)
    cum, dt_rep, cum_rep, rev_rep = _ssd_block(xg, bg, cg, dt_ref, dtb_ref, alog_ref, lu_ref, rep_ref, y_sc)
    h = h_sc[...]
    y_inter = _dot_nt(cg.astype(BF16), h.astype(BF16))
    y = y_sc[...] + y_inter * jnp.exp(cum_rep) + dsk_ref[...] * xg
    yn_ref[...] = _gated_group_norm(y, z_ref[...], nw_ref[...]).astype(yn_ref.dtype)
    xw = (xg * (dt_rep * jnp.exp(rev_rep))).astype(BF16)
    h_new = _head_rows(jnp.exp(cum[ROWS - 1:ROWS, :])) * h + _dot_tn(xw, bg.astype(BF16))
    h_sc[...] = h_new

    @pl.when(c == pl.num_programs(2) - 1)
    def _():
        hout_ref[...] = h_new.reshape(hout_ref.shape)


def _ssd_sample_body(x_ref, b_ref, c_ref, z_ref, dt_ref, hx_ref, hb_ref, hc_ref, st_ref,
                     cwx_ref, cwb_ref, cwc_ref, cbx_ref, cbb_ref, cbc_ref,
                     dtb_ref, alog_ref, dsk_ref, nw_ref, lu_ref, rep_ref,
                     yn_ref, hout_ref, y_sc, yi_sc, cum_sc, *, seq):
    xg = _conv_sample(x_ref, hx_ref, cwx_ref, cbx_ref, seq)
    bg = _conv_sample(b_ref, hb_ref, cwb_ref, cbb_ref, seq)
    cg = _conv_sample(c_ref, hc_ref, cwc_ref, cbc_ref, seq)
    cum, dt_rep, cum_rep, rev_rep = _ssd_block(xg, bg, cg, dt_ref, dtb_ref, alog_ref, lu_ref, rep_ref, y_sc)
    cum_sc[...] = cum
    yi_sc[...] = jnp.zeros_like(yi_sc)
    c_bf = cg.astype(BF16)
    xw = (xg * (dt_rep * jnp.exp(rev_rep))).astype(BF16)
    shift = seq.bit_length() - 1
    seq_of_row_w = lax.shift_right_logical(lax.broadcasted_iota(jnp.int32, (ROWS, GROUP_W), 0), shift)
    seq_of_row_n = lax.shift_right_logical(lax.broadcasted_iota(jnp.int32, (ROWS, SSM_D_STATE), 0), shift)
    state_shape = (HEADS_PER_GROUP * SSM_HEAD_DIM, SSM_D_STATE)

    def per_sequence(s, carry):
        h0 = st_ref[s].reshape(state_shape)
        y_inter = _dot_nt(c_bf, h0.astype(BF16))
        yi_sc[...] += jnp.where(seq_of_row_w == s, y_inter, 0.0)
        b_own = jnp.where(seq_of_row_n == s, bg, 0.0).astype(BF16)
        last = cum_sc[pl.ds(s * seq + seq - 1, 1), :]
        h_new = _head_rows(jnp.exp(last)) * h0 + _dot_tn(xw, b_own)
        hout_ref[s] = h_new.reshape(hout_ref.shape[1:])
        return carry

    lax.fori_loop(0, ROWS // seq, per_sequence, 0)
    y = y_sc[...] + yi_sc[...] * jnp.exp(cum_rep) + dsk_ref[...] * xg
    yn_ref[...] = _gated_group_norm(y, z_ref[...], nw_ref[...]).astype(yn_ref.dtype)


def _ssd_param_specs(gidx):
    def col(block, off):
        return lambda *ids: (0, off // block + gidx(*ids))
    return [
        pl.BlockSpec((SSM_CONV_WIDTH, GROUP_W), col(GROUP_W, 0)),
        pl.BlockSpec((SSM_CONV_WIDTH, SSM_D_STATE), col(SSM_D_STATE, SSM_D_INNER)),
        pl.BlockSpec((SSM_CONV_WIDTH, SSM_D_STATE), col(SSM_D_STATE, SSM_D_INNER + SSM_BC)),
        pl.BlockSpec((1, GROUP_W), col(GROUP_W, 0)),
        pl.BlockSpec((1, SSM_D_STATE), col(SSM_D_STATE, SSM_D_INNER)),
        pl.BlockSpec((1, SSM_D_STATE), col(SSM_D_STATE, SSM_D_INNER + SSM_BC)),
        pl.BlockSpec((1, LANES), col(LANES, 0)),
        pl.BlockSpec((1, LANES), col(LANES, 0)),
        pl.BlockSpec((1, GROUP_W), col(GROUP_W, 0)),
        pl.BlockSpec((1, GROUP_W), col(GROUP_W, 0)),
        pl.BlockSpec((2 * ROWS, ROWS), lambda *ids: (0, 0)),
        pl.BlockSpec((LANES, GROUP_W), lambda *ids: (0, 0)),
    ]


def _ssd_token_specs(ridx, gidx):
    def col(block, off):
        return lambda *ids: (ridx(*ids), off // block + gidx(*ids))
    return [
        pl.BlockSpec((ROWS, GROUP_W), col(GROUP_W, COL_X)),
        pl.BlockSpec((ROWS, SSM_D_STATE), col(SSM_D_STATE, COL_B)),
        pl.BlockSpec((ROWS, SSM_D_STATE), col(SSM_D_STATE, COL_C)),
        pl.BlockSpec((ROWS, GROUP_W), col(GROUP_W, COL_Z)),
        pl.BlockSpec((ROWS, LANES), col(LANES, COL_DT)),
    ]


def _ssd_prompt(proj, yn, ssd_params, batch, length):
    nblk = length // ROWS
    lu, rep = _ssd_consts(ROWS)
    ridx = lambda b, g, c: b * nblk + c
    gidx = lambda b, g, c: g
    conv_w, conv_b, dtb, alog, dsk, nw = ssd_params
    args = [proj] * 5 + [conv_w] * 3 + [conv_b] * 3 + [dtb, alog, dsk, nw, lu, rep, yn]
    n_in = len(args)
    return pl.pallas_call(
        lambda *refs: _ssd_prompt_body(*refs[:n_in - 1], *refs[n_in:]),
        grid=(batch, SSM_N_GROUPS, nblk),
        in_specs=_ssd_token_specs(ridx, gidx) + _ssd_param_specs(gidx) + [pl.BlockSpec(memory_space=pl.ANY)],
        out_specs=[
            pl.BlockSpec((ROWS, GROUP_W), lambda b, g, c: (b * nblk + c, g)),
            pl.BlockSpec((1, HEADS_PER_GROUP, SSM_HEAD_DIM, SSM_D_STATE), lambda b, g, c: (b, g, 0, 0)),
        ],
        out_shape=[
            jax.ShapeDtypeStruct(yn.shape, yn.dtype),
            jax.ShapeDtypeStruct((batch, SSM_N_HEADS, SSM_HEAD_DIM, SSM_D_STATE), F32),
        ],
        scratch_shapes=[
            pltpu.VMEM((HEADS_PER_GROUP * SSM_HEAD_DIM, SSM_D_STATE), F32),
            pltpu.VMEM((8, GROUP_W), F32),
            pltpu.VMEM((8, SSM_D_STATE), F32),
            pltpu.VMEM((8, SSM_D_STATE), F32),
            pltpu.VMEM((ROWS, GROUP_W), F32),
        ],
        input_output_aliases={n_in - 1: 0},
        compiler_params=_params("parallel", "parallel", "arbitrary"),
        name="ssd_prompt",
    )(*args)


def _ssd_sample(proj, yn, halo, state, ssd_params, row0, batch, seq):
    per_blk = ROWS // seq
    nblk = batch // per_blk
    blk0 = row0 // ROWS
    lu, rep = _ssd_consts(seq)
    ridx = lambda m, g: blk0 + m
    gidx = lambda m, g: g
    conv_w, conv_b, dtb, alog, dsk, nw = ssd_params

    def halo_spec(block, off):
        return pl.BlockSpec((SSM_CONV_WIDTH - 1, ROWS, block), lambda m, g: (0, m, off // block + g))

    state_spec = pl.BlockSpec((per_blk, HEADS_PER_GROUP, SSM_HEAD_DIM, SSM_D_STATE), lambda m, g: (m, g, 0, 0))
    args = [proj] * 5 + [halo] * 3 + [state] + [conv_w] * 3 + [conv_b] * 3 + [dtb, alog, dsk, nw, lu, rep, yn]
    n_in = len(args)
    return pl.pallas_call(
        lambda *refs: _ssd_sample_body(*refs[:n_in - 1], *refs[n_in:], seq=seq),
        grid=(nblk, SSM_N_GROUPS),
        in_specs=_ssd_token_specs(ridx, gidx)
        + [halo_spec(GROUP_W, 0), halo_spec(SSM_D_STATE, SSM_D_INNER), halo_spec(SSM_D_STATE, SSM_D_INNER + SSM_BC),
           state_spec]
        + _ssd_param_specs(gidx) + [pl.BlockSpec(memory_space=pl.ANY)],
        out_specs=[pl.BlockSpec((ROWS, GROUP_W), lambda m, g: (blk0 + m, g)), state_spec],
        out_shape=[jax.ShapeDtypeStruct(yn.shape, yn.dtype), jax.ShapeDtypeStruct(state.shape, F32)],
        scratch_shapes=[
            pltpu.VMEM((ROWS, GROUP_W), F32),
            pltpu.VMEM((ROWS, GROUP_W), F32),
            pltpu.VMEM((ROWS, LANES), F32),
        ],
        input_output_aliases={n_in - 1: 0},
        compiler_params=_params("parallel", "parallel"),
        name="ssd_sample",
    )(*args)


def _lower_bound(lbraw_ref, layer):
    raw = lbraw_ref[...]
    e = jnp.exp(raw - jnp.max(raw, axis=0, keepdims=True))
    return jnp.sum(e[0:layer + 1], axis=0, keepdims=True) / jnp.sum(e, axis=0, keepdims=True)


def _hgrn_block(q_ref, f_ref, i_ref, lbraw_ref, a_ref, m_ref, layer, n_levels):
    q = q_ref[...]
    hf = f_ref[...]
    v_bf = i_ref[...].astype(BF16)
    lb = _lower_bound(lbraw_ref, layer)
    log_f = jnp.log(lb + (1.0 - lb) * _sigmoid(hf))
    k = (1.0 - lb) * _sigmoid(-hf)
    sums = _dot3_l(a_ref[...], log_f)
    att = m_ref[n_levels] * _dot_nt(q.astype(BF16), k.astype(BF16))
    for lvl in range(n_levels):
        e = jnp.exp(sums[lvl * ROWS:(lvl + 1) * ROWS])
        att = att + m_ref[lvl] * _dot_nt((q * e).astype(BF16), (k * e).astype(BF16))
    cum = sums[n_levels * ROWS:(n_levels + 1) * ROWS]
    rev = sums[(n_levels + 1) * ROWS:(n_levels + 2) * ROWS]
    o_intra = _dot(att.astype(BF16), v_bf)
    q_dec = (q * jnp.exp(cum)).astype(BF16)
    k_dec = (k * jnp.exp(rev)).astype(BF16)
    return o_intra, q_dec, k_dec, v_bf, cum


def _hgrn_out(o, g_ref, nw_ref, on_ref):
    on_ref[...] = (_rms(o, nw_ref[...]) * _silu(g_ref[...])).astype(on_ref.dtype)


def _hgrn_prompt_body(q_ref, f_ref, i_ref, g_ref, lbraw_ref, nw_ref, a_ref, m_ref,
                      on_ref, sout_ref, s_sc, *, layer, n_levels):
    c = pl.program_id(2)

    @pl.when(c == 0)
    def _():
        s_sc[...] = jnp.zeros_like(s_sc)

    o_intra, q_dec, k_dec, v_bf, cum = _hgrn_block(q_ref, f_ref, i_ref, lbraw_ref, a_ref, m_ref, layer, n_levels)
    s = s_sc[...]
    _hgrn_out(o_intra + _dot(q_dec, s.astype(BF16)), g_ref, nw_ref, on_ref)
    last_col = cum.T[:, ROWS - 1:ROWS]
    s_new = jnp.exp(last_col) * s + _dot_tn(k_dec, v_bf)
    s_sc[...] = s_new

    @pl.when(c == pl.num_programs(2) - 1)
    def _():
        sout_ref[...] = s_new.reshape(sout_ref.shape)


def _hgrn_sample_body(q_ref, f_ref, i_ref, g_ref, st_ref, lbraw_ref, nw_ref, a_ref, m_ref,
                      on_ref, sout_ref, o_sc, cumt_sc, *, layer, n_levels, seq):
    o_intra, q_dec, k_dec, v_bf, cum = _hgrn_block(q_ref, f_ref, i_ref, lbraw_ref, a_ref, m_ref, layer, n_levels)
    o_sc[...] = o_intra
    cumt_sc[...] = cum.T
    v = v_bf.astype(F32)
    shift = seq.bit_length() - 1
    row = lax.broadcasted_iota(jnp.int32, (ROWS, HG_VAL_DIM), 0)
    seq_of_row = lax.shift_right_logical(row, shift)
    lane = lax.broadcasted_iota(jnp.int32, (HG_KEY_DIM, ROWS), 1)

    def per_sequence(s, carry):
        s0 = st_ref[s, 0]
        own = seq_of_row == s
        o_sc[...] += jnp.where(own, _dot(q_dec, s0.astype(BF16)), 0.0)
        v_own = jnp.where(own, v, 0.0).astype(BF16)
        last_col = jnp.sum(jnp.where(lane == s * seq + seq - 1, cumt_sc[...], 0.0), axis=1, keepdims=True)
        sout_ref[s, 0] = jnp.exp(last_col) * s0 + _dot_tn(k_dec, v_own)
        return carry

    lax.fori_loop(0, ROWS // seq, per_sequence, 0)
    _hgrn_out(o_sc[...], g_ref, nw_ref, on_ref)


def _hgrn_token_specs(ridx, hidx):
    def col(off):
        return lambda *ids: (ridx(*ids), off // HG_KEY_DIM + hidx(*ids))
    return [pl.BlockSpec((ROWS, HG_KEY_DIM), col(off)) for off in (COL_Q, COL_F, COL_I, COL_G)]


def _hgrn_param_specs(hidx, n_rows, n_levels):
    return [
        pl.BlockSpec((n_rows, HG_KEY_DIM), lambda *ids: (0, hidx(*ids))),
        pl.BlockSpec((1, HG_VAL_DIM), lambda *ids: (0, 0)),
        pl.BlockSpec(((n_levels + 2) * ROWS, ROWS), lambda *ids: (0, 0)),
        pl.BlockSpec((n_levels + 1, ROWS, ROWS), lambda *ids: (0, 0, 0)),
    ]


def _hgrn_prompt(proj, on, hg_lb, hg_norm, layer, batch, length):
    nblk = length // ROWS
    a_all, m_all, n_levels = _hgrn_consts(ROWS)
    ridx = lambda b, h, c: b * nblk + c
    hidx = lambda b, h, c: h
    args = [proj] * 4 + [hg_lb, hg_norm, a_all, m_all, on]
    n_in = len(args)
    return pl.pallas_call(
        lambda *refs: functools.partial(_hgrn_prompt_body, layer=layer, n_levels=n_levels)(
            *refs[:n_in - 1], *refs[n_in:]),
        grid=(batch, HG_N_HEADS, nblk),
        in_specs=_hgrn_token_specs(ridx, hidx) + _hgrn_param_specs(hidx, hg_lb.shape[0], n_levels)
        + [pl.BlockSpec(memory_space=pl.ANY)],
        out_specs=[
            pl.BlockSpec((ROWS, HG_VAL_DIM), lambda b, h, c: (b * nblk + c, h)),
            pl.BlockSpec((1, 1, HG_KEY_DIM, HG_VAL_DIM), lambda b, h, c: (b, h, 0, 0)),
        ],
        out_shape=[
            jax.ShapeDtypeStruct(on.shape, on.dtype),
            jax.ShapeDtypeStruct((batch, HG_N_HEADS, HG_KEY_DIM, HG_VAL_DIM), F32),
        ],
        scratch_shapes=[pltpu.VMEM((HG_KEY_DIM, HG_VAL_DIM), F32)],
        input_output_aliases={n_in - 1: 0},
        compiler_params=_params("parallel", "parallel", "arbitrary"),
        name="hgrn_prompt",
    )(*args)


def _hgrn_sample(proj, on, state, hg_lb, hg_norm, layer, row0, batch, seq):
    per_blk = ROWS // seq
    nblk = batch // per_blk
    blk0 = row0 // ROWS
    a_all, m_all, n_levels = _hgrn_consts(seq)
    ridx = lambda m, h: blk0 + m
    hidx = lambda m, h: h
    state_spec = pl.BlockSpec((per_blk, 1, HG_KEY_DIM, HG_VAL_DIM), lambda m, h: (m, h, 0, 0))
    args = [proj] * 4 + [state, hg_lb, hg_norm, a_all, m_all, on]
    n_in = len(args)
    return pl.pallas_call(
        lambda *refs: functools.partial(_hgrn_sample_body, layer=layer, n_levels=n_levels, seq=seq)(
            *refs[:n_in - 1], *refs[n_in:]),
        grid=(nblk, HG_N_HEADS),
        in_specs=_hgrn_token_specs(ridx, hidx) + [state_spec]
        + _hgrn_param_specs(hidx, hg_lb.shape[0], n_levels) + [pl.BlockSpec(memory_space=pl.ANY)],
        out_specs=[pl.BlockSpec((ROWS, HG_VAL_DIM), lambda m, h: (blk0 + m, h)), state_spec],
        out_shape=[jax.ShapeDtypeStruct(on.shape, on.dtype), jax.ShapeDtypeStruct(state.shape, F32)],
        scratch_shapes=[pltpu.VMEM((ROWS, HG_VAL_DIM), F32), pltpu.VMEM((HG_KEY_DIM, ROWS), F32)],
        input_output_aliases={n_in - 1: 0},
        compiler_params=_params("parallel", "parallel"),
        name="hgrn_sample",
    )(*args)


def _repack_w_in(w):
    wdt = w[:, OFF_DT:OFF_HQ].reshape(D_MODEL, SSM_N_GROUPS, HEADS_PER_GROUP)
    wdt = jnp.pad(wdt, ((0, 0), (0, 0), (0, LANES - HEADS_PER_GROUP))).reshape(D_MODEL, SSM_N_GROUPS * LANES)
    return jnp.concatenate([w[:, :OFF_DT], w[:, OFF_HQ:], wdt], axis=1).astype(BF16)


def _head_slabs(v):
    v = v.reshape(SSM_N_GROUPS, HEADS_PER_GROUP).astype(F32)
    return jnp.pad(v, ((0, 0), (0, LANES - HEADS_PER_GROUP))).reshape(1, SSM_N_GROUPS * LANES)


def _sample_halo(conv0, seq):
    bsz, wm1, cd = conv0.shape
    out = []
    for d in range(1, SSM_CONV_WIDTH):
        out.append(jnp.pad(conv0[:, wm1 - d:, :], ((0, 0), (0, seq - d), (0, 0))).reshape(bsz * seq, cd))
    return jnp.stack(out)


def kernel(x_prompt, x_sample, p_prompt, p_sample, state_ssm, state_conv, state_hgrn, norm_mix_pre, w_in, conv_w, conv_b, dt_bias, a_log, d_skip, ssm_norm, w_br_a, hg_lb, hg_norm, w_br_b, w_out, norm_mix_post, norm_ffn_pre, w_up, w_down, norm_ffn_post, norm_ple, w_ple_gate, w_ple_proj):
    depth = w_in.shape[0]
    bp, lp, _ = x_prompt.shape
    bs, ls, _ = x_sample.shape
    tp, ts = bp * lp, bs * ls
    assert lp % ROWS == 0 and ROWS % ls == 0 and ls & (ls - 1) == 0 and ts % ROWS == 0 and tp % ROWS == 0
    assert lp >= SSM_CONV_WIDTH - 1 and ls >= SSM_CONV_WIDTH - 1

    x = jnp.concatenate([x_prompt.reshape(tp, D_MODEL), x_sample.reshape(ts, D_MODEL)], axis=0)
    row = lambda a: a.reshape(1, -1).astype(F32)
    outs = {k: [] for k in ("ssm_p", "conv_p", "hg_p", "ssm_s", "conv_s", "hg_s")}
    for li in range(depth):
        p = jnp.concatenate([p_prompt[li].reshape(tp, -1), p_sample[li].reshape(ts, -1)], axis=0)
        h = _rmsnorm_cast(x, row(norm_mix_pre[li]))
        proj = _matmul(h, _repack_w_in(w_in[li]), out_dtype=F32, name="in_proj")

        xbc_p = proj[:tp, COL_X:COL_Q].reshape(bp, lp, SSM_CONV_DIM)
        xbc_s = proj[tp:, COL_X:COL_Q].reshape(bs, ls, SSM_CONV_DIM)
        outs["conv_p"].append(xbc_p[:, lp - (SSM_CONV_WIDTH - 1):])
        outs["conv_s"].append(xbc_s[:, ls - (SSM_CONV_WIDTH - 1):])

        ssd_params = (conv_w[li], row(conv_b[li]), _head_slabs(dt_bias[li]), _head_slabs(a_log[li]),
                      row(jnp.repeat(d_skip[li], SSM_HEAD_DIM)), row(ssm_norm[li]))
        yn = jnp.zeros((tp + ts, SSM_D_INNER), BF16)
        yn, ssm_p = _ssd_prompt(proj, yn, ssd_params, bp, lp)
        yn, ssm_s = _ssd_sample(proj, yn, _sample_halo(state_conv[li], ls), state_ssm[li], ssd_params, tp, bs, ls)
        on = jnp.zeros((tp + ts, D_MODEL), BF16)
        on, hg_p = _hgrn_prompt(proj, on, hg_lb, row(hg_norm[li]), li, bp, lp)
        on, hg_s = _hgrn_sample(proj, on, state_hgrn[li], hg_lb, row(hg_norm[li]), li, tp, bs, ls)
        outs["ssm_p"].append(ssm_p)
        outs["ssm_s"].append(ssm_s)
        outs["hg_p"].append(hg_p)
        outs["hg_s"].append(hg_s)

        mixed = _merge(yn, w_br_a[li].astype(BF16), on, w_br_b[li].astype(BF16), proj)
        x1, h2 = _out_proj(mixed, w_out[li].astype(BF16), x, row(norm_mix_post[li]), row(norm_ffn_pre[li]))
        u = _matmul(h2, w_up[li].astype(BF16), out_dtype=BF16, act="relu2", name="ffn_up")
        x2, h3 = _ffn_down(u, w_down[li].astype(BF16), x1, row(norm_ffn_post[li]), row(norm_ple[li]))
        x = _ple(h3, w_ple_gate[li].astype(BF16), p, w_ple_proj[li].astype(BF16), x2)

    stack = lambda k: jnp.stack(outs[k])
    return (x[:tp].reshape(bp, lp, D_MODEL), x[tp:].reshape(bs, ls, D_MODEL),
            stack("ssm_p"), stack("conv_p"), stack("hg_p"), stack("ssm_s"), stack("conv_s"), stack("hg_s"))
```

```python
import functools
import math

import numpy as np
import jax
import jax.numpy as jnp
from jax import lax
from jax.experimental import pallas as pl
from jax.experimental.pallas import tpu as pltpu

F32 = jnp.float32
BF16 = jnp.bfloat16

D_MODEL = 2048
SSM_D_INNER = 2 * D_MODEL
SSM_HEAD_DIM = 64
SSM_N_HEADS = SSM_D_INNER // SSM_HEAD_DIM
SSM_N_GROUPS = 8
SSM_D_STATE = 128
SSM_CONV_WIDTH = 4
SSM_BC = SSM_N_GROUPS * SSM_D_STATE
SSM_CONV_DIM = SSM_D_INNER + 2 * SSM_BC
HG_KEY_DIM = 128
HG_N_HEADS = D_MODEL // HG_KEY_DIM
HG_VAL_DIM = D_MODEL // HG_N_HEADS
FFN_HIDDEN = 4 * D_MODEL
NORM_EPS = 1e-6

OFF_XBC = SSM_D_INNER
OFF_DT = OFF_XBC + SSM_CONV_DIM
OFF_HQ = OFF_DT + SSM_N_HEADS

LANES = 128
ROWS = 128
GROUP_W = SSM_D_INNER // SSM_N_GROUPS
HEADS_PER_GROUP = SSM_N_HEADS // SSM_N_GROUPS
HG_HEADS_PER_STEP = 4

COL_Z = 0
COL_X = SSM_D_INNER
COL_B = COL_X + SSM_D_INNER
COL_C = COL_B + SSM_BC
COL_Q = COL_C + SSM_BC
COL_F = COL_Q + D_MODEL
COL_I = COL_F + D_MODEL
COL_G = COL_I + D_MODEL
COL_GA = COL_G + D_MODEL
COL_GB = COL_GA + D_MODEL
COL_DT = COL_GB + D_MODEL
PROJ_W = COL_DT + SSM_N_GROUPS * LANES

VMEM_LIMIT = 52 * 1024 * 1024


def _params(*sem):
    return pltpu.CompilerParams(dimension_semantics=sem, vmem_limit_bytes=VMEM_LIMIT)


def _pick(n, cands):
    for c in cands:
        if n % c == 0:
            return c
    raise ValueError(f"no tile for {n} in {cands}")


def _dot(a, b):
    return jnp.dot(a, b, preferred_element_type=F32)


def _dot_nt(a, b):
    return lax.dot_general(a, b, (((1,), (1,)), ((), ())), preferred_element_type=F32)


def _dot_tn(a, b):
    return lax.dot_general(a, b, (((0,), (0,)), ((), ())), preferred_element_type=F32)


def _split3(x):
    hi = x.astype(BF16)
    r1 = x - hi.astype(F32)
    mid = r1.astype(BF16)
    lo = (r1 - mid.astype(F32)).astype(BF16)
    return hi, mid, lo


def _dot3_l(a01, x):
    hi, mid, lo = _split3(x)
    return _dot(a01, hi) + _dot(a01, mid) + _dot(a01, lo)


def _dot3_r(x, b01):
    hi, mid, lo = _split3(x)
    return _dot(hi, b01) + _dot(mid, b01) + _dot(lo, b01)


def _sigmoid(x):
    return 1.0 / (1.0 + jnp.exp(-x))


def _silu(x):
    return x * _sigmoid(x)


def _softplus(x):
    return jnp.maximum(x, 0.0) + jnp.log1p(jnp.exp(-jnp.abs(x)))


def _rms(x, w):
    ms = jnp.mean(x * x, axis=-1, keepdims=True)
    return x * lax.rsqrt(ms + NORM_EPS) * w


def _seq_masks(seq):
    t = np.arange(ROWS)
    same = (t[:, None] // seq) == (t[None, :] // seq)
    lower = same & (t[None, :] <= t[:, None])
    upper = same & (t[None, :] > t[:, None])
    return lower, upper


def _ssd_consts(seq):
    lower, upper = _seq_masks(seq)
    lu = np.concatenate([lower, upper], axis=0).astype(np.float32)
    rep = np.zeros((LANES, GROUP_W), np.float32)
    for j in range(HEADS_PER_GROUP):
        rep[j, j * SSM_HEAD_DIM:(j + 1) * SSM_HEAD_DIM] = 1.0
    return jnp.asarray(lu, BF16), jnp.asarray(rep, BF16)


def _hgrn_consts(seq):
    lower, upper = _seq_masks(seq)
    t = np.arange(ROWS)
    mats, masks = [], []
    b = seq // 2
    while b >= 1:
        r = (t // (2 * b)) * (2 * b) + b - 1
        is_q = t > r
        u = t[None, :]
        a = np.where(is_q[:, None], (u > r[:, None]) & (u <= t[:, None]), (u > t[:, None]) & (u <= r[:, None]))
        pair = (t[:, None] // (2 * b)) == (t[None, :] // (2 * b))
        mats.append(a)
        masks.append(pair & is_q[:, None] & (~is_q)[None, :])
        b //= 2
    masks.append(np.eye(ROWS, dtype=bool))
    a_all = np.concatenate(mats + [lower, upper], axis=0).astype(np.float32)
    m_all = np.stack(masks).astype(np.float32)
    assert np.array_equal(m_all.sum(0) > 0, lower)
    return jnp.asarray(a_all, BF16), jnp.asarray(m_all, F32), len(mats)


def _group_specs(tm, n_prompt_blocks, width):
    last = n_prompt_blocks - 1
    return [
        pl.BlockSpec((tm, width), lambda i, *_: (jnp.minimum(i, last), 0)),
        pl.BlockSpec((tm, width), lambda i, *_: (jnp.maximum(i - n_prompt_blocks, 0), 0)),
    ]


def _group_rows(xp_ref, xs_ref, n_prompt_blocks):
    return jnp.where(pl.program_id(0) < n_prompt_blocks, xp_ref[...], xs_ref[...])


def _rmsnorm_body(xp_ref, xs_ref, w_ref, o_ref, *, n_prompt_blocks):
    o_ref[...] = _rms(_group_rows(xp_ref, xs_ref, n_prompt_blocks), w_ref[...]).astype(o_ref.dtype)


def _rmsnorm_cast(xp, xs, w):
    tp, d = xp.shape
    ts = xs.shape[0]
    tm = _pick(math.gcd(tp, ts), (512, 256, 128))
    return pl.pallas_call(
        functools.partial(_rmsnorm_body, n_prompt_blocks=tp // tm),
        grid=((tp + ts) // tm,),
        in_specs=_group_specs(tm, tp // tm, d) + [pl.BlockSpec((1, d), lambda i: (0, 0))],
        out_specs=pl.BlockSpec((tm, d), lambda i: (i, 0)),
        out_shape=jax.ShapeDtypeStruct((tp + ts, d), BF16),
        compiler_params=_params("parallel"),
        name="rmsnorm_cast",
    )(xp, xs, w)


def _repack_body(w_ref, o_ref):
    chunk = 2048
    for c0 in range(0, OFF_DT, chunk):
        o_ref[:, c0:c0 + chunk] = w_ref[:, c0:c0 + chunk].astype(BF16)
    for c0 in range(0, COL_DT - OFF_DT, chunk):
        o_ref[:, OFF_DT + c0:OFF_DT + c0 + chunk] = w_ref[:, OFF_HQ + c0:OFF_HQ + c0 + chunk].astype(BF16)
    dt_cols = w_ref[:, OFF_DT:OFF_DT + LANES]
    lane = lax.broadcasted_iota(jnp.int32, dt_cols.shape, 1)
    for g in range(SSM_N_GROUPS):
        moved = dt_cols if g == 0 else pltpu.roll(dt_cols, LANES - g * HEADS_PER_GROUP, 1)
        o_ref[:, COL_DT + g * LANES:COL_DT + (g + 1) * LANES] = (
            jnp.where(lane < HEADS_PER_GROUP, moved, 0.0).astype(BF16))


def _repack_w_in(w):
    k, n = w.shape
    tk = 32
    return pl.pallas_call(
        _repack_body,
        grid=(k // tk,),
        in_specs=[pl.BlockSpec((tk, n), lambda i: (i, 0))],
        out_specs=pl.BlockSpec((tk, PROJ_W), lambda i: (i, 0)),
        out_shape=jax.ShapeDtypeStruct((k, PROJ_W), BF16),
        compiler_params=_params("parallel"),
        name="repack_w_in",
    )(w)


def _matmul_body(a_ref, b_ref, o_ref, *, act):
    acc = _dot(a_ref[...], b_ref[...])
    if act == "relu2":
        acc = jnp.square(jnp.maximum(acc, 0.0))
    o_ref[...] = acc.astype(o_ref.dtype)


def _matmul(a, b, *, out_dtype, act=None, name):
    m, k = a.shape
    n = b.shape[1]
    tm = _pick(m, (1088, 512, 256, 128))
    tn = _pick(n, (1024, 512))
    return pl.pallas_call(
        functools.partial(_matmul_body, act=act),
        grid=(m // tm, n // tn),
        in_specs=[pl.BlockSpec((tm, k), lambda i, j: (i, 0)), pl.BlockSpec((k, tn), lambda i, j: (0, j))],
        out_specs=pl.BlockSpec((tm, tn), lambda i, j: (i, j)),
        out_shape=jax.ShapeDtypeStruct((m, n), out_dtype),
        compiler_params=_params("parallel", "parallel"),
        name=name,
    )(a, b)


def _merge_body(ya_ref, wa_ref, yb_ref, wb_ref, ga_ref, gb_ref, o_ref):
    a = _dot(ya_ref[...], wa_ref[...])
    b = _dot(yb_ref[...], wb_ref[...])
    o_ref[...] = (_sigmoid(ga_ref[...]) * a + _sigmoid(gb_ref[...]) * b).astype(o_ref.dtype)


def _merge(ya, wa, yb, wb, proj):
    t = ya.shape[0]
    tm = _pick(t, (544, 512, 256, 128))
    tn = 512
    return pl.pallas_call(
        _merge_body,
        grid=(t // tm, D_MODEL // tn),
        in_specs=[
            pl.BlockSpec((tm, SSM_D_INNER), lambda i, j: (i, 0)),
            pl.BlockSpec((SSM_D_INNER, tn), lambda i, j: (0, j)),
            pl.BlockSpec((tm, D_MODEL), lambda i, j: (i, 0)),
            pl.BlockSpec((D_MODEL, tn), lambda i, j: (0, j)),
            pl.BlockSpec((tm, tn), lambda i, j: (i, COL_GA // tn + j)),
            pl.BlockSpec((tm, tn), lambda i, j: (i, COL_GB // tn + j)),
        ],
        out_specs=pl.BlockSpec((tm, tn), lambda i, j: (i, j)),
        out_shape=jax.ShapeDtypeStruct((t, D_MODEL), BF16),
        compiler_params=_params("parallel", "parallel"),
        name="merge_branches",
    )(ya, wa, yb, wb, proj, proj)


def _out_body(a_ref, w_ref, xp_ref, xs_ref, npost_ref, nnext_ref, x1_ref, h_ref, *, n_prompt_blocks):
    acc = _dot(a_ref[...], w_ref[...])
    x1 = _group_rows(xp_ref, xs_ref, n_prompt_blocks) + _rms(acc, npost_ref[...])
    x1_ref[...] = x1
    h_ref[...] = _rms(x1, nnext_ref[...]).astype(h_ref.dtype)


def _out_proj(a, w, xp, xs, npost, nnext):
    t = a.shape[0]
    tp, ts = xp.shape[0], xs.shape[0]
    tm = _pick(math.gcd(tp, ts), (256, 128))
    row = lambda i: (i, 0)
    fixed = lambda i: (0, 0)
    return pl.pallas_call(
        functools.partial(_out_body, n_prompt_blocks=tp // tm),
        grid=(t // tm,),
        in_specs=[pl.BlockSpec((tm, D_MODEL), row), pl.BlockSpec((D_MODEL, D_MODEL), fixed)]
        + _group_specs(tm, tp // tm, D_MODEL)
        + [pl.BlockSpec((1, D_MODEL), fixed), pl.BlockSpec((1, D_MODEL), fixed)],
        out_specs=[pl.BlockSpec((tm, D_MODEL), row), pl.BlockSpec((tm, D_MODEL), row)],
        out_shape=[jax.ShapeDtypeStruct((t, D_MODEL), F32), jax.ShapeDtypeStruct((t, D_MODEL), BF16)],
        compiler_params=_params("parallel"),
        name="out_proj_norm",
    )(a, w, xp, xs, npost, nnext)


def _down_body(u_ref, w_ref, x_ref, npost_ref, nnext_ref, x2_ref, h_ref, acc_ref):
    k = pl.program_id(1)

    @pl.when(k == 0)
    def _():
        acc_ref[...] = jnp.zeros_like(acc_ref)

    acc_ref[...] += _dot(u_ref[...], w_ref[...])

    @pl.when(k == pl.num_programs(1) - 1)
    def _():
        x2 = x_ref[...] + _rms(acc_ref[...], npost_ref[...])
        x2_ref[...] = x2
        h_ref[...] = _rms(x2, nnext_ref[...]).astype(h_ref.dtype)


def _ffn_down(u, w, x, npost, nnext):
    t, f = u.shape
    tm = _pick(t, (544, 512, 256, 128))
    tk = 1024
    row = lambda i, k: (i, 0)
    fixed = lambda i, k: (0, 0)
    return pl.pallas_call(
        _down_body,
        grid=(t // tm, f // tk),
        in_specs=[
            pl.BlockSpec((tm, tk), lambda i, k: (i, k)),
            pl.BlockSpec((tk, D_MODEL), lambda i, k: (k, 0)),
            pl.BlockSpec((tm, D_MODEL), row),
            pl.BlockSpec((1, D_MODEL), fixed),
            pl.BlockSpec((1, D_MODEL), fixed),
        ],
        out_specs=[pl.BlockSpec((tm, D_MODEL), row), pl.BlockSpec((tm, D_MODEL), row)],
        out_shape=[jax.ShapeDtypeStruct((t, D_MODEL), F32), jax.ShapeDtypeStruct((t, D_MODEL), BF16)],
        scratch_shapes=[pltpu.VMEM((tm, D_MODEL), F32)],
        compiler_params=_params("parallel", "arbitrary"),
        name="ffn_down_norm",
    )(u, w, x, npost, nnext)


def _ple_body(h_ref, wg_ref, pp_ref, ps_ref, wp_ref, x_ref, yp_ref, ys_ref, *, n_prompt_blocks):
    i = pl.program_id(0)
    g = _sigmoid(_dot(h_ref[...], wg_ref[...]))
    e = _dot(_group_rows(pp_ref, ps_ref, n_prompt_blocks).astype(BF16), wp_ref[...])
    y = x_ref[...] + g * e

    @pl.when(i < n_prompt_blocks)
    def _():
        yp_ref[...] = y

    @pl.when(i >= n_prompt_blocks)
    def _():
        ys_ref[...] = y


def _ple(h, wg, pp, ps, wp, x):
    tp, pd = pp.shape
    ts = ps.shape[0]
    tm = _pick(math.gcd(tp, ts), (256, 128))
    npb = tp // tm
    row = lambda i: (i, 0)
    fixed = lambda i: (0, 0)
    return pl.pallas_call(
        functools.partial(_ple_body, n_prompt_blocks=npb),
        grid=((tp + ts) // tm,),
        in_specs=[pl.BlockSpec((tm, D_MODEL), row), pl.BlockSpec((D_MODEL, D_MODEL), fixed)]
        + _group_specs(tm, npb, pd)
        + [pl.BlockSpec((pd, D_MODEL), fixed), pl.BlockSpec((tm, D_MODEL), row)],
        out_specs=_group_specs(tm, npb, D_MODEL),
        out_shape=[jax.ShapeDtypeStruct((tp, D_MODEL), F32), jax.ShapeDtypeStruct((ts, D_MODEL), F32)],
        compiler_params=_params("arbitrary"),
        name="ple_gate",
    )(h, wg, pp, ps, wp, x)


def _conv_taps(cur, shifted, cw, cb):
    acc = cw[SSM_CONV_WIDTH - 1:SSM_CONV_WIDTH, :] * cur
    for d in range(1, SSM_CONV_WIDTH):
        acc = acc + cw[SSM_CONV_WIDTH - 1 - d:SSM_CONV_WIDTH - d, :] * shifted[d]
    return _silu(acc + cb)


def _conv_prompt(cur_ref, halo_sc, cw_ref, cb_ref):
    cur = cur_ref[...]
    top = cur[0:8]
    halo = halo_sc[...]
    row8 = lax.broadcasted_iota(jnp.int32, top.shape, 0)
    shifted, shifted_top = {}, {}
    for d in range(1, SSM_CONV_WIDTH):
        shifted[d] = pltpu.roll(cur, d, 0)
        shifted_top[d] = jnp.where(row8 < d, pltpu.roll(halo, d, 0), pltpu.roll(top, d, 0))
    cw = cw_ref[...]
    cb = cb_ref[...]
    out = _conv_taps(cur, shifted, cw, cb)
    out_top = _conv_taps(top, shifted_top, cw, cb)
    halo_sc[...] = cur[ROWS - 8:ROWS]
    return jnp.concatenate([out_top, out[8:]], axis=0)


def _conv_sample(cur_ref, carry_ref, cw_ref, cb_ref, seq):
    cur = cur_ref[...]
    carry = carry_ref[...]
    pos = lax.broadcasted_iota(jnp.int32, cur.shape, 0) & (seq - 1)
    shifted = {}
    for d in range(1, SSM_CONV_WIDTH):
        shifted[d] = jnp.where(pos >= d, pltpu.roll(cur, d, 0), pltpu.roll(carry, ROWS - (seq - d), 0))
    return _conv_taps(cur, shifted, cw_ref[...], cb_ref[...])


def _ssd_block(xg, bg, cg, dt_ref, dtb_ref, alog_ref, lu_ref, rep_ref, y_sc):
    lu = lu_ref[...]
    mask = lu[0:ROWS].astype(F32) > 0.5
    dt = _softplus(dt_ref[...] + dtb_ref[...])
    da = dt * (-jnp.exp(alog_ref[...]))
    cr = _dot3_l(lu, da)
    cum, rev = cr[0:ROWS], cr[ROWS:2 * ROWS]
    cum_t = cum.T
    dt_t = dt.T
    x_bf = xg.astype(BF16)
    cb = _dot_nt(cg.astype(BF16), bg.astype(BF16))
    for j in range(HEADS_PER_GROUP):
        seg = cum[:, j:j + 1] - cum_t[j:j + 1, :]
        decay = jnp.where(mask, jnp.exp(seg), 0.0)
        m = (cb * decay * dt_t[j:j + 1, :]).astype(BF16)
        y_sc[:, j * SSM_HEAD_DIM:(j + 1) * SSM_HEAD_DIM] = _dot(m, x_bf[:, j * SSM_HEAD_DIM:(j + 1) * SSM_HEAD_DIM])
    slabs = _dot3_r(jnp.concatenate([dt, cum, rev], axis=0), rep_ref[...])
    return cum, slabs[0:ROWS], slabs[ROWS:2 * ROWS], slabs[2 * ROWS:3 * ROWS]


def _head_rows(e_row):
    return jnp.concatenate(
        [jnp.broadcast_to(e_row[0:1, j:j + 1], (SSM_HEAD_DIM, SSM_D_STATE)) for j in range(HEADS_PER_GROUP)], axis=0)


def _gated_group_norm(y, z, nw):
    return _rms(y * _silu(z), nw)


def _ssd_prompt_body(x_ref, b_ref, c_ref, z_ref, dt_ref, cwx_ref, cwb_ref, cwc_ref, cbx_ref, cbb_ref, cbc_ref,
                     dtb_ref, alog_ref, dsk_ref, nw_ref, lu_ref, rep_ref,
                     yn_ref, hout_ref, h_sc, px_sc, pb_sc, pc_sc, y_sc):
    c = pl.program_id(2)

    @pl.when(c == 0)
    def _():
        h_sc[...] = jnp.zeros_like(h_sc)
        px_sc[...] = jnp.zeros_like(px_sc)
        pb_sc[...] = jnp.zeros_like(pb_sc)
        pc_sc[...] = jnp.zeros_like(pc_sc)

    xg = _conv_prompt(x_ref, px_sc, cwx_ref, cbx_ref)
    bg = _conv_prompt(b_ref, pb_sc, cwb_ref, cbb_ref)
    cg = _conv_prompt(c_ref, pc_sc, cwc_ref, cbc_ref)
    cum, dt_rep, cum_rep, rev_rep = _ssd_block(xg, bg, cg, dt_ref, dtb_ref, alog_ref, lu_ref, rep_ref, y_sc)
    h = h_sc[...]
    y_inter = _dot_nt(cg.astype(BF16), h.astype(BF16))
    y = y_sc[...] + y_inter * jnp.exp(cum_rep) + dsk_ref[...] * xg
    yn_ref[...] = _gated_group_norm(y, z_ref[...], nw_ref[...]).astype(yn_ref.dtype)
    xw = (xg * (dt_rep * jnp.exp(rev_rep))).astype(BF16)
    h_new = _head_rows(jnp.exp(cum[ROWS - 1:ROWS, :])) * h + _dot_tn(xw, bg.astype(BF16))
    h_sc[...] = h_new

    @pl.when(c == pl.num_programs(2) - 1)
    def _():
        hout_ref[...] = h_new.reshape(hout_ref.shape)


def _ssd_sample_body(x_ref, b_ref, c_ref, z_ref, dt_ref, hx_ref, hb_ref, hc_ref, st_ref,
                     cwx_ref, cwb_ref, cwc_ref, cbx_ref, cbb_ref, cbc_ref,
                     dtb_ref, alog_ref, dsk_ref, nw_ref, lu_ref, rep_ref,
                     yn_ref, hout_ref, y_sc, yi_sc, cum_sc, *, seq):
    xg = _conv_sample(x_ref, hx_ref, cwx_ref, cbx_ref, seq)
    bg = _conv_sample(b_ref, hb_ref, cwb_ref, cbb_ref, seq)
    cg = _conv_sample(c_ref, hc_ref, cwc_ref, cbc_ref, seq)
    cum, dt_rep, cum_rep, rev_rep = _ssd_block(xg, bg, cg, dt_ref, dtb_ref, alog_ref, lu_ref, rep_ref, y_sc)
    cum_sc[...] = cum
    yi_sc[...] = jnp.zeros_like(yi_sc)
    c_bf = cg.astype(BF16)
    xw = (xg * (dt_rep * jnp.exp(rev_rep))).astype(BF16)
    shift = seq.bit_length() - 1
    seq_of_row_w = lax.shift_right_logical(lax.broadcasted_iota(jnp.int32, (ROWS, GROUP_W), 0), shift)
    seq_of_row_n = lax.shift_right_logical(lax.broadcasted_iota(jnp.int32, (ROWS, SSM_D_STATE), 0), shift)
    state_shape = (HEADS_PER_GROUP * SSM_HEAD_DIM, SSM_D_STATE)

    def per_sequence(s, carry):
        h0 = st_ref[s].reshape(state_shape)
        y_inter = _dot_nt(c_bf, h0.astype(BF16))
        yi_sc[...] += jnp.where(seq_of_row_w == s, y_inter, 0.0)
        b_own = jnp.where(seq_of_row_n == s, bg, 0.0).astype(BF16)
        last = cum_sc[pl.ds(s * seq + seq - 1, 1), :]
        h_new = _head_rows(jnp.exp(last)) * h0 + _dot_tn(xw, b_own)
        hout_ref[s] = h_new.reshape(hout_ref.shape[1:])
        return carry

    lax.fori_loop(0, ROWS // seq, per_sequence, 0)
    y = y_sc[...] + yi_sc[...] * jnp.exp(cum_rep) + dsk_ref[...] * xg
    yn_ref[...] = _gated_group_norm(y, z_ref[...], nw_ref[...]).astype(yn_ref.dtype)


def _ssd_param_specs(gidx):
    def col(block, off):
        return lambda *ids: (0, off // block + gidx(*ids))
    return [
        pl.BlockSpec((SSM_CONV_WIDTH, GROUP_W), col(GROUP_W, 0)),
        pl.BlockSpec((SSM_CONV_WIDTH, SSM_D_STATE), col(SSM_D_STATE, SSM_D_INNER)),
        pl.BlockSpec((SSM_CONV_WIDTH, SSM_D_STATE), col(SSM_D_STATE, SSM_D_INNER + SSM_BC)),
        pl.BlockSpec((1, GROUP_W), col(GROUP_W, 0)),
        pl.BlockSpec((1, SSM_D_STATE), col(SSM_D_STATE, SSM_D_INNER)),
        pl.BlockSpec((1, SSM_D_STATE), col(SSM_D_STATE, SSM_D_INNER + SSM_BC)),
        pl.BlockSpec((1, LANES), col(LANES, 0)),
        pl.BlockSpec((1, LANES), col(LANES, 0)),
        pl.BlockSpec((1, GROUP_W), col(GROUP_W, 0)),
        pl.BlockSpec((1, GROUP_W), col(GROUP_W, 0)),
        pl.BlockSpec((2 * ROWS, ROWS), lambda *ids: (0, 0)),
        pl.BlockSpec((LANES, GROUP_W), lambda *ids: (0, 0)),
    ]


def _ssd_token_specs(ridx, gidx):
    def col(block, off):
        return lambda *ids: (ridx(*ids), off // block + gidx(*ids))
    return [
        pl.BlockSpec((ROWS, GROUP_W), col(GROUP_W, COL_X)),
        pl.BlockSpec((ROWS, SSM_D_STATE), col(SSM_D_STATE, COL_B)),
        pl.BlockSpec((ROWS, SSM_D_STATE), col(SSM_D_STATE, COL_C)),
        pl.BlockSpec((ROWS, GROUP_W), col(GROUP_W, COL_Z)),
        pl.BlockSpec((ROWS, LANES), col(LANES, COL_DT)),
    ]


def _ssd_prompt(proj, ssd_params, batch, length):
    nblk = length // ROWS
    lu, rep = _ssd_consts(ROWS)
    ridx = lambda b, g, c: b * nblk + c
    gidx = lambda b, g, c: g
    conv_w, conv_b, dtb, alog, dsk, nw = ssd_params
    args = [proj] * 5 + [conv_w] * 3 + [conv_b] * 3 + [dtb, alog, dsk, nw, lu, rep]
    return pl.pallas_call(
        _ssd_prompt_body,
        grid=(batch, SSM_N_GROUPS, nblk),
        in_specs=_ssd_token_specs(ridx, gidx) + _ssd_param_specs(gidx),
        out_specs=[
            pl.BlockSpec((ROWS, GROUP_W), lambda b, g, c: (b * nblk + c, g)),
            pl.BlockSpec((1, HEADS_PER_GROUP, SSM_HEAD_DIM, SSM_D_STATE), lambda b, g, c: (b, g, 0, 0)),
        ],
        out_shape=[
            jax.ShapeDtypeStruct((proj.shape[0], SSM_D_INNER), BF16),
            jax.ShapeDtypeStruct((batch, SSM_N_HEADS, SSM_HEAD_DIM, SSM_D_STATE), F32),
        ],
        scratch_shapes=[
            pltpu.VMEM((HEADS_PER_GROUP * SSM_HEAD_DIM, SSM_D_STATE), F32),
            pltpu.VMEM((8, GROUP_W), F32),
            pltpu.VMEM((8, SSM_D_STATE), F32),
            pltpu.VMEM((8, SSM_D_STATE), F32),
            pltpu.VMEM((ROWS, GROUP_W), F32),
        ],
        compiler_params=_params("parallel", "parallel", "arbitrary"),
        name="ssd_prompt",
    )(*args)


def _ssd_sample(proj, yn, carry, state, ssd_params, row0, batch, seq):
    per_blk = ROWS // seq
    nblk = batch // per_blk
    blk0 = row0 // ROWS
    lu, rep = _ssd_consts(seq)
    ridx = lambda m, g: blk0 + m
    gidx = lambda m, g: g
    conv_w, conv_b, dtb, alog, dsk, nw = ssd_params

    def carry_spec(block, off):
        return pl.BlockSpec((ROWS, block), lambda m, g: (m, off // block + g))

    state_spec = pl.BlockSpec((per_blk, HEADS_PER_GROUP, SSM_HEAD_DIM, SSM_D_STATE), lambda m, g: (m, g, 0, 0))
    args = [proj] * 5 + [carry] * 3 + [state] + [conv_w] * 3 + [conv_b] * 3 + [dtb, alog, dsk, nw, lu, rep, yn]
    n_in = len(args)
    return pl.pallas_call(
        lambda *refs: _ssd_sample_body(*refs[:n_in - 1], *refs[n_in:], seq=seq),
        grid=(nblk, SSM_N_GROUPS),
        in_specs=_ssd_token_specs(ridx, gidx)
        + [carry_spec(GROUP_W, 0), carry_spec(SSM_D_STATE, SSM_D_INNER), carry_spec(SSM_D_STATE, SSM_D_INNER + SSM_BC),
           state_spec]
        + _ssd_param_specs(gidx) + [pl.BlockSpec(memory_space=pl.ANY)],
        out_specs=[pl.BlockSpec((ROWS, GROUP_W), lambda m, g: (blk0 + m, g)), state_spec],
        out_shape=[jax.ShapeDtypeStruct(yn.shape, yn.dtype), jax.ShapeDtypeStruct(state.shape, F32)],
        scratch_shapes=[
            pltpu.VMEM((ROWS, GROUP_W), F32),
            pltpu.VMEM((ROWS, GROUP_W), F32),
            pltpu.VMEM((ROWS, LANES), F32),
        ],
        input_output_aliases={n_in - 1: 0},
        compiler_params=_params("parallel", "parallel"),
        name="ssd_sample",
    )(*args)


def _lower_bound(raw, layer):
    e = jnp.exp(raw - jnp.max(raw, axis=0, keepdims=True))
    return jnp.sum(e[0:layer + 1], axis=0, keepdims=True) / jnp.sum(e, axis=0, keepdims=True)


def _head_cols(ref, hh):
    return ref[:, hh * HG_KEY_DIM:(hh + 1) * HG_KEY_DIM]


def _hgrn_blocks(q_ref, f_ref, i_ref, lbraw_ref, a_ref, m_ref, layer, n_levels):
    heads = range(HG_HEADS_PER_STEP)
    a01 = a_ref[...]
    q = [_head_cols(q_ref, hh) for hh in heads]
    v_bf = [_head_cols(i_ref, hh).astype(BF16) for hh in heads]
    k, parts = [], []
    for hh in heads:
        hf = _head_cols(f_ref, hh)
        lb = _lower_bound(_head_cols(lbraw_ref, hh), layer)
        parts.append(_split3(jnp.log(lb + (1.0 - lb) * _sigmoid(hf))))
        k.append((1.0 - lb) * _sigmoid(-hf))
    sums = [_dot(a01, p[0]) + _dot(a01, p[1]) + _dot(a01, p[2]) for p in parts]
    att = [m_ref[n_levels] * _dot_nt(q[hh].astype(BF16), k[hh].astype(BF16)) for hh in heads]
    for lvl in range(n_levels):
        for hh in heads:
            e = jnp.exp(sums[hh][lvl * ROWS:(lvl + 1) * ROWS])
            att[hh] = att[hh] + m_ref[lvl] * _dot_nt((q[hh] * e).astype(BF16), (k[hh] * e).astype(BF16))
    cum = [s[n_levels * ROWS:(n_levels + 1) * ROWS] for s in sums]
    o_intra = [_dot(att[hh].astype(BF16), v_bf[hh]) for hh in heads]
    q_dec = [(q[hh] * jnp.exp(cum[hh])).astype(BF16) for hh in heads]
    k_dec = [(k[hh] * jnp.exp(sums[hh][(n_levels + 1) * ROWS:(n_levels + 2) * ROWS])).astype(BF16) for hh in heads]
    return o_intra, q_dec, k_dec, v_bf, cum


def _hgrn_gate_norm(o, g, nw):
    return (_rms(o, nw) * _silu(g)).astype(BF16)


def _hgrn_prompt_body(q_ref, f_ref, i_ref, g_ref, lbraw_ref, nw_ref, a_ref, m_ref,
                      on_ref, sout_ref, s_sc, *, layer, n_levels):
    c = pl.program_id(2)

    @pl.when(c == 0)
    def _():
        s_sc[...] = jnp.zeros_like(s_sc)

    heads = range(HG_HEADS_PER_STEP)
    o_intra, q_dec, k_dec, v_bf, cum = _hgrn_blocks(q_ref, f_ref, i_ref, lbraw_ref, a_ref, m_ref, layer, n_levels)
    s = [s_sc[hh] for hh in heads]
    o = [o_intra[hh] + _dot(q_dec[hh], s[hh].astype(BF16)) for hh in heads]
    upd = [_dot_tn(k_dec[hh], v_bf[hh]) for hh in heads]
    for hh in heads:
        on_ref[:, hh * HG_VAL_DIM:(hh + 1) * HG_VAL_DIM] = _hgrn_gate_norm(o[hh], _head_cols(g_ref, hh), nw_ref[...])
        last_col = cum[hh].T[:, ROWS - 1:ROWS]
        s_sc[hh] = jnp.exp(last_col) * s[hh] + upd[hh]

    @pl.when(c == pl.num_programs(2) - 1)
    def _():
        sout_ref[...] = s_sc[...].reshape(sout_ref.shape)


def _hgrn_sample_body(q_ref, f_ref, i_ref, g_ref, st_ref, lbraw_ref, nw_ref, a_ref, m_ref,
                      on_ref, sout_ref, o_sc, cumt_sc, qd_sc, kd_sc, *, layer, n_levels, seq):
    o_intra, q_dec, k_dec, _, cum = _hgrn_blocks(q_ref, f_ref, i_ref, lbraw_ref, a_ref, m_ref, layer, n_levels)
    for hh in range(HG_HEADS_PER_STEP):
        o_sc[hh] = o_intra[hh]
        cumt_sc[hh] = cum[hh].T
        qd_sc[hh] = q_dec[hh]
        kd_sc[hh] = k_dec[hh]
    shift = seq.bit_length() - 1
    seq_of_row = lax.shift_right_logical(lax.broadcasted_iota(jnp.int32, (ROWS, HG_VAL_DIM), 0), shift)
    lane = lax.broadcasted_iota(jnp.int32, (HG_KEY_DIM, ROWS), 1)

    def per_sequence(s, carry):
        own = seq_of_row == s
        is_last = lane == s * seq + seq - 1
        heads = range(HG_HEADS_PER_STEP)
        s0 = [st_ref[s, hh] for hh in heads]
        o_inter = [_dot(qd_sc[hh], s0[hh].astype(BF16)) for hh in heads]
        upd = [_dot_tn(kd_sc[hh], jnp.where(own, _head_cols(i_ref, hh), 0.0).astype(BF16)) for hh in heads]
        for hh in heads:
            o_sc[hh] += jnp.where(own, o_inter[hh], 0.0)
            last_col = jnp.sum(jnp.where(is_last, cumt_sc[hh], 0.0), axis=1, keepdims=True)
            sout_ref[s, hh] = jnp.exp(last_col) * s0[hh] + upd[hh]
        return carry

    lax.fori_loop(0, ROWS // seq, per_sequence, 0)
    for hh in range(HG_HEADS_PER_STEP):
        on_ref[:, hh * HG_VAL_DIM:(hh + 1) * HG_VAL_DIM] = _hgrn_gate_norm(o_sc[hh], _head_cols(g_ref, hh), nw_ref[...])


def _hgrn_token_specs(ridx, hidx):
    width = HG_HEADS_PER_STEP * HG_KEY_DIM

    def col(off):
        return lambda *ids: (ridx(*ids), off // width + hidx(*ids))
    return [pl.BlockSpec((ROWS, width), col(off)) for off in (COL_Q, COL_F, COL_I, COL_G)]


def _hgrn_param_specs(hidx, n_rows, n_levels):
    return [
        pl.BlockSpec((n_rows, HG_HEADS_PER_STEP * HG_KEY_DIM), lambda *ids: (0, hidx(*ids))),
        pl.BlockSpec((1, HG_VAL_DIM), lambda *ids: (0, 0)),
        pl.BlockSpec(((n_levels + 2) * ROWS, ROWS), lambda *ids: (0, 0)),
        pl.BlockSpec((n_levels + 1, ROWS, ROWS), lambda *ids: (0, 0, 0)),
    ]


def _hgrn_prompt(proj, hg_lb, hg_norm, layer, batch, length):
    nblk = length // ROWS
    hb = HG_HEADS_PER_STEP
    a_all, m_all, n_levels = _hgrn_consts(ROWS)
    ridx = lambda b, h, c: b * nblk + c
    hidx = lambda b, h, c: h
    return pl.pallas_call(
        functools.partial(_hgrn_prompt_body, layer=layer, n_levels=n_levels),
        grid=(batch, HG_N_HEADS // hb, nblk),
        in_specs=_hgrn_token_specs(ridx, hidx) + _hgrn_param_specs(hidx, hg_lb.shape[0], n_levels),
        out_specs=[
            pl.BlockSpec((ROWS, hb * HG_VAL_DIM), lambda b, h, c: (b * nblk + c, h)),
            pl.BlockSpec((1, hb, HG_KEY_DIM, HG_VAL_DIM), lambda b, h, c: (b, h, 0, 0)),
        ],
        out_shape=[
            jax.ShapeDtypeStruct((proj.shape[0], D_MODEL), BF16),
            jax.ShapeDtypeStruct((batch, HG_N_HEADS, HG_KEY_DIM, HG_VAL_DIM), F32),
        ],
        scratch_shapes=[pltpu.VMEM((hb, HG_KEY_DIM, HG_VAL_DIM), F32)],
        compiler_params=_params("parallel", "parallel", "arbitrary"),
        name="hgrn_prompt",
    )(*([proj] * 4 + [hg_lb, hg_norm, a_all, m_all]))


def _hgrn_sample(proj, on, state, hg_lb, hg_norm, layer, row0, batch, seq):
    per_blk = ROWS // seq
    nblk = batch // per_blk
    blk0 = row0 // ROWS
    hb = HG_HEADS_PER_STEP
    a_all, m_all, n_levels = _hgrn_consts(seq)
    ridx = lambda m, h: blk0 + m
    hidx = lambda m, h: h
    state_spec = pl.BlockSpec((per_blk, hb, HG_KEY_DIM, HG_VAL_DIM), lambda m, h: (m, h, 0, 0))
    args = [proj] * 4 + [state, hg_lb, hg_norm, a_all, m_all, on]
    n_in = len(args)
    return pl.pallas_call(
        lambda *refs: functools.partial(_hgrn_sample_body, layer=layer, n_levels=n_levels, seq=seq)(
            *refs[:n_in - 1], *refs[n_in:]),
        grid=(nblk, HG_N_HEADS // hb),
        in_specs=_hgrn_token_specs(ridx, hidx) + [state_spec]
        + _hgrn_param_specs(hidx, hg_lb.shape[0], n_levels) + [pl.BlockSpec(memory_space=pl.ANY)],
        out_specs=[pl.BlockSpec((ROWS, hb * HG_VAL_DIM), lambda m, h: (blk0 + m, h)), state_spec],
        out_shape=[jax.ShapeDtypeStruct(on.shape, on.dtype), jax.ShapeDtypeStruct(state.shape, F32)],
        scratch_shapes=[
            pltpu.VMEM((hb, ROWS, HG_VAL_DIM), F32),
            pltpu.VMEM((hb, HG_KEY_DIM, ROWS), F32),
            pltpu.VMEM((hb, ROWS, HG_KEY_DIM), BF16),
            pltpu.VMEM((hb, ROWS, HG_KEY_DIM), BF16),
        ],
        input_output_aliases={n_in - 1: 0},
        compiler_params=_params("parallel", "parallel"),
        name="hgrn_sample",
    )(*args)


def _head_slabs(v):
    v = v.reshape(SSM_N_GROUPS, HEADS_PER_GROUP).astype(F32)
    return jnp.pad(v, ((0, 0), (0, LANES - HEADS_PER_GROUP))).reshape(1, SSM_N_GROUPS * LANES)


def _sample_carry(conv0, seq):
    bsz, wm1, cd = conv0.shape
    return jnp.pad(conv0, ((0, 0), (seq - wm1, 0), (0, 0))).reshape(bsz * seq, cd)


def kernel(x_prompt, x_sample, p_prompt, p_sample, state_ssm, state_conv, state_hgrn, norm_mix_pre, w_in, conv_w, conv_b, dt_bias, a_log, d_skip, ssm_norm, w_br_a, hg_lb, hg_norm, w_br_b, w_out, norm_mix_post, norm_ffn_pre, w_up, w_down, norm_ffn_post, norm_ple, w_ple_gate, w_ple_proj):
    depth = w_in.shape[0]
    bp, lp, _ = x_prompt.shape
    bs, ls, _ = x_sample.shape
    tp, ts = bp * lp, bs * ls
    wm1 = SSM_CONV_WIDTH - 1
    assert lp % ROWS == 0 and ROWS % ls == 0 and ls & (ls - 1) == 0 and ts % ROWS == 0
    assert lp >= wm1 and ls >= wm1

    xp = x_prompt.reshape(tp, D_MODEL)
    xs = x_sample.reshape(ts, D_MODEL)
    row = lambda a: a.reshape(1, -1).astype(F32)
    outs = {k: [] for k in ("ssm_p", "conv_p", "hg_p", "ssm_s", "conv_s", "hg_s")}
    for li in range(depth):
        h = _rmsnorm_cast(xp, xs, row(norm_mix_pre[li]))
        proj = _matmul(h, _repack_w_in(w_in[li]), out_dtype=F32, name="in_proj")

        tails = [proj[b * lp + lp - wm1:b * lp + lp, COL_X:COL_Q] for b in range(bp)]
        outs["conv_p"].append(jnp.stack(tails))
        outs["conv_s"].append(proj[tp:, COL_X:COL_Q].reshape(bs, ls, SSM_CONV_DIM)[:, ls - wm1:])

        ssd_params = (conv_w[li], row(conv_b[li]), _head_slabs(dt_bias[li]), _head_slabs(a_log[li]),
                      row(jnp.repeat(d_skip[li], SSM_HEAD_DIM)), row(ssm_norm[li]))
        yn, ssm_p = _ssd_prompt(proj, ssd_params, bp, lp)
        yn, ssm_s = _ssd_sample(proj, yn, _sample_carry(state_conv[li], ls), state_ssm[li], ssd_params, tp, bs, ls)
        on, hg_p = _hgrn_prompt(proj, hg_lb, row(hg_norm[li]), li, bp, lp)
        on, hg_s = _hgrn_sample(proj, on, state_hgrn[li], hg_lb, row(hg_norm[li]), li, tp, bs, ls)
        outs["ssm_p"].append(ssm_p)
        outs["ssm_s"].append(ssm_s)
        outs["hg_p"].append(hg_p)
        outs["hg_s"].append(hg_s)

        mixed = _merge(yn, w_br_a[li].astype(BF16), on, w_br_b[li].astype(BF16), proj)
        x1, h2 = _out_proj(mixed, w_out[li].astype(BF16), xp, xs, row(norm_mix_post[li]), row(norm_ffn_pre[li]))
        u = _matmul(h2, w_up[li].astype(BF16), out_dtype=BF16, act="relu2", name="ffn_up")
        x2, h3 = _ffn_down(u, w_down[li].astype(BF16), x1, row(norm_ffn_post[li]), row(norm_ple[li]))
        xp, xs = _ple(h3, w_ple_gate[li].astype(BF16), p_prompt[li].reshape(tp, -1), p_sample[li].reshape(ts, -1),
                      w_ple_proj[li].astype(BF16), x2)

    stack = lambda k: jnp.stack(outs[k])
    return (xp.reshape(bp, lp, D_MODEL), xs.reshape(bs, ls, D_MODEL),
            stack("ssm_p"), stack("conv_p"), stack("hg_p"), stack("ssm_s"), stack("conv_s"), stack("hg_s"))
```

```python
import functools
import math

import numpy as np
import jax
import jax.numpy as jnp
from jax import lax
from jax.experimental import pallas as pl
from jax.experimental.pallas import tpu as pltpu

F32 = jnp.float32
BF16 = jnp.bfloat16

D_MODEL = 2048
SSM_D_INNER = 2 * D_MODEL
SSM_HEAD_DIM = 64
SSM_N_HEADS = SSM_D_INNER // SSM_HEAD_DIM
SSM_N_GROUPS = 8
SSM_D_STATE = 128
SSM_CONV_WIDTH = 4
SSM_BC = SSM_N_GROUPS * SSM_D_STATE
SSM_CONV_DIM = SSM_D_INNER + 2 * SSM_BC
HG_KEY_DIM = 128
HG_N_HEADS = D_MODEL // HG_KEY_DIM
HG_VAL_DIM = D_MODEL // HG_N_HEADS
FFN_HIDDEN = 4 * D_MODEL
NORM_EPS = 1e-6

OFF_XBC = SSM_D_INNER
OFF_DT = OFF_XBC + SSM_CONV_DIM
OFF_HQ = OFF_DT + SSM_N_HEADS

LANES = 128
ROWS = 128
GROUP_W = SSM_D_INNER // SSM_N_GROUPS
HEADS_PER_GROUP = SSM_N_HEADS // SSM_N_GROUPS
HG_HEADS_PER_STEP = 4
SSD_GROUPS_PER_STEP = 4
LOG2E = 1.4426950408889634

COL_Z = 0
COL_X = SSM_D_INNER
COL_B = COL_X + SSM_D_INNER
COL_C = COL_B + SSM_BC
COL_Q = COL_C + SSM_BC
COL_F = COL_Q + D_MODEL
COL_I = COL_F + D_MODEL
COL_G = COL_I + D_MODEL
COL_GA = COL_G + D_MODEL
COL_GB = COL_GA + D_MODEL
COL_DT = COL_GB + D_MODEL
PROJ_W = COL_DT + SSM_N_GROUPS * LANES

VMEM_LIMIT = 52 * 1024 * 1024


def _params(*sem):
    return pltpu.CompilerParams(dimension_semantics=sem, vmem_limit_bytes=VMEM_LIMIT)


def _pick(n, cands):
    for c in cands:
        if n % c == 0:
            return c
    raise ValueError(f"no tile for {n} in {cands}")


def _dot(a, b):
    return jnp.dot(a, b, preferred_element_type=F32)


def _dot_nt(a, b):
    return lax.dot_general(a, b, (((1,), (1,)), ((), ())), preferred_element_type=F32)


def _dot_tn(a, b):
    return lax.dot_general(a, b, (((0,), (0,)), ((), ())), preferred_element_type=F32)


def _split3(x):
    hi = x.astype(BF16)
    r1 = x - hi.astype(F32)
    mid = r1.astype(BF16)
    lo = (r1 - mid.astype(F32)).astype(BF16)
    return hi, mid, lo


def _dot3_l(a01x3, x):
    return _dot(a01x3, jnp.concatenate(_split3(x), axis=0))


def _dot3_r(x, b01x3):
    return _dot(jnp.concatenate(_split3(x), axis=1), b01x3)


def _sigmoid(x):
    return 1.0 / (1.0 + jnp.exp(-x))


def _silu(x):
    return x * _sigmoid(x)


def _softplus(x):
    return jnp.maximum(x, 0.0) + jnp.log1p(jnp.exp(-jnp.abs(x)))


def _rms(x, w):
    ms = jnp.mean(x * x, axis=-1, keepdims=True)
    return x * lax.rsqrt(ms + NORM_EPS) * w


def _seq_masks(seq):
    t = np.arange(ROWS)
    same = (t[:, None] // seq) == (t[None, :] // seq)
    lower = same & (t[None, :] <= t[:, None])
    upper = same & (t[None, :] > t[:, None])
    return lower, upper


def _ssd_consts(seq):
    lower, upper = _seq_masks(seq)
    lu = np.concatenate([lower, upper], axis=0).astype(np.float32)
    rep = np.zeros((LANES, GROUP_W), np.float32)
    for j in range(HEADS_PER_GROUP):
        rep[j, j * SSM_HEAD_DIM:(j + 1) * SSM_HEAD_DIM] = 1.0
    return jnp.asarray(np.tile(lu, (1, 3)), BF16), jnp.asarray(np.tile(rep, (3, 1)), BF16)


def _level_widths(seq):
    widths, b = [], seq // 2
    while b >= 1:
        widths.append(b)
        b //= 2
    return widths


def _hgrn_consts(seq):
    lower, upper = _seq_masks(seq)
    t = np.arange(ROWS)
    masks = []
    for b in _level_widths(seq):
        is_q = (t // b) % 2 == 1
        pair = (t[:, None] // (2 * b)) == (t[None, :] // (2 * b))
        masks.append(pair & is_q[:, None] & (~is_q)[None, :])
    masks.append(np.eye(ROWS, dtype=bool))
    m_all = np.stack(masks).astype(np.float32)
    assert np.array_equal(m_all.sum(0) > 0, lower) and m_all.sum(0).max() == 1
    lu = np.concatenate([lower, upper], axis=0).astype(np.float32)
    return jnp.asarray(np.tile(lu, (1, 3)), BF16), jnp.asarray(m_all, F32)


def _group_specs(tm, n_prompt_blocks, width):
    last = n_prompt_blocks - 1
    return [
        pl.BlockSpec((tm, width), lambda i, *_: (jnp.minimum(i, last), 0)),
        pl.BlockSpec((tm, width), lambda i, *_: (jnp.maximum(i - n_prompt_blocks, 0), 0)),
    ]


def _group_rows(xp_ref, xs_ref, n_prompt_blocks):
    return jnp.where(pl.program_id(0) < n_prompt_blocks, xp_ref[...], xs_ref[...])


def _rmsnorm_body(xp_ref, xs_ref, w_ref, o_ref, *, n_prompt_blocks):
    o_ref[...] = _rms(_group_rows(xp_ref, xs_ref, n_prompt_blocks), w_ref[...]).astype(o_ref.dtype)


def _rmsnorm_cast(xp, xs, w):
    tp, d = xp.shape
    ts = xs.shape[0]
    tm = _pick(math.gcd(tp, ts), (512, 256, 128))
    return pl.pallas_call(
        functools.partial(_rmsnorm_body, n_prompt_blocks=tp // tm),
        grid=((tp + ts) // tm,),
        in_specs=_group_specs(tm, tp // tm, d) + [pl.BlockSpec((1, d), lambda i: (0, 0))],
        out_specs=pl.BlockSpec((tm, d), lambda i: (i, 0)),
        out_shape=jax.ShapeDtypeStruct((tp + ts, d), BF16),
        compiler_params=_params("parallel"),
        name="rmsnorm_cast",
    )(xp, xs, w)


IN_TN = 1024
IN_SHIFT = OFF_HQ - OFF_DT
N_SEG1 = OFF_DT // IN_TN
N_SEG2 = (COL_DT - OFF_DT) // IN_TN


def _in_proj_body(h_ref, wa_ref, wb_ref, wd_ref, o_ref, w_sc):
    j = pl.program_id(0)

    @pl.when(pl.program_id(1) == 0)
    def _():
        @pl.when(j < N_SEG1)
        def _():
            w_sc[...] = wa_ref[...].astype(BF16)

        @pl.when(jnp.logical_and(j >= N_SEG1, j < N_SEG1 + N_SEG2))
        def _():
            w_sc[0:IN_TN - IN_SHIFT] = wa_ref[IN_SHIFT:IN_TN].astype(BF16)
            w_sc[IN_TN - IN_SHIFT:IN_TN] = wb_ref[...].astype(BF16)

        @pl.when(j >= N_SEG1 + N_SEG2)
        def _():
            pad = jnp.zeros((LANES - HEADS_PER_GROUP, wd_ref.shape[1]), F32)
            for g in range(SSM_N_GROUPS):
                slab = jnp.concatenate([wd_ref[g * HEADS_PER_GROUP:(g + 1) * HEADS_PER_GROUP], pad], axis=0)
                w_sc[g * LANES:(g + 1) * LANES] = slab.astype(BF16)

    o_ref[...] = _dot_nt(h_ref[...], w_sc[...])


def _in_proj(h, wt):
    t, d = h.shape
    assert OFF_DT % IN_TN == 0 and (COL_DT - OFF_DT) % IN_TN == 0 and PROJ_W - COL_DT == IN_TN
    assert IN_TN % IN_SHIFT == 0 and IN_SHIFT % 8 == 0
    tm = _pick(t, (1088, 512, 256, 128))
    last_a = N_SEG1 + N_SEG2 - 1
    per = IN_TN // IN_SHIFT
    return pl.pallas_call(
        _in_proj_body,
        grid=(PROJ_W // IN_TN, t // tm),
        in_specs=[
            pl.BlockSpec((tm, d), lambda j, i: (i, 0)),
            pl.BlockSpec((IN_TN, d), lambda j, i: (jnp.minimum(j, last_a), 0)),
            pl.BlockSpec((IN_SHIFT, d), lambda j, i: (jnp.minimum(j, last_a) * per + per, 0)),
            pl.BlockSpec((IN_SHIFT, d), lambda j, i: (OFF_DT // IN_SHIFT, 0)),
        ],
        out_specs=pl.BlockSpec((tm, IN_TN), lambda j, i: (i, j)),
        out_shape=jax.ShapeDtypeStruct((t, PROJ_W), F32),
        scratch_shapes=[pltpu.VMEM((IN_TN, d), BF16)],
        compiler_params=_params("arbitrary", "arbitrary"),
        name="in_proj",
    )(h, wt, wt, wt)


def _matmul_body(a_ref, b_ref, o_ref, *, act):
    acc = _dot(a_ref[...], b_ref[...])
    if act == "relu2":
        acc = jnp.square(jnp.maximum(acc, 0.0))
    o_ref[...] = acc.astype(o_ref.dtype)


def _matmul(a, b, *, out_dtype, act=None, name):
    m, k = a.shape
    n = b.shape[1]
    tm = _pick(m, (1088, 512, 256, 128))
    tn = _pick(n, (1024, 512))
    return pl.pallas_call(
        functools.partial(_matmul_body, act=act),
        grid=(m // tm, n // tn),
        in_specs=[pl.BlockSpec((tm, k), lambda i, j: (i, 0)), pl.BlockSpec((k, tn), lambda i, j: (0, j))],
        out_specs=pl.BlockSpec((tm, tn), lambda i, j: (i, j)),
        out_shape=jax.ShapeDtypeStruct((m, n), out_dtype),
        compiler_params=_params("parallel", "parallel"),
        name=name,
    )(a, b)


def _merge_body(ya_ref, wa_ref, yb_ref, wb_ref, ga_ref, gb_ref, o_ref):
    a = _dot(ya_ref[...], wa_ref[...])
    b = _dot(yb_ref[...], wb_ref[...])
    o_ref[...] = (_sigmoid(ga_ref[...]) * a + _sigmoid(gb_ref[...]) * b).astype(o_ref.dtype)


def _merge(ya, wa, yb, wb, proj):
    t = ya.shape[0]
    tm = _pick(t, (544, 512, 256, 128))
    tn = 512
    return pl.pallas_call(
        _merge_body,
        grid=(t // tm, D_MODEL // tn),
        in_specs=[
            pl.BlockSpec((tm, SSM_D_INNER), lambda i, j: (i, 0)),
            pl.BlockSpec((SSM_D_INNER, tn), lambda i, j: (0, j)),
            pl.BlockSpec((tm, D_MODEL), lambda i, j: (i, 0)),
            pl.BlockSpec((D_MODEL, tn), lambda i, j: (0, j)),
            pl.BlockSpec((tm, tn), lambda i, j: (i, COL_GA // tn + j)),
            pl.BlockSpec((tm, tn), lambda i, j: (i, COL_GB // tn + j)),
        ],
        out_specs=pl.BlockSpec((tm, tn), lambda i, j: (i, j)),
        out_shape=jax.ShapeDtypeStruct((t, D_MODEL), BF16),
        compiler_params=_params("parallel", "parallel"),
        name="merge_branches",
    )(ya, wa, yb, wb, proj, proj)


def _out_body(a_ref, w_ref, xp_ref, xs_ref, npost_ref, nnext_ref, x1_ref, h_ref, *, n_prompt_blocks):
    acc = _dot(a_ref[...], w_ref[...])
    x1 = _group_rows(xp_ref, xs_ref, n_prompt_blocks) + _rms(acc, npost_ref[...])
    x1_ref[...] = x1
    h_ref[...] = _rms(x1, nnext_ref[...]).astype(h_ref.dtype)


def _out_proj(a, w, xp, xs, npost, nnext):
    t = a.shape[0]
    tp, ts = xp.shape[0], xs.shape[0]
    tm = _pick(math.gcd(tp, ts), (256, 128))
    row = lambda i: (i, 0)
    fixed = lambda i: (0, 0)
    return pl.pallas_call(
        functools.partial(_out_body, n_prompt_blocks=tp // tm),
        grid=(t // tm,),
        in_specs=[pl.BlockSpec((tm, D_MODEL), row), pl.BlockSpec((D_MODEL, D_MODEL), fixed)]
        + _group_specs(tm, tp // tm, D_MODEL)
        + [pl.BlockSpec((1, D_MODEL), fixed), pl.BlockSpec((1, D_MODEL), fixed)],
        out_specs=[pl.BlockSpec((tm, D_MODEL), row), pl.BlockSpec((tm, D_MODEL), row)],
        out_shape=[jax.ShapeDtypeStruct((t, D_MODEL), F32), jax.ShapeDtypeStruct((t, D_MODEL), BF16)],
        compiler_params=_params("parallel"),
        name="out_proj_norm",
    )(a, w, xp, xs, npost, nnext)


def _down_body(u_ref, w_ref, x_ref, npost_ref, nnext_ref, x2_ref, h_ref, acc_ref):
    k = pl.program_id(1)

    @pl.when(k == 0)
    def _():
        acc_ref[...] = jnp.zeros_like(acc_ref)

    acc_ref[...] += _dot(u_ref[...], w_ref[...])

    @pl.when(k == pl.num_programs(1) - 1)
    def _():
        x2 = x_ref[...] + _rms(acc_ref[...], npost_ref[...])
        x2_ref[...] = x2
        h_ref[...] = _rms(x2, nnext_ref[...]).astype(h_ref.dtype)


def _ffn_down(u, w, x, npost, nnext):
    t, f = u.shape
    tm = _pick(t, (544, 512, 256, 128))
    tk = 2048
    row = lambda i, k: (i, 0)
    fixed = lambda i, k: (0, 0)
    return pl.pallas_call(
        _down_body,
        grid=(t // tm, f // tk),
        in_specs=[
            pl.BlockSpec((tm, tk), lambda i, k: (i, k)),
            pl.BlockSpec((tk, D_MODEL), lambda i, k: (k, 0)),
            pl.BlockSpec((tm, D_MODEL), row),
            pl.BlockSpec((1, D_MODEL), fixed),
            pl.BlockSpec((1, D_MODEL), fixed),
        ],
        out_specs=[pl.BlockSpec((tm, D_MODEL), row), pl.BlockSpec((tm, D_MODEL), row)],
        out_shape=[jax.ShapeDtypeStruct((t, D_MODEL), F32), jax.ShapeDtypeStruct((t, D_MODEL), BF16)],
        scratch_shapes=[pltpu.VMEM((tm, D_MODEL), F32)],
        compiler_params=_params("parallel", "arbitrary"),
        name="ffn_down_norm",
    )(u, w, x, npost, nnext)


def _ple_body(h_ref, wg_ref, pp_ref, ps_ref, wp_ref, x_ref, yp_ref, ys_ref, *, n_prompt_blocks):
    i = pl.program_id(0)
    g = _sigmoid(_dot(h_ref[...], wg_ref[...]))
    e = _dot(_group_rows(pp_ref, ps_ref, n_prompt_blocks).astype(BF16), wp_ref[...])
    y = x_ref[...] + g * e

    @pl.when(i < n_prompt_blocks)
    def _():
        yp_ref[...] = y

    @pl.when(i >= n_prompt_blocks)
    def _():
        ys_ref[...] = y


def _ple(h, wg, pp, ps, wp, x):
    tp, pd = pp.shape
    ts = ps.shape[0]
    tm = _pick(math.gcd(tp, ts), (256, 128))
    npb = tp // tm
    row = lambda i: (i, 0)
    fixed = lambda i: (0, 0)
    return pl.pallas_call(
        functools.partial(_ple_body, n_prompt_blocks=npb),
        grid=((tp + ts) // tm,),
        in_specs=[pl.BlockSpec((tm, D_MODEL), row), pl.BlockSpec((D_MODEL, D_MODEL), fixed)]
        + _group_specs(tm, npb, pd)
        + [pl.BlockSpec((pd, D_MODEL), fixed), pl.BlockSpec((tm, D_MODEL), row)],
        out_specs=_group_specs(tm, npb, D_MODEL),
        out_shape=[jax.ShapeDtypeStruct((tp, D_MODEL), F32), jax.ShapeDtypeStruct((ts, D_MODEL), F32)],
        compiler_params=_params("arbitrary"),
        name="ple_gate",
    )(h, wg, pp, ps, wp, x)


def _conv_taps(cur, shifted, cw, cb):
    acc = cw[SSM_CONV_WIDTH - 1:SSM_CONV_WIDTH, :] * cur
    for d in range(1, SSM_CONV_WIDTH):
        acc = acc + cw[SSM_CONV_WIDTH - 1 - d:SSM_CONV_WIDTH - d, :] * shifted[d]
    return _silu(acc + cb)


def _conv_prompt(cur_ref, halo_sc, cw_ref, cb_ref):
    cur = cur_ref[...]
    top = cur[0:8]
    halo = halo_sc[...]
    row8 = lax.broadcasted_iota(jnp.int32, top.shape, 0)
    shifted, shifted_top = {}, {}
    for d in range(1, SSM_CONV_WIDTH):
        shifted[d] = pltpu.roll(cur, d, 0)
        shifted_top[d] = jnp.where(row8 < d, pltpu.roll(halo, d, 0), pltpu.roll(top, d, 0))
    cw = cw_ref[...]
    cb = cb_ref[...]
    out = _conv_taps(cur, shifted, cw, cb)
    out_top = _conv_taps(top, shifted_top, cw, cb)
    halo_sc[...] = cur[ROWS - 8:ROWS]
    return jnp.concatenate([out_top, out[8:]], axis=0)


def _conv_sample(cur_ref, carry_ref, cw_ref, cb_ref, seq):
    cur = cur_ref[...]
    carry = carry_ref[...]
    pos = lax.broadcasted_iota(jnp.int32, cur.shape, 0) & (seq - 1)
    shifted = {}
    for d in range(1, SSM_CONV_WIDTH):
        shifted[d] = jnp.where(pos >= d, pltpu.roll(cur, d, 0), pltpu.roll(carry, ROWS - (seq - d), 0))
    return _conv_taps(cur, shifted, cw_ref[...], cb_ref[...])


def _ssd_blocks(xg, bg_bf, cg_bf, dt_raw, dtb, alog, lu_ref, rep_ref, y_sc):
    groups = range(len(xg))
    lu = lu_ref[...]
    mask = lu[0:ROWS, 0:ROWS].astype(F32) > 0.5
    dt = [_softplus(dt_raw[g] + dtb[g]) for g in groups]
    cr = [_dot3_l(lu, dt[g] * (-jnp.exp(alog[g]))) for g in groups]
    cum = [c[0:ROWS] for c in cr]
    col = [c * LOG2E for c in cum]
    row_t = [(col[g] - jnp.log2(dt[g])).T for g in groups]
    x_bf = [x.astype(BF16) for x in xg]
    cb = [_dot_nt(cg_bf[g], bg_bf[g]) for g in groups]
    for j in range(HEADS_PER_GROUP):
        hs = slice(j * SSM_HEAD_DIM, (j + 1) * SSM_HEAD_DIM)
        for g in groups:
            seg = col[g][:, j:j + 1] - row_t[g][j:j + 1, :]
            m = (cb[g] * jnp.where(mask, jnp.exp2(seg), 0.0)).astype(BF16)
            y_sc[:, g * GROUP_W + j * SSM_HEAD_DIM:g * GROUP_W + (j + 1) * SSM_HEAD_DIM] = _dot(m, x_bf[g][:, hs])
    rep = rep_ref[...]
    slabs = [_dot3_r(jnp.concatenate([jnp.exp2(col[g]), dt[g] * jnp.exp(cr[g][ROWS:2 * ROWS])], axis=0), rep)
             for g in groups]
    return cum, [sl[0:ROWS] for sl in slabs], [sl[ROWS:2 * ROWS] for sl in slabs]


def _head_rows(e_row):
    return jnp.concatenate(
        [jnp.broadcast_to(e_row[0:1, j:j + 1], (SSM_HEAD_DIM, SSM_D_STATE)) for j in range(HEADS_PER_GROUP)], axis=0)


def _gated_group_norm(y, z, nw):
    return _rms(y * _silu(z), nw)


def _ssd_prompt_body(x_ref, b_ref, c_ref, z_ref, dt_ref, cwx_ref, cwb_ref, cwc_ref, cbx_ref, cbb_ref, cbc_ref,
                     dtb_ref, alog_ref, dsk_ref, nw_ref, lu_ref, rep_ref,
                     yn_ref, hout_ref, h_sc, px_sc, pb_sc, pc_sc, y_sc):
    c = pl.program_id(2)
    groups = range(SSD_GROUPS_PER_STEP)

    @pl.when(c == 0)
    def _():
        h_sc[...] = jnp.zeros_like(h_sc)
        px_sc[...] = jnp.zeros_like(px_sc)
        pb_sc[...] = jnp.zeros_like(pb_sc)
        pc_sc[...] = jnp.zeros_like(pc_sc)

    x_all = _conv_prompt(x_ref, px_sc, cwx_ref, cbx_ref)
    b_all = _conv_prompt(b_ref, pb_sc, cwb_ref, cbb_ref).astype(BF16)
    c_all = _conv_prompt(c_ref, pc_sc, cwc_ref, cbc_ref).astype(BF16)
    wide = [slice(g * GROUP_W, (g + 1) * GROUP_W) for g in groups]
    narrow = [slice(g * SSM_D_STATE, (g + 1) * SSM_D_STATE) for g in groups]
    xg = [x_all[:, wide[g]] for g in groups]
    bg = [b_all[:, narrow[g]] for g in groups]
    cg = [c_all[:, narrow[g]] for g in groups]
    cum, e_rep, w_rep = _ssd_blocks(
        xg, bg, cg, [dt_ref[:, narrow[g]] for g in groups], [dtb_ref[:, narrow[g]] for g in groups],
        [alog_ref[:, narrow[g]] for g in groups], lu_ref, rep_ref, y_sc)
    h = [h_sc[wide[g]] for g in groups]
    y_inter = [_dot_nt(cg[g], h[g].astype(BF16)) for g in groups]
    upd = [_dot_tn((xg[g] * w_rep[g]).astype(BF16), bg[g]) for g in groups]
    for g in groups:
        y = y_sc[:, wide[g]] + y_inter[g] * e_rep[g] + dsk_ref[:, wide[g]] * xg[g]
        yn_ref[:, wide[g]] = _gated_group_norm(y, z_ref[:, wide[g]], nw_ref[:, wide[g]]).astype(yn_ref.dtype)
        h_sc[wide[g]] = _head_rows(jnp.exp(cum[g][ROWS - 1:ROWS, :])) * h[g] + upd[g]

    @pl.when(c == pl.num_programs(2) - 1)
    def _():
        hout_ref[...] = h_sc[...].reshape(hout_ref.shape)


def _ssd_sample_body(x_ref, b_ref, c_ref, z_ref, dt_ref, hx_ref, hb_ref, hc_ref, st_ref,
                     cwx_ref, cwb_ref, cwc_ref, cbx_ref, cbb_ref, cbc_ref,
                     dtb_ref, alog_ref, dsk_ref, nw_ref, lu_ref, rep_ref,
                     yn_ref, hout_ref, y_sc, yi_sc, cum_sc, *, seq):
    xg = _conv_sample(x_ref, hx_ref, cwx_ref, cbx_ref, seq)
    bg = _conv_sample(b_ref, hb_ref, cwb_ref, cbb_ref, seq)
    cg = _conv_sample(c_ref, hc_ref, cwc_ref, cbc_ref, seq)
    c_bf = cg.astype(BF16)
    cums, e_reps, w_reps = _ssd_blocks([xg], [bg.astype(BF16)], [c_bf], [dt_ref[...]], [dtb_ref[...]], [alog_ref[...]],
                                       lu_ref, rep_ref, y_sc)
    cum_sc[...] = cums[0]
    yi_sc[...] = jnp.zeros_like(yi_sc)
    xw = (xg * w_reps[0]).astype(BF16)
    shift = seq.bit_length() - 1
    seq_of_row_w = lax.shift_right_logical(lax.broadcasted_iota(jnp.int32, (ROWS, GROUP_W), 0), shift)
    seq_of_row_n = lax.shift_right_logical(lax.broadcasted_iota(jnp.int32, (ROWS, SSM_D_STATE), 0), shift)
    state_shape = (HEADS_PER_GROUP * SSM_HEAD_DIM, SSM_D_STATE)

    def per_sequence(s, carry):
        h0 = st_ref[s].reshape(state_shape)
        y_inter = _dot_nt(c_bf, h0.astype(BF16))
        yi_sc[...] += jnp.where(seq_of_row_w == s, y_inter, 0.0)
        b_own = jnp.where(seq_of_row_n == s, bg, 0.0).astype(BF16)
        last = cum_sc[pl.ds(s * seq + seq - 1, 1), :]
        h_new = _head_rows(jnp.exp(last)) * h0 + _dot_tn(xw, b_own)
        hout_ref[s] = h_new.reshape(hout_ref.shape[1:])
        return carry

    lax.fori_loop(0, ROWS // seq, per_sequence, 0)
    y = y_sc[...] + yi_sc[...] * e_reps[0] + dsk_ref[...] * xg
    yn_ref[...] = _gated_group_norm(y, z_ref[...], nw_ref[...]).astype(yn_ref.dtype)


def _ssd_param_specs(gidx, ng):
    wide, narrow = ng * GROUP_W, ng * SSM_D_STATE

    def col(block, off):
        return lambda *ids: (0, off // block + gidx(*ids))
    return [
        pl.BlockSpec((SSM_CONV_WIDTH, wide), col(wide, 0)),
        pl.BlockSpec((SSM_CONV_WIDTH, narrow), col(narrow, SSM_D_INNER)),
        pl.BlockSpec((SSM_CONV_WIDTH, narrow), col(narrow, SSM_D_INNER + SSM_BC)),
        pl.BlockSpec((1, wide), col(wide, 0)),
        pl.BlockSpec((1, narrow), col(narrow, SSM_D_INNER)),
        pl.BlockSpec((1, narrow), col(narrow, SSM_D_INNER + SSM_BC)),
        pl.BlockSpec((1, narrow), col(narrow, 0)),
        pl.BlockSpec((1, narrow), col(narrow, 0)),
        pl.BlockSpec((1, wide), col(wide, 0)),
        pl.BlockSpec((1, wide), col(wide, 0)),
        pl.BlockSpec((2 * ROWS, 3 * ROWS), lambda *ids: (0, 0)),
        pl.BlockSpec((3 * LANES, GROUP_W), lambda *ids: (0, 0)),
    ]


def _ssd_token_specs(ridx, gidx, ng):
    wide, narrow = ng * GROUP_W, ng * SSM_D_STATE

    def col(block, off):
        return lambda *ids: (ridx(*ids), off // block + gidx(*ids))
    return [
        pl.BlockSpec((ROWS, wide), col(wide, COL_X)),
        pl.BlockSpec((ROWS, narrow), col(narrow, COL_B)),
        pl.BlockSpec((ROWS, narrow), col(narrow, COL_C)),
        pl.BlockSpec((ROWS, wide), col(wide, COL_Z)),
        pl.BlockSpec((ROWS, narrow), col(narrow, COL_DT)),
    ]


def _ssd_prompt(proj, ssd_params, batch, length):
    nblk = length // ROWS
    ng = SSD_GROUPS_PER_STEP
    lu, rep = _ssd_consts(ROWS)
    ridx = lambda b, g, c: b * nblk + c
    gidx = lambda b, g, c: g
    conv_w, conv_b, dtb, alog, dsk, nw = ssd_params
    args = [proj] * 5 + [conv_w] * 3 + [conv_b] * 3 + [dtb, alog, dsk, nw, lu, rep]
    return pl.pallas_call(
        _ssd_prompt_body,
        grid=(batch, SSM_N_GROUPS // ng, nblk),
        in_specs=_ssd_token_specs(ridx, gidx, ng) + _ssd_param_specs(gidx, ng),
        out_specs=[
            pl.BlockSpec((ROWS, ng * GROUP_W), lambda b, g, c: (b * nblk + c, g)),
            pl.BlockSpec((1, ng * HEADS_PER_GROUP, SSM_HEAD_DIM, SSM_D_STATE), lambda b, g, c: (b, g, 0, 0)),
        ],
        out_shape=[
            jax.ShapeDtypeStruct((proj.shape[0], SSM_D_INNER), BF16),
            jax.ShapeDtypeStruct((batch, SSM_N_HEADS, SSM_HEAD_DIM, SSM_D_STATE), F32),
        ],
        scratch_shapes=[
            pltpu.VMEM((ng * HEADS_PER_GROUP * SSM_HEAD_DIM, SSM_D_STATE), F32),
            pltpu.VMEM((8, ng * GROUP_W), F32),
            pltpu.VMEM((8, ng * SSM_D_STATE), F32),
            pltpu.VMEM((8, ng * SSM_D_STATE), F32),
            pltpu.VMEM((ROWS, ng * GROUP_W), F32),
        ],
        compiler_params=_params("parallel", "parallel", "arbitrary"),
        name="ssd_prompt",
    )(*args)


def _ssd_sample(proj, yn, carry, state, ssd_params, row0, batch, seq):
    per_blk = ROWS // seq
    nblk = batch // per_blk
    blk0 = row0 // ROWS
    lu, rep = _ssd_consts(seq)
    ridx = lambda m, g: blk0 + m
    gidx = lambda m, g: g
    conv_w, conv_b, dtb, alog, dsk, nw = ssd_params

    def carry_spec(block, off):
        return pl.BlockSpec((ROWS, block), lambda m, g: (m, off // block + g))

    state_spec = pl.BlockSpec((per_blk, HEADS_PER_GROUP, SSM_HEAD_DIM, SSM_D_STATE), lambda m, g: (m, g, 0, 0))
    args = [proj] * 5 + [carry] * 3 + [state] + [conv_w] * 3 + [conv_b] * 3 + [dtb, alog, dsk, nw, lu, rep, yn]
    n_in = len(args)
    return pl.pallas_call(
        lambda *refs: _ssd_sample_body(*refs[:n_in - 1], *refs[n_in:], seq=seq),
        grid=(nblk, SSM_N_GROUPS),
        in_specs=_ssd_token_specs(ridx, gidx, 1)
        + [carry_spec(GROUP_W, 0), carry_spec(SSM_D_STATE, SSM_D_INNER), carry_spec(SSM_D_STATE, SSM_D_INNER + SSM_BC),
           state_spec]
        + _ssd_param_specs(gidx, 1) + [pl.BlockSpec(memory_space=pl.ANY)],
        out_specs=[pl.BlockSpec((ROWS, GROUP_W), lambda m, g: (blk0 + m, g)), state_spec],
        out_shape=[jax.ShapeDtypeStruct(yn.shape, yn.dtype), jax.ShapeDtypeStruct(state.shape, F32)],
        scratch_shapes=[
            pltpu.VMEM((ROWS, GROUP_W), F32),
            pltpu.VMEM((ROWS, GROUP_W), F32),
            pltpu.VMEM((ROWS, LANES), F32),
        ],
        input_output_aliases={n_in - 1: 0},
        compiler_params=_params("parallel", "parallel"),
        name="ssd_sample",
    )(*args)


def _lower_bound(raw, layer):
    e = jnp.exp(raw - jnp.max(raw, axis=0, keepdims=True))
    return jnp.sum(e[0:layer + 1], axis=0, keepdims=True) / jnp.sum(e, axis=0, keepdims=True)


def _head_cols(ref, hh):
    return ref[:, hh * HG_KEY_DIM:(hh + 1) * HG_KEY_DIM]


def _level_ref(cum, b):
    width = cum.shape[1]
    if b >= 8:
        blocks = [jnp.broadcast_to(cum[p * 2 * b + b - 1:p * 2 * b + b, :], (2 * b, width))
                  for p in range(ROWS // (2 * b))]
        return blocks[0] if len(blocks) == 1 else jnp.concatenate(blocks, axis=0)
    if b == 1:
        odd = (lax.broadcasted_iota(jnp.int32, cum.shape, 0) & 1) == 1
        return jnp.where(odd, pltpu.roll(cum, 1, 0), cum)
    tiles = cum.reshape(ROWS // 8, 8, width)
    if b == 4:
        ref = jnp.broadcast_to(tiles[:, 3:4, :], tiles.shape)
    else:
        sub = lax.broadcasted_iota(jnp.int32, tiles.shape, 1)
        ref = jnp.where(sub < 4, jnp.broadcast_to(tiles[:, 1:2, :], tiles.shape),
                        jnp.broadcast_to(tiles[:, 5:6, :], tiles.shape))
    return ref.reshape(cum.shape)


def _hgrn_blocks(q_ref, f_ref, i_ref, lbraw_ref, lu_ref, m_ref, layer, seq):
    heads = range(HG_HEADS_PER_STEP)
    widths = _level_widths(seq)
    lu = lu_ref[...]
    q = [_head_cols(q_ref, hh) for hh in heads]
    v_bf = [_head_cols(i_ref, hh).astype(BF16) for hh in heads]
    k, log_f = [], []
    for hh in heads:
        hf = _head_cols(f_ref, hh)
        lb = _lower_bound(_head_cols(lbraw_ref, hh), layer)
        log_f.append(jnp.log(lb + (1.0 - lb) * _sigmoid(hf)))
        k.append((1.0 - lb) * _sigmoid(-hf))
    sums = [_dot3_l(lu, p) for p in log_f]
    cum = [s[0:ROWS] for s in sums]
    att = [m_ref[len(widths)] * _dot_nt(q[hh].astype(BF16), k[hh].astype(BF16)) for hh in heads]
    for lvl, b in enumerate(widths):
        for hh in heads:
            e = jnp.exp(-jnp.abs(cum[hh] - _level_ref(cum[hh], b)))
            att[hh] = att[hh] + m_ref[lvl] * _dot_nt((q[hh] * e).astype(BF16), (k[hh] * e).astype(BF16))
    o_intra = [_dot(att[hh].astype(BF16), v_bf[hh]) for hh in heads]
    q_dec = [(q[hh] * jnp.exp(cum[hh])).astype(BF16) for hh in heads]
    k_dec = [(k[hh] * jnp.exp(sums[hh][ROWS:2 * ROWS])).astype(BF16) for hh in heads]
    return o_intra, q_dec, k_dec, v_bf, cum


def _hgrn_gate_norm(o, g, nw):
    return (_rms(o, nw) * _silu(g)).astype(BF16)


def _hgrn_prompt_body(q_ref, f_ref, i_ref, g_ref, lbraw_ref, nw_ref, lu_ref, m_ref,
                      on_ref, sout_ref, s_sc, *, layer):
    c = pl.program_id(2)

    @pl.when(c == 0)
    def _():
        s_sc[...] = jnp.zeros_like(s_sc)

    heads = range(HG_HEADS_PER_STEP)
    o_intra, q_dec, k_dec, v_bf, cum = _hgrn_blocks(q_ref, f_ref, i_ref, lbraw_ref, lu_ref, m_ref, layer, ROWS)
    s = [s_sc[hh] for hh in heads]
    o = [o_intra[hh] + _dot(q_dec[hh], s[hh].astype(BF16)) for hh in heads]
    upd = [_dot_tn(k_dec[hh], v_bf[hh]) for hh in heads]
    for hh in heads:
        on_ref[:, hh * HG_VAL_DIM:(hh + 1) * HG_VAL_DIM] = _hgrn_gate_norm(o[hh], _head_cols(g_ref, hh), nw_ref[...])
        last_col = cum[hh].T[:, ROWS - 1:ROWS]
        s_sc[hh] = jnp.exp(last_col) * s[hh] + upd[hh]

    @pl.when(c == pl.num_programs(2) - 1)
    def _():
        sout_ref[...] = s_sc[...].reshape(sout_ref.shape)


def _hgrn_sample_body(q_ref, f_ref, i_ref, g_ref, st_ref, lbraw_ref, nw_ref, lu_ref, m_ref,
                      on_ref, sout_ref, o_sc, cumt_sc, qd_sc, kd_sc, *, layer, seq):
    o_intra, q_dec, k_dec, _, cum = _hgrn_blocks(q_ref, f_ref, i_ref, lbraw_ref, lu_ref, m_ref, layer, seq)
    for hh in range(HG_HEADS_PER_STEP):
        o_sc[hh] = o_intra[hh]
        cumt_sc[hh] = cum[hh].T
        qd_sc[hh] = q_dec[hh]
        kd_sc[hh] = k_dec[hh]
    shift = seq.bit_length() - 1
    seq_of_row = lax.shift_right_logical(lax.broadcasted_iota(jnp.int32, (ROWS, HG_VAL_DIM), 0), shift)
    lane = lax.broadcasted_iota(jnp.int32, (HG_KEY_DIM, ROWS), 1)

    def per_sequence(s, carry):
        own = seq_of_row == s
        is_last = lane == s * seq + seq - 1
        heads = range(HG_HEADS_PER_STEP)
        s0 = [st_ref[s, hh] for hh in heads]
        o_inter = [_dot(qd_sc[hh], s0[hh].astype(BF16)) for hh in heads]
        upd = [_dot_tn(kd_sc[hh], jnp.where(own, _head_cols(i_ref, hh), 0.0).astype(BF16)) for hh in heads]
        for hh in heads:
            o_sc[hh] += jnp.where(own, o_inter[hh], 0.0)
            last_col = jnp.sum(jnp.where(is_last, cumt_sc[hh], 0.0), axis=1, keepdims=True)
            sout_ref[s, hh] = jnp.exp(last_col) * s0[hh] + upd[hh]
        return carry

    lax.fori_loop(0, ROWS // seq, per_sequence, 0)
    for hh in range(HG_HEADS_PER_STEP):
        on_ref[:, hh * HG_VAL_DIM:(hh + 1) * HG_VAL_DIM] = _hgrn_gate_norm(o_sc[hh], _head_cols(g_ref, hh), nw_ref[...])


def _hgrn_token_specs(ridx, hidx):
    width = HG_HEADS_PER_STEP * HG_KEY_DIM

    def col(off):
        return lambda *ids: (ridx(*ids), off // width + hidx(*ids))
    return [pl.BlockSpec((ROWS, width), col(off)) for off in (COL_Q, COL_F, COL_I, COL_G)]


def _hgrn_param_specs(hidx, n_rows, n_masks):
    return [
        pl.BlockSpec((n_rows, HG_HEADS_PER_STEP * HG_KEY_DIM), lambda *ids: (0, hidx(*ids))),
        pl.BlockSpec((1, HG_VAL_DIM), lambda *ids: (0, 0)),
        pl.BlockSpec((2 * ROWS, 3 * ROWS), lambda *ids: (0, 0)),
        pl.BlockSpec((n_masks, ROWS, ROWS), lambda *ids: (0, 0, 0)),
    ]


def _hgrn_prompt(proj, hg_lb, hg_norm, layer, batch, length):
    nblk = length // ROWS
    hb = HG_HEADS_PER_STEP
    lu, m_all = _hgrn_consts(ROWS)
    ridx = lambda b, h, c: b * nblk + c
    hidx = lambda b, h, c: h
    return pl.pallas_call(
        functools.partial(_hgrn_prompt_body, layer=layer),
        grid=(batch, HG_N_HEADS // hb, nblk),
        in_specs=_hgrn_token_specs(ridx, hidx) + _hgrn_param_specs(hidx, hg_lb.shape[0], m_all.shape[0]),
        out_specs=[
            pl.BlockSpec((ROWS, hb * HG_VAL_DIM), lambda b, h, c: (b * nblk + c, h)),
            pl.BlockSpec((1, hb, HG_KEY_DIM, HG_VAL_DIM), lambda b, h, c: (b, h, 0, 0)),
        ],
        out_shape=[
            jax.ShapeDtypeStruct((proj.shape[0], D_MODEL), BF16),
            jax.ShapeDtypeStruct((batch, HG_N_HEADS, HG_KEY_DIM, HG_VAL_DIM), F32),
        ],
        scratch_shapes=[pltpu.VMEM((hb, HG_KEY_DIM, HG_VAL_DIM), F32)],
        compiler_params=_params("parallel", "parallel", "arbitrary"),
        name="hgrn_prompt",
    )(*([proj] * 4 + [hg_lb, hg_norm, lu, m_all]))


def _hgrn_sample(proj, on, state, hg_lb, hg_norm, layer, row0, batch, seq):
    per_blk = ROWS // seq
    nblk = batch // per_blk
    blk0 = row0 // ROWS
    hb = HG_HEADS_PER_STEP
    lu, m_all = _hgrn_consts(seq)
    ridx = lambda m, h: blk0 + m
    hidx = lambda m, h: h
    state_spec = pl.BlockSpec((per_blk, hb, HG_KEY_DIM, HG_VAL_DIM), lambda m, h: (m, h, 0, 0))
    args = [proj] * 4 + [state, hg_lb, hg_norm, lu, m_all, on]
    n_in = len(args)
    return pl.pallas_call(
        lambda *refs: functools.partial(_hgrn_sample_body, layer=layer, seq=seq)(
            *refs[:n_in - 1], *refs[n_in:]),
        grid=(nblk, HG_N_HEADS // hb),
        in_specs=_hgrn_token_specs(ridx, hidx) + [state_spec]
        + _hgrn_param_specs(hidx, hg_lb.shape[0], m_all.shape[0]) + [pl.BlockSpec(memory_space=pl.ANY)],
        out_specs=[pl.BlockSpec((ROWS, hb * HG_VAL_DIM), lambda m, h: (blk0 + m, h)), state_spec],
        out_shape=[jax.ShapeDtypeStruct(on.shape, on.dtype), jax.ShapeDtypeStruct(state.shape, F32)],
        scratch_shapes=[
            pltpu.VMEM((hb, ROWS, HG_VAL_DIM), F32),
            pltpu.VMEM((hb, HG_KEY_DIM, ROWS), F32),
            pltpu.VMEM((hb, ROWS, HG_KEY_DIM), BF16),
            pltpu.VMEM((hb, ROWS, HG_KEY_DIM), BF16),
        ],
        input_output_aliases={n_in - 1: 0},
        compiler_params=_params("parallel", "parallel"),
        name="hgrn_sample",
    )(*args)


def _head_slabs(v):
    v = v.reshape(SSM_N_GROUPS, HEADS_PER_GROUP).astype(F32)
    return jnp.pad(v, ((0, 0), (0, LANES - HEADS_PER_GROUP))).reshape(1, SSM_N_GROUPS * LANES)


def _sample_carry(conv0, seq):
    bsz, wm1, cd = conv0.shape
    return jnp.pad(conv0, ((0, 0), (seq - wm1, 0), (0, 0))).reshape(bsz * seq, cd)


def kernel(x_prompt, x_sample, p_prompt, p_sample, state_ssm, state_conv, state_hgrn, norm_mix_pre, w_in, conv_w, conv_b, dt_bias, a_log, d_skip, ssm_norm, w_br_a, hg_lb, hg_norm, w_br_b, w_out, norm_mix_post, norm_ffn_pre, w_up, w_down, norm_ffn_post, norm_ple, w_ple_gate, w_ple_proj):
    depth = w_in.shape[0]
    bp, lp, _ = x_prompt.shape
    bs, ls, _ = x_sample.shape
    tp, ts = bp * lp, bs * ls
    wm1 = SSM_CONV_WIDTH - 1
    assert lp % ROWS == 0 and ROWS % ls == 0 and ls & (ls - 1) == 0 and ts % ROWS == 0
    assert lp >= wm1 and ls >= wm1

    xp = x_prompt.reshape(tp, D_MODEL)
    xs = x_sample.reshape(ts, D_MODEL)
    row = lambda a: a.reshape(1, -1).astype(F32)
    outs = {k: [] for k in ("ssm_p", "conv_p", "hg_p", "ssm_s", "conv_s", "hg_s")}
    for li in range(depth):
        h = _rmsnorm_cast(xp, xs, row(norm_mix_pre[li]))
        proj = _in_proj(h, jnp.transpose(w_in[li]))

        tails = [proj[b * lp + lp - wm1:b * lp + lp, COL_X:COL_Q] for b in range(bp)]
        outs["conv_p"].append(jnp.stack(tails))
        outs["conv_s"].append(proj[tp:, COL_X:COL_Q].reshape(bs, ls, SSM_CONV_DIM)[:, ls - wm1:])

        ssd_params = (conv_w[li], row(conv_b[li]), _head_slabs(dt_bias[li]), _head_slabs(a_log[li]),
                      row(jnp.repeat(d_skip[li], SSM_HEAD_DIM)), row(ssm_norm[li]))
        yn, ssm_p = _ssd_prompt(proj, ssd_params, bp, lp)
        yn, ssm_s = _ssd_sample(proj, yn, _sample_carry(state_conv[li], ls), state_ssm[li], ssd_params, tp, bs, ls)
        on, hg_p = _hgrn_prompt(proj, hg_lb, row(hg_norm[li]), li, bp, lp)
        on, hg_s = _hgrn_sample(proj, on, state_hgrn[li], hg_lb, row(hg_norm[li]), li, tp, bs, ls)
        outs["ssm_p"].append(ssm_p)
        outs["ssm_s"].append(ssm_s)
        outs["hg_p"].append(hg_p)
        outs["hg_s"].append(hg_s)

        mixed = _merge(yn, w_br_a[li].astype(BF16), on, w_br_b[li].astype(BF16), proj)
        x1, h2 = _out_proj(mixed, w_out[li].astype(BF16), xp, xs, row(norm_mix_post[li]), row(norm_ffn_pre[li]))
        u = _matmul(h2, w_up[li].astype(BF16), out_dtype=BF16, act="relu2", name="ffn_up")
        x2, h3 = _ffn_down(u, w_down[li].astype(BF16), x1, row(norm_ffn_post[li]), row(norm_ple[li]))
        xp, xs = _ple(h3, w_ple_gate[li].astype(BF16), p_prompt[li].reshape(tp, -1), p_sample[li].reshape(ts, -1),
                      w_ple_proj[li].astype(BF16), x2)

    stack = lambda k: jnp.stack(outs[k])
    return (xp.reshape(bp, lp, D_MODEL), xs.reshape(bs, ls, D_MODEL),
            stack("ssm_p"), stack("conv_p"), stack("hg_p"), stack("ssm_s"), stack("conv_s"), stack("hg_s"))
```

```python
import functools
import math

import numpy as np
import jax
import jax.numpy as jnp
from jax import lax
from jax.experimental import pallas as pl
from jax.experimental.pallas import tpu as pltpu

F32 = jnp.float32
BF16 = jnp.bfloat16

D_MODEL = 2048
SSM_D_INNER = 2 * D_MODEL
SSM_HEAD_DIM = 64
SSM_N_HEADS = SSM_D_INNER // SSM_HEAD_DIM
SSM_N_GROUPS = 8
SSM_D_STATE = 128
SSM_CONV_WIDTH = 4
SSM_BC = SSM_N_GROUPS * SSM_D_STATE
SSM_CONV_DIM = SSM_D_INNER + 2 * SSM_BC
HG_KEY_DIM = 128
HG_N_HEADS = D_MODEL // HG_KEY_DIM
HG_VAL_DIM = D_MODEL // HG_N_HEADS
FFN_HIDDEN = 4 * D_MODEL
NORM_EPS = 1e-6

OFF_XBC = SSM_D_INNER
OFF_DT = OFF_XBC + SSM_CONV_DIM
OFF_HQ = OFF_DT + SSM_N_HEADS

LANES = 128
ROWS = 128
GROUP_W = SSM_D_INNER // SSM_N_GROUPS
HEADS_PER_GROUP = SSM_N_HEADS // SSM_N_GROUPS
HG_HEADS_PER_STEP = 4
SSD_GROUPS_PER_STEP = 4
LOG2E = 1.4426950408889634

COL_Z = 0
COL_X = SSM_D_INNER
COL_B = COL_X + SSM_D_INNER
COL_C = COL_B + SSM_BC
COL_Q = COL_C + SSM_BC
COL_F = COL_Q + D_MODEL
COL_I = COL_F + D_MODEL
COL_G = COL_I + D_MODEL
COL_GA = COL_G + D_MODEL
COL_GB = COL_GA + D_MODEL
COL_DT = COL_GB + D_MODEL
PROJ_W = COL_DT + SSM_N_GROUPS * LANES

VMEM_LIMIT = 52 * 1024 * 1024


def _params(*sem):
    return pltpu.CompilerParams(dimension_semantics=sem, vmem_limit_bytes=VMEM_LIMIT)


def _pick(n, cands):
    for c in cands:
        if n % c == 0:
            return c
    raise ValueError(f"no tile for {n} in {cands}")


def _dot(a, b):
    return jnp.dot(a, b, preferred_element_type=F32)


def _dot_nt(a, b):
    return lax.dot_general(a, b, (((1,), (1,)), ((), ())), preferred_element_type=F32)


def _dot_tn(a, b):
    return lax.dot_general(a, b, (((0,), (0,)), ((), ())), preferred_element_type=F32)


def _split3(x):
    hi = x.astype(BF16)
    r1 = x - hi.astype(F32)
    mid = r1.astype(BF16)
    lo = (r1 - mid.astype(F32)).astype(BF16)
    return hi, mid, lo


def _dot3_l(a01x3, x):
    return _dot(a01x3, jnp.concatenate(_split3(x), axis=0))


def _dot3_r(x, b01x3):
    return _dot(jnp.concatenate(_split3(x), axis=1), b01x3)


def _sigmoid(x):
    return 1.0 / (1.0 + jnp.exp(-x))


def _silu(x):
    return x * _sigmoid(x)


def _softplus(x):
    return jnp.maximum(x, 0.0) + jnp.log1p(jnp.exp(-jnp.abs(x)))


def _rms(x, w):
    ms = jnp.mean(x * x, axis=-1, keepdims=True)
    return x * lax.rsqrt(ms + NORM_EPS) * w


def _seq_masks(seq):
    t = np.arange(ROWS)
    same = (t[:, None] // seq) == (t[None, :] // seq)
    lower = same & (t[None, :] <= t[:, None])
    upper = same & (t[None, :] > t[:, None])
    return lower, upper


def _ssd_consts(seq):
    lower, upper = _seq_masks(seq)
    lu = np.concatenate([lower, upper], axis=0).astype(np.float32)
    rep = np.zeros((LANES, GROUP_W), np.float32)
    for j in range(HEADS_PER_GROUP):
        rep[j, j * SSM_HEAD_DIM:(j + 1) * SSM_HEAD_DIM] = 1.0
    return jnp.asarray(np.tile(lu, (1, 3)), BF16), jnp.asarray(np.tile(rep, (3, 1)), BF16)


def _level_widths(seq):
    widths, b = [], seq // 2
    while b >= 1:
        widths.append(b)
        b //= 2
    return widths


def _hgrn_consts(seq):
    lower, upper = _seq_masks(seq)
    t = np.arange(ROWS)
    masks = []
    for b in _level_widths(seq):
        is_q = (t // b) % 2 == 1
        pair = (t[:, None] // (2 * b)) == (t[None, :] // (2 * b))
        masks.append(pair & is_q[:, None] & (~is_q)[None, :])
    masks.append(np.eye(ROWS, dtype=bool))
    m_all = np.stack(masks).astype(np.float32)
    assert np.array_equal(m_all.sum(0) > 0, lower) and m_all.sum(0).max() == 1
    lu = np.concatenate([lower, upper], axis=0).astype(np.float32)
    return jnp.asarray(np.tile(lu, (1, 3)), BF16), jnp.asarray(m_all, F32)


def _group_specs(tm, n_prompt_blocks, width):
    last = n_prompt_blocks - 1
    return [
        pl.BlockSpec((tm, width), lambda i, *_: (jnp.minimum(i, last), 0)),
        pl.BlockSpec((tm, width), lambda i, *_: (jnp.maximum(i - n_prompt_blocks, 0), 0)),
    ]


def _group_rows(xp_ref, xs_ref, n_prompt_blocks):
    return jnp.where(pl.program_id(0) < n_prompt_blocks, xp_ref[...], xs_ref[...])


def _rmsnorm_body(xp_ref, xs_ref, w_ref, o_ref, *, n_prompt_blocks):
    o_ref[...] = _rms(_group_rows(xp_ref, xs_ref, n_prompt_blocks), w_ref[...]).astype(o_ref.dtype)


def _rmsnorm_cast(xp, xs, w):
    tp, d = xp.shape
    ts = xs.shape[0]
    tm = _pick(math.gcd(tp, ts), (512, 256, 128))
    return pl.pallas_call(
        functools.partial(_rmsnorm_body, n_prompt_blocks=tp // tm),
        grid=((tp + ts) // tm,),
        in_specs=_group_specs(tm, tp // tm, d) + [pl.BlockSpec((1, d), lambda i: (0, 0))],
        out_specs=pl.BlockSpec((tm, d), lambda i: (i, 0)),
        out_shape=jax.ShapeDtypeStruct((tp + ts, d), BF16),
        compiler_params=_params("parallel"),
        name="rmsnorm_cast",
    )(xp, xs, w)


IN_TN = 1024
IN_SHIFT = OFF_HQ - OFF_DT
N_SEG1 = OFF_DT // IN_TN
N_SEG2 = (COL_DT - OFF_DT) // IN_TN


def _in_proj_body(h_ref, wa_ref, wb_ref, wd_ref, o_ref, w_sc):
    j = pl.program_id(0)

    @pl.when(pl.program_id(1) == 0)
    def _():
        @pl.when(j < N_SEG1)
        def _():
            w_sc[...] = wa_ref[...].astype(BF16)

        @pl.when(jnp.logical_and(j >= N_SEG1, j < N_SEG1 + N_SEG2))
        def _():
            w_sc[0:IN_TN - IN_SHIFT] = wa_ref[IN_SHIFT:IN_TN].astype(BF16)
            w_sc[IN_TN - IN_SHIFT:IN_TN] = wb_ref[...].astype(BF16)

        @pl.when(j >= N_SEG1 + N_SEG2)
        def _():
            pad = jnp.zeros((LANES - HEADS_PER_GROUP, wd_ref.shape[1]), F32)
            for g in range(SSM_N_GROUPS):
                slab = jnp.concatenate([wd_ref[g * HEADS_PER_GROUP:(g + 1) * HEADS_PER_GROUP], pad], axis=0)
                w_sc[g * LANES:(g + 1) * LANES] = slab.astype(BF16)

    o_ref[...] = _dot_nt(h_ref[...], w_sc[...])


def _in_proj(h, wt):
    t, d = h.shape
    assert OFF_DT % IN_TN == 0 and (COL_DT - OFF_DT) % IN_TN == 0 and PROJ_W - COL_DT == IN_TN
    assert IN_TN % IN_SHIFT == 0 and IN_SHIFT % 8 == 0
    tm = _pick(t, (1088, 512, 256, 128))
    last_a = N_SEG1 + N_SEG2 - 1
    per = IN_TN // IN_SHIFT
    return pl.pallas_call(
        _in_proj_body,
        grid=(PROJ_W // IN_TN, t // tm),
        in_specs=[
            pl.BlockSpec((tm, d), lambda j, i: (i, 0)),
            pl.BlockSpec((IN_TN, d), lambda j, i: (jnp.minimum(j, last_a), 0)),
            pl.BlockSpec((IN_SHIFT, d), lambda j, i: (jnp.minimum(j, last_a) * per + per, 0)),
            pl.BlockSpec((IN_SHIFT, d), lambda j, i: (OFF_DT // IN_SHIFT, 0)),
        ],
        out_specs=pl.BlockSpec((tm, IN_TN), lambda j, i: (i, j)),
        out_shape=jax.ShapeDtypeStruct((t, PROJ_W), F32),
        scratch_shapes=[pltpu.VMEM((IN_TN, d), BF16)],
        compiler_params=_params("arbitrary", "arbitrary"),
        name="in_proj",
    )(h, wt, wt, wt)


def _matmul_body(a_ref, b_ref, o_ref, *, act):
    acc = _dot(a_ref[...], b_ref[...])
    if act == "relu2":
        acc = jnp.square(jnp.maximum(acc, 0.0))
    o_ref[...] = acc.astype(o_ref.dtype)


def _matmul(a, b, *, out_dtype, act=None, name):
    m, k = a.shape
    n = b.shape[1]
    tm = _pick(m, (1088, 512, 256, 128))
    tn = _pick(n, (1024, 512))
    return pl.pallas_call(
        functools.partial(_matmul_body, act=act),
        grid=(m // tm, n // tn),
        in_specs=[pl.BlockSpec((tm, k), lambda i, j: (i, 0)), pl.BlockSpec((k, tn), lambda i, j: (0, j))],
        out_specs=pl.BlockSpec((tm, tn), lambda i, j: (i, j)),
        out_shape=jax.ShapeDtypeStruct((m, n), out_dtype),
        compiler_params=_params("parallel", "parallel"),
        name=name,
    )(a, b)


def _merge_body(ya_ref, wa_ref, yb_ref, wb_ref, ga_ref, gb_ref, o_ref):
    a = _dot(ya_ref[...], wa_ref[...])
    b = _dot(yb_ref[...], wb_ref[...])
    o_ref[...] = (_sigmoid(ga_ref[...]) * a + _sigmoid(gb_ref[...]) * b).astype(o_ref.dtype)


def _merge(ya, wa, yb, wb, proj):
    t = ya.shape[0]
    tm = _pick(t, (544, 512, 256, 128))
    tn = 512
    return pl.pallas_call(
        _merge_body,
        grid=(t // tm, D_MODEL // tn),
        in_specs=[
            pl.BlockSpec((tm, SSM_D_INNER), lambda i, j: (i, 0)),
            pl.BlockSpec((SSM_D_INNER, tn), lambda i, j: (0, j)),
            pl.BlockSpec((tm, D_MODEL), lambda i, j: (i, 0)),
            pl.BlockSpec((D_MODEL, tn), lambda i, j: (0, j)),
            pl.BlockSpec((tm, tn), lambda i, j: (i, COL_GA // tn + j)),
            pl.BlockSpec((tm, tn), lambda i, j: (i, COL_GB // tn + j)),
        ],
        out_specs=pl.BlockSpec((tm, tn), lambda i, j: (i, j)),
        out_shape=jax.ShapeDtypeStruct((t, D_MODEL), BF16),
        compiler_params=_params("parallel", "parallel"),
        name="merge_branches",
    )(ya, wa, yb, wb, proj, proj)


def _out_body(a_ref, w_ref, xp_ref, xs_ref, npost_ref, nnext_ref, x1_ref, h_ref, *, n_prompt_blocks):
    acc = _dot(a_ref[...], w_ref[...])
    x1 = _group_rows(xp_ref, xs_ref, n_prompt_blocks) + _rms(acc, npost_ref[...])
    x1_ref[...] = x1
    h_ref[...] = _rms(x1, nnext_ref[...]).astype(h_ref.dtype)


def _out_proj(a, w, xp, xs, npost, nnext):
    t = a.shape[0]
    tp, ts = xp.shape[0], xs.shape[0]
    tm = _pick(math.gcd(tp, ts), (256, 128))
    row = lambda i: (i, 0)
    fixed = lambda i: (0, 0)
    return pl.pallas_call(
        functools.partial(_out_body, n_prompt_blocks=tp // tm),
        grid=(t // tm,),
        in_specs=[pl.BlockSpec((tm, D_MODEL), row), pl.BlockSpec((D_MODEL, D_MODEL), fixed)]
        + _group_specs(tm, tp // tm, D_MODEL)
        + [pl.BlockSpec((1, D_MODEL), fixed), pl.BlockSpec((1, D_MODEL), fixed)],
        out_specs=[pl.BlockSpec((tm, D_MODEL), row), pl.BlockSpec((tm, D_MODEL), row)],
        out_shape=[jax.ShapeDtypeStruct((t, D_MODEL), F32), jax.ShapeDtypeStruct((t, D_MODEL), BF16)],
        compiler_params=_params("parallel"),
        name="out_proj_norm",
    )(a, w, xp, xs, npost, nnext)


def _down_body(u_ref, w_ref, x_ref, npost_ref, nnext_ref, x2_ref, h_ref, acc_ref):
    k = pl.program_id(1)

    @pl.when(k == 0)
    def _():
        acc_ref[...] = jnp.zeros_like(acc_ref)

    acc_ref[...] += _dot(u_ref[...], w_ref[...])

    @pl.when(k == pl.num_programs(1) - 1)
    def _():
        x2 = x_ref[...] + _rms(acc_ref[...], npost_ref[...])
        x2_ref[...] = x2
        h_ref[...] = _rms(x2, nnext_ref[...]).astype(h_ref.dtype)


def _ffn_down(u, w, x, npost, nnext):
    t, f = u.shape
    tm = _pick(t, (544, 512, 256, 128))
    tk = 2048
    row = lambda i, k: (i, 0)
    fixed = lambda i, k: (0, 0)
    return pl.pallas_call(
        _down_body,
        grid=(t // tm, f // tk),
        in_specs=[
            pl.BlockSpec((tm, tk), lambda i, k: (i, k)),
            pl.BlockSpec((tk, D_MODEL), lambda i, k: (k, 0)),
            pl.BlockSpec((tm, D_MODEL), row),
            pl.BlockSpec((1, D_MODEL), fixed),
            pl.BlockSpec((1, D_MODEL), fixed),
        ],
        out_specs=[pl.BlockSpec((tm, D_MODEL), row), pl.BlockSpec((tm, D_MODEL), row)],
        out_shape=[jax.ShapeDtypeStruct((t, D_MODEL), F32), jax.ShapeDtypeStruct((t, D_MODEL), BF16)],
        scratch_shapes=[pltpu.VMEM((tm, D_MODEL), F32)],
        compiler_params=_params("parallel", "arbitrary"),
        name="ffn_down_norm",
    )(u, w, x, npost, nnext)


def _ple_body(h_ref, wg_ref, pp_ref, ps_ref, wp_ref, x_ref, yp_ref, ys_ref, *, n_prompt_blocks):
    i = pl.program_id(0)
    g = _sigmoid(_dot(h_ref[...], wg_ref[...]))
    e = _dot(_group_rows(pp_ref, ps_ref, n_prompt_blocks).astype(BF16), wp_ref[...])
    y = x_ref[...] + g * e

    @pl.when(i < n_prompt_blocks)
    def _():
        yp_ref[...] = y

    @pl.when(i >= n_prompt_blocks)
    def _():
        ys_ref[...] = y


def _ple(h, wg, pp, ps, wp, x):
    tp, pd = pp.shape
    ts = ps.shape[0]
    tm = _pick(math.gcd(tp, ts), (256, 128))
    npb = tp // tm
    row = lambda i: (i, 0)
    fixed = lambda i: (0, 0)
    return pl.pallas_call(
        functools.partial(_ple_body, n_prompt_blocks=npb),
        grid=((tp + ts) // tm,),
        in_specs=[pl.BlockSpec((tm, D_MODEL), row), pl.BlockSpec((D_MODEL, D_MODEL), fixed)]
        + _group_specs(tm, npb, pd)
        + [pl.BlockSpec((pd, D_MODEL), fixed), pl.BlockSpec((tm, D_MODEL), row)],
        out_specs=_group_specs(tm, npb, D_MODEL),
        out_shape=[jax.ShapeDtypeStruct((tp, D_MODEL), F32), jax.ShapeDtypeStruct((ts, D_MODEL), F32)],
        compiler_params=_params("arbitrary"),
        name="ple_gate",
    )(h, wg, pp, ps, wp, x)


def _conv_taps(cur, shifted, cw, cb):
    acc = cw[SSM_CONV_WIDTH - 1:SSM_CONV_WIDTH, :] * cur
    for d in range(1, SSM_CONV_WIDTH):
        acc = acc + cw[SSM_CONV_WIDTH - 1 - d:SSM_CONV_WIDTH - d, :] * shifted[d]
    return _silu(acc + cb)


def _conv_prompt(cur_ref, halo_sc, cw_ref, cb_ref):
    cur = cur_ref[...]
    top = cur[0:8]
    halo = halo_sc[...]
    row8 = lax.broadcasted_iota(jnp.int32, top.shape, 0)
    shifted, shifted_top = {}, {}
    for d in range(1, SSM_CONV_WIDTH):
        shifted[d] = pltpu.roll(cur, d, 0)
        shifted_top[d] = jnp.where(row8 < d, pltpu.roll(halo, d, 0), pltpu.roll(top, d, 0))
    cw = cw_ref[...]
    cb = cb_ref[...]
    out = _conv_taps(cur, shifted, cw, cb)
    out_top = _conv_taps(top, shifted_top, cw, cb)
    halo_sc[...] = cur[ROWS - 8:ROWS]
    return jnp.concatenate([out_top, out[8:]], axis=0)


def _conv_sample(cur_ref, carry_ref, cw_ref, cb_ref, seq):
    cur = cur_ref[...]
    carry = carry_ref[...]
    pos = lax.broadcasted_iota(jnp.int32, cur.shape, 0) & (seq - 1)
    shifted = {}
    for d in range(1, SSM_CONV_WIDTH):
        shifted[d] = jnp.where(pos >= d, pltpu.roll(cur, d, 0), pltpu.roll(carry, ROWS - (seq - d), 0))
    return _conv_taps(cur, shifted, cw_ref[...], cb_ref[...])


def _ssd_blocks(xg, bg_bf, cg_bf, dt_raw, dtb, alog, lu_ref, rep_ref, y_sc):
    groups = range(len(xg))
    lu = lu_ref[...]
    mask = lu[0:ROWS, 0:ROWS].astype(F32) > 0.5
    dt = [_softplus(dt_raw[g] + dtb[g]) for g in groups]
    cr = [_dot3_l(lu, dt[g] * (-jnp.exp(alog[g]))) for g in groups]
    cum = [c[0:ROWS] for c in cr]
    col = [c * LOG2E for c in cum]
    row_t = [(col[g] - jnp.log2(dt[g])).T for g in groups]
    x_bf = [x.astype(BF16) for x in xg]
    cb = [_dot_nt(cg_bf[g], bg_bf[g]) for g in groups]
    for j in range(HEADS_PER_GROUP):
        hs = slice(j * SSM_HEAD_DIM, (j + 1) * SSM_HEAD_DIM)
        for g in groups:
            seg = col[g][:, j:j + 1] - row_t[g][j:j + 1, :]
            m = (cb[g] * jnp.where(mask, jnp.exp2(seg), 0.0)).astype(BF16)
            y_sc[:, g * GROUP_W + j * SSM_HEAD_DIM:g * GROUP_W + (j + 1) * SSM_HEAD_DIM] = _dot(m, x_bf[g][:, hs])
    rep = rep_ref[...]
    slabs = [_dot3_r(jnp.concatenate([jnp.exp2(col[g]), dt[g] * jnp.exp(cr[g][ROWS:2 * ROWS])], axis=0), rep)
             for g in groups]
    return cum, [sl[0:ROWS] for sl in slabs], [sl[ROWS:2 * ROWS] for sl in slabs]


def _head_rows(e_row):
    return jnp.concatenate(
        [jnp.broadcast_to(e_row[0:1, j:j + 1], (SSM_HEAD_DIM, SSM_D_STATE)) for j in range(HEADS_PER_GROUP)], axis=0)


def _gated_group_norm(y, z, nw):
    return _rms(y * _silu(z), nw)


def _ssd_prompt_body(x_ref, b_ref, c_ref, z_ref, dt_ref, cwx_ref, cwb_ref, cwc_ref, cbx_ref, cbb_ref, cbc_ref,
                     dtb_ref, alog_ref, dsk_ref, nw_ref, lu_ref, rep_ref,
                     yn_ref, hout_ref, h_sc, px_sc, pb_sc, pc_sc, y_sc):
    c = pl.program_id(2)
    groups = range(SSD_GROUPS_PER_STEP)

    @pl.when(c == 0)
    def _():
        h_sc[...] = jnp.zeros_like(h_sc)
        px_sc[...] = jnp.zeros_like(px_sc)
        pb_sc[...] = jnp.zeros_like(pb_sc)
        pc_sc[...] = jnp.zeros_like(pc_sc)

    x_all = _conv_prompt(x_ref, px_sc, cwx_ref, cbx_ref)
    b_all = _conv_prompt(b_ref, pb_sc, cwb_ref, cbb_ref).astype(BF16)
    c_all = _conv_prompt(c_ref, pc_sc, cwc_ref, cbc_ref).astype(BF16)
    wide = [slice(g * GROUP_W, (g + 1) * GROUP_W) for g in groups]
    narrow = [slice(g * SSM_D_STATE, (g + 1) * SSM_D_STATE) for g in groups]
    xg = [x_all[:, wide[g]] for g in groups]
    bg = [b_all[:, narrow[g]] for g in groups]
    cg = [c_all[:, narrow[g]] for g in groups]
    cum, e_rep, w_rep = _ssd_blocks(
        xg, bg, cg, [dt_ref[:, narrow[g]] for g in groups], [dtb_ref[:, narrow[g]] for g in groups],
        [alog_ref[:, narrow[g]] for g in groups], lu_ref, rep_ref, y_sc)
    h = [h_sc[wide[g]] for g in groups]
    y_inter = [_dot_nt(cg[g], h[g].astype(BF16)) for g in groups]
    upd = [_dot_tn((xg[g] * w_rep[g]).astype(BF16), bg[g]) for g in groups]
    for g in groups:
        y = y_sc[:, wide[g]] + y_inter[g] * e_rep[g] + dsk_ref[:, wide[g]] * xg[g]
        yn_ref[:, wide[g]] = _gated_group_norm(y, z_ref[:, wide[g]], nw_ref[:, wide[g]]).astype(yn_ref.dtype)
        h_sc[wide[g]] = _head_rows(jnp.exp(cum[g][ROWS - 1:ROWS, :])) * h[g] + upd[g]

    @pl.when(c == pl.num_programs(2) - 1)
    def _():
        hout_ref[...] = h_sc[...].reshape(hout_ref.shape)


SLAB = 16


def _ssd_sample_body(x_ref, b_ref, c_ref, z_ref, dt_ref, hx_ref, hb_ref, hc_ref, st_ref,
                     cwx_ref, cwb_ref, cwc_ref, cbx_ref, cbb_ref, cbc_ref,
                     dtb_ref, alog_ref, dsk_ref, nw_ref, lu_ref, rep_ref,
                     yn_ref, hout_ref, y_sc, yi_sc, cum_sc, c_sc, *, seq):
    xg = _conv_sample(x_ref, hx_ref, cwx_ref, cbx_ref, seq)
    bg = _conv_sample(b_ref, hb_ref, cwb_ref, cbb_ref, seq)
    cg = _conv_sample(c_ref, hc_ref, cwc_ref, cbc_ref, seq)
    c_bf = cg.astype(BF16)
    cums, e_reps, w_reps = _ssd_blocks([xg], [bg.astype(BF16)], [c_bf], [dt_ref[...]], [dtb_ref[...]], [alog_ref[...]],
                                       lu_ref, rep_ref, y_sc)
    cum_sc[...] = cums[0]
    c_sc[...] = c_bf
    yi_sc[...] = jnp.zeros_like(yi_sc)
    xw_t = (xg * w_reps[0]).T.astype(BF16)
    shift = seq.bit_length() - 1
    per_slab = SLAB // seq
    slab_seq = lax.shift_right_logical(lax.broadcasted_iota(jnp.int32, (SLAB, GROUP_W), 0), shift)
    seq_of_row_n = lax.shift_right_logical(lax.broadcasted_iota(jnp.int32, (ROWS, SSM_D_STATE), 0), shift)
    state_shape = (HEADS_PER_GROUP * SSM_HEAD_DIM, SSM_D_STATE)

    def per_sequence(s, carry):
        h0 = st_ref[s].reshape(state_shape)
        r0 = pl.multiple_of((s // per_slab) * SLAB, SLAB)
        y_inter = _dot_nt(c_sc[pl.ds(r0, SLAB), :], h0.astype(BF16))
        yi_sc[pl.ds(r0, SLAB), :] += jnp.where(slab_seq == s % per_slab, y_inter, 0.0)
        b_own = jnp.where(seq_of_row_n == s, bg, 0.0).astype(BF16)
        last = cum_sc[pl.ds(s * seq + seq - 1, 1), :]
        h_new = _head_rows(jnp.exp(last)) * h0 + _dot(xw_t, b_own)
        hout_ref[s] = h_new.reshape(hout_ref.shape[1:])
        return carry

    lax.fori_loop(0, ROWS // seq, per_sequence, 0, unroll=2)
    y = y_sc[...] + yi_sc[...] * e_reps[0] + dsk_ref[...] * xg
    yn_ref[...] = _gated_group_norm(y, z_ref[...], nw_ref[...]).astype(yn_ref.dtype)


def _ssd_param_specs(gidx, ng):
    wide, narrow = ng * GROUP_W, ng * SSM_D_STATE

    def col(block, off):
        return lambda *ids: (0, off // block + gidx(*ids))
    return [
        pl.BlockSpec((SSM_CONV_WIDTH, wide), col(wide, 0)),
        pl.BlockSpec((SSM_CONV_WIDTH, narrow), col(narrow, SSM_D_INNER)),
        pl.BlockSpec((SSM_CONV_WIDTH, narrow), col(narrow, SSM_D_INNER + SSM_BC)),
        pl.BlockSpec((1, wide), col(wide, 0)),
        pl.BlockSpec((1, narrow), col(narrow, SSM_D_INNER)),
        pl.BlockSpec((1, narrow), col(narrow, SSM_D_INNER + SSM_BC)),
        pl.BlockSpec((1, narrow), col(narrow, 0)),
        pl.BlockSpec((1, narrow), col(narrow, 0)),
        pl.BlockSpec((1, wide), col(wide, 0)),
        pl.BlockSpec((1, wide), col(wide, 0)),
        pl.BlockSpec((2 * ROWS, 3 * ROWS), lambda *ids: (0, 0)),
        pl.BlockSpec((3 * LANES, GROUP_W), lambda *ids: (0, 0)),
    ]


def _ssd_token_specs(ridx, gidx, ng):
    wide, narrow = ng * GROUP_W, ng * SSM_D_STATE

    def col(block, off):
        return lambda *ids: (ridx(*ids), off // block + gidx(*ids))
    return [
        pl.BlockSpec((ROWS, wide), col(wide, COL_X)),
        pl.BlockSpec((ROWS, narrow), col(narrow, COL_B)),
        pl.BlockSpec((ROWS, narrow), col(narrow, COL_C)),
        pl.BlockSpec((ROWS, wide), col(wide, COL_Z)),
        pl.BlockSpec((ROWS, narrow), col(narrow, COL_DT)),
    ]


def _ssd_prompt(proj, ssd_params, batch, length):
    nblk = length // ROWS
    ng = SSD_GROUPS_PER_STEP
    lu, rep = _ssd_consts(ROWS)
    ridx = lambda b, g, c: b * nblk + c
    gidx = lambda b, g, c: g
    conv_w, conv_b, dtb, alog, dsk, nw = ssd_params
    args = [proj] * 5 + [conv_w] * 3 + [conv_b] * 3 + [dtb, alog, dsk, nw, lu, rep]
    return pl.pallas_call(
        _ssd_prompt_body,
        grid=(batch, SSM_N_GROUPS // ng, nblk),
        in_specs=_ssd_token_specs(ridx, gidx, ng) + _ssd_param_specs(gidx, ng),
        out_specs=[
            pl.BlockSpec((ROWS, ng * GROUP_W), lambda b, g, c: (b * nblk + c, g)),
            pl.BlockSpec((1, ng * HEADS_PER_GROUP, SSM_HEAD_DIM, SSM_D_STATE), lambda b, g, c: (b, g, 0, 0)),
        ],
        out_shape=[
            jax.ShapeDtypeStruct((proj.shape[0], SSM_D_INNER), BF16),
            jax.ShapeDtypeStruct((batch, SSM_N_HEADS, SSM_HEAD_DIM, SSM_D_STATE), F32),
        ],
        scratch_shapes=[
            pltpu.VMEM((ng * HEADS_PER_GROUP * SSM_HEAD_DIM, SSM_D_STATE), F32),
            pltpu.VMEM((8, ng * GROUP_W), F32),
            pltpu.VMEM((8, ng * SSM_D_STATE), F32),
            pltpu.VMEM((8, ng * SSM_D_STATE), F32),
            pltpu.VMEM((ROWS, ng * GROUP_W), F32),
        ],
        compiler_params=_params("parallel", "parallel", "arbitrary"),
        name="ssd_prompt",
    )(*args)


def _ssd_sample(proj, yn, carry, state, ssd_params, row0, batch, seq):
    per_blk = ROWS // seq
    nblk = batch // per_blk
    blk0 = row0 // ROWS
    lu, rep = _ssd_consts(seq)
    ridx = lambda m, g: blk0 + m
    gidx = lambda m, g: g
    conv_w, conv_b, dtb, alog, dsk, nw = ssd_params

    def carry_spec(block, off):
        return pl.BlockSpec((ROWS, block), lambda m, g: (m, off // block + g))

    state_spec = pl.BlockSpec((per_blk, HEADS_PER_GROUP, SSM_HEAD_DIM, SSM_D_STATE), lambda m, g: (m, g, 0, 0))
    args = [proj] * 5 + [carry] * 3 + [state] + [conv_w] * 3 + [conv_b] * 3 + [dtb, alog, dsk, nw, lu, rep, yn]
    n_in = len(args)
    return pl.pallas_call(
        lambda *refs: _ssd_sample_body(*refs[:n_in - 1], *refs[n_in:], seq=seq),
        grid=(nblk, SSM_N_GROUPS),
        in_specs=_ssd_token_specs(ridx, gidx, 1)
        + [carry_spec(GROUP_W, 0), carry_spec(SSM_D_STATE, SSM_D_INNER), carry_spec(SSM_D_STATE, SSM_D_INNER + SSM_BC),
           state_spec]
        + _ssd_param_specs(gidx, 1) + [pl.BlockSpec(memory_space=pl.ANY)],
        out_specs=[pl.BlockSpec((ROWS, GROUP_W), lambda m, g: (blk0 + m, g)), state_spec],
        out_shape=[jax.ShapeDtypeStruct(yn.shape, yn.dtype), jax.ShapeDtypeStruct(state.shape, F32)],
        scratch_shapes=[
            pltpu.VMEM((ROWS, GROUP_W), F32),
            pltpu.VMEM((ROWS, GROUP_W), F32),
            pltpu.VMEM((ROWS, LANES), F32),
            pltpu.VMEM((ROWS, SSM_D_STATE), BF16),
        ],
        input_output_aliases={n_in - 1: 0},
        compiler_params=_params("parallel", "parallel"),
        name="ssd_sample",
    )(*args)


def _lower_bound(raw, layer):
    e = jnp.exp(raw - jnp.max(raw, axis=0, keepdims=True))
    return jnp.sum(e[0:layer + 1], axis=0, keepdims=True) / jnp.sum(e, axis=0, keepdims=True)


def _head_cols(ref, hh):
    return ref[:, hh * HG_KEY_DIM:(hh + 1) * HG_KEY_DIM]


def _level_ref(cum, b):
    width = cum.shape[1]
    if b >= 8:
        blocks = [jnp.broadcast_to(cum[p * 2 * b + b - 1:p * 2 * b + b, :], (2 * b, width))
                  for p in range(ROWS // (2 * b))]
        return blocks[0] if len(blocks) == 1 else jnp.concatenate(blocks, axis=0)
    if b == 1:
        odd = (lax.broadcasted_iota(jnp.int32, cum.shape, 0) & 1) == 1
        return jnp.where(odd, pltpu.roll(cum, 1, 0), cum)
    tiles = cum.reshape(ROWS // 8, 8, width)
    if b == 4:
        ref = jnp.broadcast_to(tiles[:, 3:4, :], tiles.shape)
    else:
        sub = lax.broadcasted_iota(jnp.int32, tiles.shape, 1)
        ref = jnp.where(sub < 4, jnp.broadcast_to(tiles[:, 1:2, :], tiles.shape),
                        jnp.broadcast_to(tiles[:, 5:6, :], tiles.shape))
    return ref.reshape(cum.shape)


def _hgrn_blocks(q_ref, f_ref, i_ref, lbraw_ref, lu_ref, m_ref, layer, seq):
    heads = range(HG_HEADS_PER_STEP)
    widths = _level_widths(seq)
    lu = lu_ref[...]
    q = [_head_cols(q_ref, hh) for hh in heads]
    v_bf = [_head_cols(i_ref, hh).astype(BF16) for hh in heads]
    k, log_f = [], []
    for hh in heads:
        hf = _head_cols(f_ref, hh)
        lb = _lower_bound(_head_cols(lbraw_ref, hh), layer)
        log_f.append(jnp.log(lb + (1.0 - lb) * _sigmoid(hf)))
        k.append((1.0 - lb) * _sigmoid(-hf))
    sums = [_dot3_l(lu, p) for p in log_f]
    cum = [s[0:ROWS] for s in sums]
    att = [m_ref[len(widths)] * _dot_nt(q[hh].astype(BF16), k[hh].astype(BF16)) for hh in heads]
    for lvl, b in enumerate(widths):
        for hh in heads:
            e = jnp.exp(-jnp.abs(cum[hh] - _level_ref(cum[hh], b)))
            att[hh] = att[hh] + m_ref[lvl] * _dot_nt((q[hh] * e).astype(BF16), (k[hh] * e).astype(BF16))
    o_intra = [_dot(att[hh].astype(BF16), v_bf[hh]) for hh in heads]
    q_dec = [(q[hh] * jnp.exp(cum[hh])).astype(BF16) for hh in heads]
    k_dec = [(k[hh] * jnp.exp(sums[hh][ROWS:2 * ROWS])).astype(BF16) for hh in heads]
    return o_intra, q_dec, k_dec, v_bf, cum


def _hgrn_gate_norm(o, g, nw):
    return (_rms(o, nw) * _silu(g)).astype(BF16)


def _hgrn_prompt_body(q_ref, f_ref, i_ref, g_ref, lbraw_ref, nw_ref, lu_ref, m_ref,
                      on_ref, sout_ref, s_sc, *, layer):
    c = pl.program_id(2)

    @pl.when(c == 0)
    def _():
        s_sc[...] = jnp.zeros_like(s_sc)

    heads = range(HG_HEADS_PER_STEP)
    o_intra, q_dec, k_dec, v_bf, cum = _hgrn_blocks(q_ref, f_ref, i_ref, lbraw_ref, lu_ref, m_ref, layer, ROWS)
    s = [s_sc[hh] for hh in heads]
    o = [o_intra[hh] + _dot(q_dec[hh], s[hh].astype(BF16)) for hh in heads]
    upd = [_dot_tn(k_dec[hh], v_bf[hh]) for hh in heads]
    for hh in heads:
        on_ref[:, hh * HG_VAL_DIM:(hh + 1) * HG_VAL_DIM] = _hgrn_gate_norm(o[hh], _head_cols(g_ref, hh), nw_ref[...])
        last_col = cum[hh].T[:, ROWS - 1:ROWS]
        s_sc[hh] = jnp.exp(last_col) * s[hh] + upd[hh]

    @pl.when(c == pl.num_programs(2) - 1)
    def _():
        sout_ref[...] = s_sc[...].reshape(sout_ref.shape)


def _hgrn_sample_body(q_ref, f_ref, i_ref, g_ref, st_ref, lbraw_ref, nw_ref, lu_ref, m_ref,
                      on_ref, sout_ref, o_sc, cumt_sc, qd_sc, kdt_sc, *, layer, seq):
    o_intra, q_dec, k_dec, _, cum = _hgrn_blocks(q_ref, f_ref, i_ref, lbraw_ref, lu_ref, m_ref, layer, seq)
    for hh in range(HG_HEADS_PER_STEP):
        o_sc[hh] = o_intra[hh]
        cumt_sc[hh] = cum[hh].T
        qd_sc[hh] = q_dec[hh]
        kdt_sc[hh] = k_dec[hh].astype(F32).T.astype(BF16)
    shift = seq.bit_length() - 1
    per_slab = SLAB // seq
    slab_seq = lax.shift_right_logical(lax.broadcasted_iota(jnp.int32, (SLAB, HG_VAL_DIM), 0), shift)
    seq_of_row = lax.shift_right_logical(lax.broadcasted_iota(jnp.int32, (ROWS, HG_VAL_DIM), 0), shift)
    lane = lax.broadcasted_iota(jnp.int32, (HG_KEY_DIM, ROWS), 1)

    def per_sequence(s, carry):
        own = seq_of_row == s
        is_last = lane == s * seq + seq - 1
        r0 = pl.multiple_of((s // per_slab) * SLAB, SLAB)
        heads = range(HG_HEADS_PER_STEP)
        s0 = [st_ref[s, hh] for hh in heads]
        o_inter = [_dot(qd_sc[hh, pl.ds(r0, SLAB), :], s0[hh].astype(BF16)) for hh in heads]
        upd = [_dot(kdt_sc[hh], jnp.where(own, _head_cols(i_ref, hh), 0.0).astype(BF16)) for hh in heads]
        for hh in heads:
            o_sc[hh, pl.ds(r0, SLAB), :] += jnp.where(slab_seq == s % per_slab, o_inter[hh], 0.0)
            last_col = jnp.sum(jnp.where(is_last, cumt_sc[hh], 0.0), axis=1, keepdims=True)
            sout_ref[s, hh] = jnp.exp(last_col) * s0[hh] + upd[hh]
        return carry

    lax.fori_loop(0, ROWS // seq, per_sequence, 0, unroll=2)
    for hh in range(HG_HEADS_PER_STEP):
        on_ref[:, hh * HG_VAL_DIM:(hh + 1) * HG_VAL_DIM] = _hgrn_gate_norm(o_sc[hh], _head_cols(g_ref, hh), nw_ref[...])


def _hgrn_token_specs(ridx, hidx):
    width = HG_HEADS_PER_STEP * HG_KEY_DIM

    def col(off):
        return lambda *ids: (ridx(*ids), off // width + hidx(*ids))
    return [pl.BlockSpec((ROWS, width), col(off)) for off in (COL_Q, COL_F, COL_I, COL_G)]


def _hgrn_param_specs(hidx, n_rows, n_masks):
    return [
        pl.BlockSpec((n_rows, HG_HEADS_PER_STEP * HG_KEY_DIM), lambda *ids: (0, hidx(*ids))),
        pl.BlockSpec((1, HG_VAL_DIM), lambda *ids: (0, 0)),
        pl.BlockSpec((2 * ROWS, 3 * ROWS), lambda *ids: (0, 0)),
        pl.BlockSpec((n_masks, ROWS, ROWS), lambda *ids: (0, 0, 0)),
    ]


def _hgrn_prompt(proj, hg_lb, hg_norm, layer, batch, length):
    nblk = length // ROWS
    hb = HG_HEADS_PER_STEP
    lu, m_all = _hgrn_consts(ROWS)
    ridx = lambda b, h, c: b * nblk + c
    hidx = lambda b, h, c: h
    return pl.pallas_call(
        functools.partial(_hgrn_prompt_body, layer=layer),
        grid=(batch, HG_N_HEADS // hb, nblk),
        in_specs=_hgrn_token_specs(ridx, hidx) + _hgrn_param_specs(hidx, hg_lb.shape[0], m_all.shape[0]),
        out_specs=[
            pl.BlockSpec((ROWS, hb * HG_VAL_DIM), lambda b, h, c: (b * nblk + c, h)),
            pl.BlockSpec((1, hb, HG_KEY_DIM, HG_VAL_DIM), lambda b, h, c: (b, h, 0, 0)),
        ],
        out_shape=[
            jax.ShapeDtypeStruct((proj.shape[0], D_MODEL), BF16),
            jax.ShapeDtypeStruct((batch, HG_N_HEADS, HG_KEY_DIM, HG_VAL_DIM), F32),
        ],
        scratch_shapes=[pltpu.VMEM((hb, HG_KEY_DIM, HG_VAL_DIM), F32)],
        compiler_params=_params("parallel", "parallel", "arbitrary"),
        name="hgrn_prompt",
    )(*([proj] * 4 + [hg_lb, hg_norm, lu, m_all]))


def _hgrn_sample(proj, on, state, hg_lb, hg_norm, layer, row0, batch, seq):
    per_blk = ROWS // seq
    nblk = batch // per_blk
    blk0 = row0 // ROWS
    hb = HG_HEADS_PER_STEP
    lu, m_all = _hgrn_consts(seq)
    ridx = lambda m, h: blk0 + m
    hidx = lambda m, h: h
    state_spec = pl.BlockSpec((per_blk, hb, HG_KEY_DIM, HG_VAL_DIM), lambda m, h: (m, h, 0, 0))
    args = [proj] * 4 + [state, hg_lb, hg_norm, lu, m_all, on]
    n_in = len(args)
    return pl.pallas_call(
        lambda *refs: functools.partial(_hgrn_sample_body, layer=layer, seq=seq)(
            *refs[:n_in - 1], *refs[n_in:]),
        grid=(nblk, HG_N_HEADS // hb),
        in_specs=_hgrn_token_specs(ridx, hidx) + [state_spec]
        + _hgrn_param_specs(hidx, hg_lb.shape[0], m_all.shape[0]) + [pl.BlockSpec(memory_space=pl.ANY)],
        out_specs=[pl.BlockSpec((ROWS, hb * HG_VAL_DIM), lambda m, h: (blk0 + m, h)), state_spec],
        out_shape=[jax.ShapeDtypeStruct(on.shape, on.dtype), jax.ShapeDtypeStruct(state.shape, F32)],
        scratch_shapes=[
            pltpu.VMEM((hb, ROWS, HG_VAL_DIM), F32),
            pltpu.VMEM((hb, HG_KEY_DIM, ROWS), F32),
            pltpu.VMEM((hb, ROWS, HG_KEY_DIM), BF16),
            pltpu.VMEM((hb, HG_KEY_DIM, ROWS), BF16),
        ],
        input_output_aliases={n_in - 1: 0},
        compiler_params=_params("parallel", "parallel"),
        name="hgrn_sample",
    )(*args)


def _head_slabs(v):
    v = v.reshape(SSM_N_GROUPS, HEADS_PER_GROUP).astype(F32)
    return jnp.pad(v, ((0, 0), (0, LANES - HEADS_PER_GROUP))).reshape(1, SSM_N_GROUPS * LANES)


def _sample_carry(conv0, seq):
    bsz, wm1, cd = conv0.shape
    return jnp.pad(conv0, ((0, 0), (seq - wm1, 0), (0, 0))).reshape(bsz * seq, cd)


def kernel(x_prompt, x_sample, p_prompt, p_sample, state_ssm, state_conv, state_hgrn, norm_mix_pre, w_in, conv_w, conv_b, dt_bias, a_log, d_skip, ssm_norm, w_br_a, hg_lb, hg_norm, w_br_b, w_out, norm_mix_post, norm_ffn_pre, w_up, w_down, norm_ffn_post, norm_ple, w_ple_gate, w_ple_proj):
    depth = w_in.shape[0]
    bp, lp, _ = x_prompt.shape
    bs, ls, _ = x_sample.shape
    tp, ts = bp * lp, bs * ls
    wm1 = SSM_CONV_WIDTH - 1
    assert lp % ROWS == 0 and ROWS % ls == 0 and ls & (ls - 1) == 0 and ts % ROWS == 0
    assert lp >= wm1 and ls >= wm1

    xp = x_prompt.reshape(tp, D_MODEL)
    xs = x_sample.reshape(ts, D_MODEL)
    row = lambda a: a.reshape(1, -1).astype(F32)
    outs = {k: [] for k in ("ssm_p", "conv_p", "hg_p", "ssm_s", "conv_s", "hg_s")}
    for li in range(depth):
        h = _rmsnorm_cast(xp, xs, row(norm_mix_pre[li]))
        proj = _in_proj(h, jnp.transpose(w_in[li]))

        tails = [proj[b * lp + lp - wm1:b * lp + lp, COL_X:COL_Q] for b in range(bp)]
        outs["conv_p"].append(jnp.stack(tails))
        outs["conv_s"].append(proj[tp:, COL_X:COL_Q].reshape(bs, ls, SSM_CONV_DIM)[:, ls - wm1:])

        ssd_params = (conv_w[li], row(conv_b[li]), _head_slabs(dt_bias[li]), _head_slabs(a_log[li]),
                      row(jnp.repeat(d_skip[li], SSM_HEAD_DIM)), row(ssm_norm[li]))
        yn, ssm_p = _ssd_prompt(proj, ssd_params, bp, lp)
        yn, ssm_s = _ssd_sample(proj, yn, _sample_carry(state_conv[li], ls), state_ssm[li], ssd_params, tp, bs, ls)
        on, hg_p = _hgrn_prompt(proj, hg_lb, row(hg_norm[li]), li, bp, lp)
        on, hg_s = _hgrn_sample(proj, on, state_hgrn[li], hg_lb, row(hg_norm[li]), li, tp, bs, ls)
        outs["ssm_p"].append(ssm_p)
        outs["ssm_s"].append(ssm_s)
        outs["hg_p"].append(hg_p)
        outs["hg_s"].append(hg_s)

        mixed = _merge(yn, w_br_a[li].astype(BF16), on, w_br_b[li].astype(BF16), proj)
        x1, h2 = _out_proj(mixed, w_out[li].astype(BF16), xp, xs, row(norm_mix_post[li]), row(norm_ffn_pre[li]))
        u = _matmul(h2, w_up[li].astype(BF16), out_dtype=BF16, act="relu2", name="ffn_up")
        x2, h3 = _ffn_down(u, w_down[li].astype(BF16), x1, row(norm_ffn_post[li]), row(norm_ple[li]))
        xp, xs = _ple(h3, w_ple_gate[li].astype(BF16), p_prompt[li].reshape(tp, -1), p_sample[li].reshape(ts, -1),
                      w_ple_proj[li].astype(BF16), x2)

    stack = lambda k: jnp.stack(outs[k])
    return (xp.reshape(bp, lp, D_MODEL), xs.reshape(bs, ls, D_MODEL),
            stack("ssm_p"), stack("conv_p"), stack("hg_p"), stack("ssm_s"), stack("conv_s"), stack("hg_s"))
```

```python
import functools
import math

import numpy as np
import jax
import jax.numpy as jnp
from jax import lax
from jax.experimental import pallas as pl
from jax.experimental.pallas import tpu as pltpu

F32 = jnp.float32
BF16 = jnp.bfloat16

D_MODEL = 2048
SSM_D_INNER = 2 * D_MODEL
SSM_HEAD_DIM = 64
SSM_N_HEADS = SSM_D_INNER // SSM_HEAD_DIM
SSM_N_GROUPS = 8
SSM_D_STATE = 128
SSM_CONV_WIDTH = 4
SSM_BC = SSM_N_GROUPS * SSM_D_STATE
SSM_CONV_DIM = SSM_D_INNER + 2 * SSM_BC
HG_KEY_DIM = 128
HG_N_HEADS = D_MODEL // HG_KEY_DIM
HG_VAL_DIM = D_MODEL // HG_N_HEADS
FFN_HIDDEN = 4 * D_MODEL
NORM_EPS = 1e-6

OFF_XBC = SSM_D_INNER
OFF_DT = OFF_XBC + SSM_CONV_DIM
OFF_HQ = OFF_DT + SSM_N_HEADS

LANES = 128
ROWS = 128
GROUP_W = SSM_D_INNER // SSM_N_GROUPS
HEADS_PER_GROUP = SSM_N_HEADS // SSM_N_GROUPS
HG_HEADS_PER_STEP = 4
SSD_GROUPS_PER_STEP = 4
LOG2E = 1.4426950408889634

COL_Z = 0
COL_X = SSM_D_INNER
COL_B = COL_X + SSM_D_INNER
COL_C = COL_B + SSM_BC
COL_Q = COL_C + SSM_BC
COL_F = COL_Q + D_MODEL
COL_I = COL_F + D_MODEL
COL_G = COL_I + D_MODEL
COL_GA = COL_G + D_MODEL
COL_GB = COL_GA + D_MODEL
COL_DT = COL_GB + D_MODEL
PROJ_W = COL_DT + SSM_N_GROUPS * LANES

VMEM_LIMIT = 52 * 1024 * 1024


def _params(*sem):
    return pltpu.CompilerParams(dimension_semantics=sem, vmem_limit_bytes=VMEM_LIMIT)


def _pick(n, cands):
    for c in cands:
        if n % c == 0:
            return c
    raise ValueError(f"no tile for {n} in {cands}")


def _dot(a, b):
    return jnp.dot(a, b, preferred_element_type=F32)


def _dot_nt(a, b):
    return lax.dot_general(a, b, (((1,), (1,)), ((), ())), preferred_element_type=F32)


def _dot_tn(a, b):
    return lax.dot_general(a, b, (((0,), (0,)), ((), ())), preferred_element_type=F32)


def _split3(x):
    hi = x.astype(BF16)
    r1 = x - hi.astype(F32)
    mid = r1.astype(BF16)
    lo = (r1 - mid.astype(F32)).astype(BF16)
    return hi, mid, lo


def _dot3_l(a01x3, x):
    return _dot(a01x3, jnp.concatenate(_split3(x), axis=0))


def _dot3_r(x, b01x3):
    return _dot(jnp.concatenate(_split3(x), axis=1), b01x3)


def _sigmoid(x):
    return 1.0 / (1.0 + jnp.exp(-x))


def _silu(x):
    return x * _sigmoid(x)


def _softplus(x):
    return jnp.maximum(x, 0.0) + jnp.log1p(jnp.exp(-jnp.abs(x)))


def _rms(x, w):
    ms = jnp.mean(x * x, axis=-1, keepdims=True)
    return x * lax.rsqrt(ms + NORM_EPS) * w


def _seq_masks(seq):
    t = np.arange(ROWS)
    same = (t[:, None] // seq) == (t[None, :] // seq)
    lower = same & (t[None, :] <= t[:, None])
    upper = same & (t[None, :] > t[:, None])
    return lower, upper


def _ssd_consts(seq):
    lower, upper = _seq_masks(seq)
    lu = np.concatenate([lower, upper], axis=0).astype(np.float32)
    rep = np.zeros((LANES, GROUP_W), np.float32)
    for j in range(HEADS_PER_GROUP):
        rep[j, j * SSM_HEAD_DIM:(j + 1) * SSM_HEAD_DIM] = 1.0
    return jnp.asarray(np.tile(lu, (1, 3)), BF16), jnp.asarray(np.tile(rep, (3, 1)), BF16)


def _level_widths(seq):
    widths, b = [], seq // 2
    while b >= 1:
        widths.append(b)
        b //= 2
    return widths


def _hgrn_consts(seq):
    lower, upper = _seq_masks(seq)
    t = np.arange(ROWS)
    masks = []
    for b in _level_widths(seq):
        is_q = (t // b) % 2 == 1
        pair = (t[:, None] // (2 * b)) == (t[None, :] // (2 * b))
        masks.append(pair & is_q[:, None] & (~is_q)[None, :])
    masks.append(np.eye(ROWS, dtype=bool))
    m_all = np.stack(masks).astype(np.float32)
    assert np.array_equal(m_all.sum(0) > 0, lower) and m_all.sum(0).max() == 1
    lu = np.concatenate([lower, upper], axis=0).astype(np.float32)
    return jnp.asarray(np.tile(lu, (1, 3)), BF16), jnp.asarray(m_all, F32)


def _group_specs(tm, n_prompt_blocks, width, axis=0):
    last = n_prompt_blocks - 1
    return [
        pl.BlockSpec((tm, width), lambda *ids: (jnp.minimum(ids[axis], last), 0)),
        pl.BlockSpec((tm, width), lambda *ids: (jnp.maximum(ids[axis] - n_prompt_blocks, 0), 0)),
    ]


def _group_rows(xp_ref, xs_ref, n_prompt_blocks, axis=0):
    return jnp.where(pl.program_id(axis) < n_prompt_blocks, xp_ref[...], xs_ref[...])


def _rmsnorm_body(xp_ref, xs_ref, w_ref, o_ref, *, n_prompt_blocks):
    o_ref[...] = _rms(_group_rows(xp_ref, xs_ref, n_prompt_blocks), w_ref[...]).astype(o_ref.dtype)


def _rmsnorm_cast(xp, xs, w):
    tp, d = xp.shape
    ts = xs.shape[0]
    tm = _pick(math.gcd(tp, ts), (512, 256, 128))
    return pl.pallas_call(
        functools.partial(_rmsnorm_body, n_prompt_blocks=tp // tm),
        grid=((tp + ts) // tm,),
        in_specs=_group_specs(tm, tp // tm, d) + [pl.BlockSpec((1, d), lambda i: (0, 0))],
        out_specs=pl.BlockSpec((tm, d), lambda i: (i, 0)),
        out_shape=jax.ShapeDtypeStruct((tp + ts, d), BF16),
        compiler_params=_params("parallel"),
        name="rmsnorm_cast",
    )(xp, xs, w)


IN_TN = 1024
IN_SHIFT = OFF_HQ - OFF_DT
N_SEG1 = OFF_DT // IN_TN
N_SEG2 = (COL_DT - OFF_DT) // IN_TN


def _in_proj_body(h_ref, wa_ref, wb_ref, wd_ref, o_ref, w_sc):
    j = pl.program_id(0)

    @pl.when(pl.program_id(1) == 0)
    def _():
        @pl.when(j < N_SEG1)
        def _():
            w_sc[...] = wa_ref[...].astype(BF16)

        @pl.when(jnp.logical_and(j >= N_SEG1, j < N_SEG1 + N_SEG2))
        def _():
            w_sc[0:IN_TN - IN_SHIFT] = wa_ref[IN_SHIFT:IN_TN].astype(BF16)
            w_sc[IN_TN - IN_SHIFT:IN_TN] = wb_ref[...].astype(BF16)

        @pl.when(j >= N_SEG1 + N_SEG2)
        def _():
            pad = jnp.zeros((LANES - IN_SHIFT, wd_ref.shape[1]), F32)
            w_sc[0:LANES] = jnp.concatenate([wd_ref[...], pad], axis=0).astype(BF16)

    @pl.when(j < N_SEG1 + N_SEG2)
    def _():
        o_ref[...] = _dot_nt(h_ref[...], w_sc[...])

    @pl.when(j >= N_SEG1 + N_SEG2)
    def _():
        o_ref[:, 0:LANES] = _dot_nt(h_ref[...], w_sc[0:LANES])
        o_ref[:, LANES:IN_TN] = jnp.zeros((o_ref.shape[0], IN_TN - LANES), F32)


def _in_proj(h, wt):
    t, d = h.shape
    assert OFF_DT % IN_TN == 0 and (COL_DT - OFF_DT) % IN_TN == 0 and PROJ_W - COL_DT == IN_TN
    assert IN_TN % IN_SHIFT == 0 and IN_SHIFT % 8 == 0 and IN_SHIFT == SSM_N_HEADS <= LANES
    tm = _pick(t, (1088, 512, 256, 128))
    last_a = N_SEG1 + N_SEG2 - 1
    per = IN_TN // IN_SHIFT
    return pl.pallas_call(
        _in_proj_body,
        grid=(PROJ_W // IN_TN, t // tm),
        in_specs=[
            pl.BlockSpec((tm, d), lambda j, i: (i, 0)),
            pl.BlockSpec((IN_TN, d), lambda j, i: (jnp.minimum(j, last_a), 0)),
            pl.BlockSpec((IN_SHIFT, d), lambda j, i: (jnp.minimum(j, last_a) * per + per, 0)),
            pl.BlockSpec((IN_SHIFT, d), lambda j, i: (OFF_DT // IN_SHIFT, 0)),
        ],
        out_specs=pl.BlockSpec((tm, IN_TN), lambda j, i: (i, j)),
        out_shape=jax.ShapeDtypeStruct((t, PROJ_W), F32),
        scratch_shapes=[pltpu.VMEM((IN_TN, d), BF16)],
        compiler_params=_params("arbitrary", "arbitrary"),
        name="in_proj",
    )(h, wt, wt, wt)


def _matmul_body(a_ref, b_ref, o_ref, *, act):
    acc = _dot(a_ref[...], b_ref[...])
    if act == "relu2":
        acc = jnp.square(jnp.maximum(acc, 0.0))
    o_ref[...] = acc.astype(o_ref.dtype)


def _matmul(a, b, *, out_dtype, act=None, name):
    m, k = a.shape
    n = b.shape[1]
    tm = _pick(m, (1088, 512, 256, 128))
    tn = _pick(n, (1024, 512))
    return pl.pallas_call(
        functools.partial(_matmul_body, act=act),
        grid=(m // tm, n // tn),
        in_specs=[pl.BlockSpec((tm, k), lambda i, j: (i, 0)), pl.BlockSpec((k, tn), lambda i, j: (0, j))],
        out_specs=pl.BlockSpec((tm, tn), lambda i, j: (i, j)),
        out_shape=jax.ShapeDtypeStruct((m, n), out_dtype),
        compiler_params=_params("parallel", "parallel"),
        name=name,
    )(a, b)


def _merge_body(yap_ref, yas_ref, wa_ref, ybp_ref, ybs_ref, wb_ref, ga_ref, gb_ref, o_ref, *, n_prompt_blocks):
    a = _dot(_group_rows(yap_ref, yas_ref, n_prompt_blocks, axis=1), wa_ref[...])
    b = _dot(_group_rows(ybp_ref, ybs_ref, n_prompt_blocks, axis=1), wb_ref[...])
    o_ref[...] = (_sigmoid(ga_ref[...]) * a + _sigmoid(gb_ref[...]) * b).astype(o_ref.dtype)


def _merge(ya_p, ya_s, wa, yb_p, yb_s, wb, proj):
    tp, ts = ya_p.shape[0], ya_s.shape[0]
    tm = _pick(math.gcd(tp, ts), (256, 128))
    tn = 1024
    npb = tp // tm
    return pl.pallas_call(
        functools.partial(_merge_body, n_prompt_blocks=npb),
        grid=(D_MODEL // tn, (tp + ts) // tm),
        in_specs=_group_specs(tm, npb, SSM_D_INNER, axis=1)
        + [pl.BlockSpec((SSM_D_INNER, tn), lambda j, i: (0, j))]
        + _group_specs(tm, npb, D_MODEL, axis=1)
        + [pl.BlockSpec((D_MODEL, tn), lambda j, i: (0, j)),
           pl.BlockSpec((tm, tn), lambda j, i: (i, COL_GA // tn + j)),
           pl.BlockSpec((tm, tn), lambda j, i: (i, COL_GB // tn + j))],
        out_specs=pl.BlockSpec((tm, tn), lambda j, i: (i, j)),
        out_shape=jax.ShapeDtypeStruct((tp + ts, D_MODEL), BF16),
        compiler_params=_params("parallel", "parallel"),
        name="merge_branches",
    )(ya_p, ya_s, wa, yb_p, yb_s, wb, proj, proj)


def _out_body(a_ref, w_ref, xp_ref, xs_ref, npost_ref, nnext_ref, x1_ref, h_ref, *, n_prompt_blocks):
    acc = _dot(a_ref[...], w_ref[...])
    x1 = _group_rows(xp_ref, xs_ref, n_prompt_blocks) + _rms(acc, npost_ref[...])
    x1_ref[...] = x1
    h_ref[...] = _rms(x1, nnext_ref[...]).astype(h_ref.dtype)


def _out_proj(a, w, xp, xs, npost, nnext):
    t = a.shape[0]
    tp, ts = xp.shape[0], xs.shape[0]
    tm = _pick(math.gcd(tp, ts), (256, 128))
    row = lambda i: (i, 0)
    fixed = lambda i: (0, 0)
    return pl.pallas_call(
        functools.partial(_out_body, n_prompt_blocks=tp // tm),
        grid=(t // tm,),
        in_specs=[pl.BlockSpec((tm, D_MODEL), row), pl.BlockSpec((D_MODEL, D_MODEL), fixed)]
        + _group_specs(tm, tp // tm, D_MODEL)
        + [pl.BlockSpec((1, D_MODEL), fixed), pl.BlockSpec((1, D_MODEL), fixed)],
        out_specs=[pl.BlockSpec((tm, D_MODEL), row), pl.BlockSpec((tm, D_MODEL), row)],
        out_shape=[jax.ShapeDtypeStruct((t, D_MODEL), F32), jax.ShapeDtypeStruct((t, D_MODEL), BF16)],
        compiler_params=_params("parallel"),
        name="out_proj_norm",
    )(a, w, xp, xs, npost, nnext)


def _down_body(u_ref, w_ref, x_ref, npost_ref, nnext_ref, x2_ref, h_ref, acc_ref):
    k = pl.program_id(1)

    @pl.when(k == 0)
    def _():
        acc_ref[...] = jnp.zeros_like(acc_ref)

    acc_ref[...] += _dot(u_ref[...], w_ref[...])

    @pl.when(k == pl.num_programs(1) - 1)
    def _():
        x2 = x_ref[...] + _rms(acc_ref[...], npost_ref[...])
        x2_ref[...] = x2
        h_ref[...] = _rms(x2, nnext_ref[...]).astype(h_ref.dtype)


def _ffn_down(u, w, x, npost, nnext):
    t, f = u.shape
    tm = _pick(t, (544, 512, 256, 128))
    tk = 2048
    row = lambda i, k: (i, 0)
    fixed = lambda i, k: (0, 0)
    return pl.pallas_call(
        _down_body,
        grid=(t // tm, f // tk),
        in_specs=[
            pl.BlockSpec((tm, tk), lambda i, k: (i, k)),
            pl.BlockSpec((tk, D_MODEL), lambda i, k: (k, 0)),
            pl.BlockSpec((tm, D_MODEL), row),
            pl.BlockSpec((1, D_MODEL), fixed),
            pl.BlockSpec((1, D_MODEL), fixed),
        ],
        out_specs=[pl.BlockSpec((tm, D_MODEL), row), pl.BlockSpec((tm, D_MODEL), row)],
        out_shape=[jax.ShapeDtypeStruct((t, D_MODEL), F32), jax.ShapeDtypeStruct((t, D_MODEL), BF16)],
        scratch_shapes=[pltpu.VMEM((tm, D_MODEL), F32)],
        compiler_params=_params("parallel", "arbitrary"),
        name="ffn_down_norm",
    )(u, w, x, npost, nnext)


def _ple_body(h_ref, wg_ref, pp_ref, ps_ref, wp_ref, x_ref, yp_ref, ys_ref, *, n_prompt_blocks):
    i = pl.program_id(0)
    g = _sigmoid(_dot(h_ref[...], wg_ref[...]))
    e = _dot(_group_rows(pp_ref, ps_ref, n_prompt_blocks).astype(BF16), wp_ref[...])
    y = x_ref[...] + g * e

    @pl.when(i < n_prompt_blocks)
    def _():
        yp_ref[...] = y

    @pl.when(i >= n_prompt_blocks)
    def _():
        ys_ref[...] = y


def _ple(h, wg, pp, ps, wp, x):
    tp, pd = pp.shape
    ts = ps.shape[0]
    tm = _pick(math.gcd(tp, ts), (256, 128))
    npb = tp // tm
    row = lambda i: (i, 0)
    fixed = lambda i: (0, 0)
    return pl.pallas_call(
        functools.partial(_ple_body, n_prompt_blocks=npb),
        grid=((tp + ts) // tm,),
        in_specs=[pl.BlockSpec((tm, D_MODEL), row), pl.BlockSpec((D_MODEL, D_MODEL), fixed)]
        + _group_specs(tm, npb, pd)
        + [pl.BlockSpec((pd, D_MODEL), fixed), pl.BlockSpec((tm, D_MODEL), row)],
        out_specs=_group_specs(tm, npb, D_MODEL),
        out_shape=[jax.ShapeDtypeStruct((tp, D_MODEL), F32), jax.ShapeDtypeStruct((ts, D_MODEL), F32)],
        compiler_params=_params("arbitrary"),
        name="ple_gate",
    )(h, wg, pp, ps, wp, x)


def _conv_taps(cur, shifted, cw, cb):
    acc = cw[SSM_CONV_WIDTH - 1:SSM_CONV_WIDTH, :] * cur
    for d in range(1, SSM_CONV_WIDTH):
        acc = acc + cw[SSM_CONV_WIDTH - 1 - d:SSM_CONV_WIDTH - d, :] * shifted[d]
    return _silu(acc + cb)


def _conv_prompt(cur_ref, halo_sc, cw_ref, cb_ref):
    cur = cur_ref[...]
    top = cur[0:8]
    halo = halo_sc[...]
    row8 = lax.broadcasted_iota(jnp.int32, top.shape, 0)
    shifted, shifted_top = {}, {}
    for d in range(1, SSM_CONV_WIDTH):
        shifted[d] = pltpu.roll(cur, d, 0)
        shifted_top[d] = jnp.where(row8 < d, pltpu.roll(halo, d, 0), pltpu.roll(top, d, 0))
    cw = cw_ref[...]
    cb = cb_ref[...]
    out = _conv_taps(cur, shifted, cw, cb)
    out_top = _conv_taps(top, shifted_top, cw, cb)
    halo_sc[...] = cur[ROWS - 8:ROWS]
    return jnp.concatenate([out_top, out[8:]], axis=0)


def _conv_sample(cur_ref, carry_ref, cw_ref, cb_ref, seq):
    cur = cur_ref[...]
    carry = carry_ref[...]
    pos = lax.broadcasted_iota(jnp.int32, cur.shape, 0) & (seq - 1)
    shifted = {}
    for d in range(1, SSM_CONV_WIDTH):
        shifted[d] = jnp.where(pos >= d, pltpu.roll(cur, d, 0), pltpu.roll(carry, ROWS - (seq - d), 0))
    return _conv_taps(cur, shifted, cw_ref[...], cb_ref[...])


def _ssd_blocks(xg, bg_bf, cg_bf, dt_raw, dtb, alog, lu_ref, rep_ref, y_sc):
    groups = range(len(xg))
    lu = lu_ref[...]
    mask = lu[0:ROWS, 0:ROWS].astype(F32) > 0.5
    dt = [_softplus(dt_raw[g] + dtb[g]) for g in groups]
    cr = [_dot3_l(lu, dt[g] * (-jnp.exp(alog[g]))) for g in groups]
    cum = [c[0:ROWS] for c in cr]
    col = [c * LOG2E for c in cum]
    row_t = [(col[g] - jnp.log2(dt[g])).T for g in groups]
    x_bf = [x.astype(BF16) for x in xg]
    cb = [_dot_nt(cg_bf[g], bg_bf[g]) for g in groups]
    for j in range(HEADS_PER_GROUP):
        hs = slice(j * SSM_HEAD_DIM, (j + 1) * SSM_HEAD_DIM)
        for g in groups:
            seg = col[g][:, j:j + 1] - row_t[g][j:j + 1, :]
            m = (cb[g] * jnp.where(mask, jnp.exp2(seg), 0.0)).astype(BF16)
            y_sc[:, g * GROUP_W + j * SSM_HEAD_DIM:g * GROUP_W + (j + 1) * SSM_HEAD_DIM] = _dot(m, x_bf[g][:, hs])
    rep = rep_ref[...]
    slabs = [_dot3_r(jnp.concatenate([jnp.exp2(col[g]), dt[g] * jnp.exp(cr[g][ROWS:2 * ROWS])], axis=0), rep)
             for g in groups]
    return cum, [sl[0:ROWS] for sl in slabs], [sl[ROWS:2 * ROWS] for sl in slabs]


def _group_dt(dt_all, group):
    return pltpu.roll(dt_all, (LANES - group * HEADS_PER_GROUP) % LANES, 1)


def _head_rows(e_row):
    return jnp.concatenate(
        [jnp.broadcast_to(e_row[0:1, j:j + 1], (SSM_HEAD_DIM, SSM_D_STATE)) for j in range(HEADS_PER_GROUP)], axis=0)


def _gated_group_norm(y, z, nw):
    return _rms(y * _silu(z), nw)


def _ssd_prompt_body(x_ref, b_ref, c_ref, z_ref, dt_ref, cwx_ref, cwb_ref, cwc_ref, cbx_ref, cbb_ref, cbc_ref,
                     dtb_ref, alog_ref, dsk_ref, nw_ref, lu_ref, rep_ref,
                     yn_ref, hout_ref, h_sc, px_sc, pb_sc, pc_sc, y_sc):
    c = pl.program_id(2)
    groups = range(SSD_GROUPS_PER_STEP)

    @pl.when(c == 0)
    def _():
        h_sc[...] = jnp.zeros_like(h_sc)
        px_sc[...] = jnp.zeros_like(px_sc)
        pb_sc[...] = jnp.zeros_like(pb_sc)
        pc_sc[...] = jnp.zeros_like(pc_sc)

    x_all = _conv_prompt(x_ref, px_sc, cwx_ref, cbx_ref)
    b_all = _conv_prompt(b_ref, pb_sc, cwb_ref, cbb_ref).astype(BF16)
    c_all = _conv_prompt(c_ref, pc_sc, cwc_ref, cbc_ref).astype(BF16)
    wide = [slice(g * GROUP_W, (g + 1) * GROUP_W) for g in groups]
    narrow = [slice(g * SSM_D_STATE, (g + 1) * SSM_D_STATE) for g in groups]
    xg = [x_all[:, wide[g]] for g in groups]
    bg = [b_all[:, narrow[g]] for g in groups]
    cg = [c_all[:, narrow[g]] for g in groups]
    cum, e_rep, w_rep = _ssd_blocks(
        xg, bg, cg, [_group_dt(dt_ref[...], pl.program_id(1) * SSD_GROUPS_PER_STEP + g) for g in groups],
        [dtb_ref[:, narrow[g]] for g in groups],
        [alog_ref[:, narrow[g]] for g in groups], lu_ref, rep_ref, y_sc)
    h = [h_sc[wide[g]] for g in groups]
    y_inter = [_dot_nt(cg[g], h[g].astype(BF16)) for g in groups]
    upd = [_dot_tn((xg[g] * w_rep[g]).astype(BF16), bg[g]) for g in groups]
    for g in groups:
        y = y_sc[:, wide[g]] + y_inter[g] * e_rep[g] + dsk_ref[:, wide[g]] * xg[g]
        yn_ref[:, wide[g]] = _gated_group_norm(y, z_ref[:, wide[g]], nw_ref[:, wide[g]]).astype(yn_ref.dtype)
        h_sc[wide[g]] = _head_rows(jnp.exp(cum[g][ROWS - 1:ROWS, :])) * h[g] + upd[g]

    @pl.when(c == pl.num_programs(2) - 1)
    def _():
        hout_ref[...] = h_sc[...].reshape(hout_ref.shape)


SLAB = 16


def _ssd_sample_body(x_ref, b_ref, c_ref, z_ref, dt_ref, hx_ref, hb_ref, hc_ref, st_ref,
                     cwx_ref, cwb_ref, cwc_ref, cbx_ref, cbb_ref, cbc_ref,
                     dtb_ref, alog_ref, dsk_ref, nw_ref, lu_ref, rep_ref,
                     yn_ref, hout_ref, y_sc, yi_sc, cum_sc, c_sc, *, seq):
    xg = _conv_sample(x_ref, hx_ref, cwx_ref, cbx_ref, seq)
    bg = _conv_sample(b_ref, hb_ref, cwb_ref, cbb_ref, seq)
    cg = _conv_sample(c_ref, hc_ref, cwc_ref, cbc_ref, seq)
    c_bf = cg.astype(BF16)
    cums, e_reps, w_reps = _ssd_blocks([xg], [bg.astype(BF16)], [c_bf], [_group_dt(dt_ref[...], pl.program_id(1))],
                                       [dtb_ref[...]], [alog_ref[...]],
                                       lu_ref, rep_ref, y_sc)
    cum_sc[...] = cums[0]
    c_sc[...] = c_bf
    yi_sc[...] = jnp.zeros_like(yi_sc)
    xw_t = (xg * w_reps[0]).T.astype(BF16)
    shift = seq.bit_length() - 1
    per_slab = SLAB // seq
    slab_seq = lax.shift_right_logical(lax.broadcasted_iota(jnp.int32, (SLAB, GROUP_W), 0), shift)
    seq_of_row_n = lax.shift_right_logical(lax.broadcasted_iota(jnp.int32, (ROWS, SSM_D_STATE), 0), shift)
    state_shape = (HEADS_PER_GROUP * SSM_HEAD_DIM, SSM_D_STATE)

    def per_sequence(s, carry):
        h0 = st_ref[s].reshape(state_shape)
        r0 = pl.multiple_of((s // per_slab) * SLAB, SLAB)
        y_inter = _dot_nt(c_sc[pl.ds(r0, SLAB), :], h0.astype(BF16))
        yi_sc[pl.ds(r0, SLAB), :] += jnp.where(slab_seq == s % per_slab, y_inter, 0.0)
        b_own = jnp.where(seq_of_row_n == s, bg, 0.0).astype(BF16)
        last = cum_sc[pl.ds(s * seq + seq - 1, 1), :]
        h_new = _head_rows(jnp.exp(last)) * h0 + _dot(xw_t, b_own)
        hout_ref[s] = h_new.reshape(hout_ref.shape[1:])
        return carry

    lax.fori_loop(0, ROWS // seq, per_sequence, 0, unroll=2)
    y = y_sc[...] + yi_sc[...] * e_reps[0] + dsk_ref[...] * xg
    yn_ref[...] = _gated_group_norm(y, z_ref[...], nw_ref[...]).astype(yn_ref.dtype)


def _ssd_param_specs(gidx, ng):
    wide, narrow = ng * GROUP_W, ng * SSM_D_STATE

    def col(block, off):
        return lambda *ids: (0, off // block + gidx(*ids))
    return [
        pl.BlockSpec((SSM_CONV_WIDTH, wide), col(wide, 0)),
        pl.BlockSpec((SSM_CONV_WIDTH, narrow), col(narrow, SSM_D_INNER)),
        pl.BlockSpec((SSM_CONV_WIDTH, narrow), col(narrow, SSM_D_INNER + SSM_BC)),
        pl.BlockSpec((1, wide), col(wide, 0)),
        pl.BlockSpec((1, narrow), col(narrow, SSM_D_INNER)),
        pl.BlockSpec((1, narrow), col(narrow, SSM_D_INNER + SSM_BC)),
        pl.BlockSpec((1, narrow), col(narrow, 0)),
        pl.BlockSpec((1, narrow), col(narrow, 0)),
        pl.BlockSpec((1, wide), col(wide, 0)),
        pl.BlockSpec((1, wide), col(wide, 0)),
        pl.BlockSpec((2 * ROWS, 3 * ROWS), lambda *ids: (0, 0)),
        pl.BlockSpec((3 * LANES, GROUP_W), lambda *ids: (0, 0)),
    ]


def _ssd_token_specs(ridx, gidx, ng):
    wide, narrow = ng * GROUP_W, ng * SSM_D_STATE

    def col(block, off):
        return lambda *ids: (ridx(*ids), off // block + gidx(*ids))
    return [
        pl.BlockSpec((ROWS, wide), col(wide, COL_X)),
        pl.BlockSpec((ROWS, narrow), col(narrow, COL_B)),
        pl.BlockSpec((ROWS, narrow), col(narrow, COL_C)),
        pl.BlockSpec((ROWS, wide), col(wide, COL_Z)),
        pl.BlockSpec((ROWS, LANES), lambda *ids: (ridx(*ids), COL_DT // LANES)),
    ]


def _ssd_prompt(proj, ssd_params, batch, length):
    nblk = length // ROWS
    ng = SSD_GROUPS_PER_STEP
    lu, rep = _ssd_consts(ROWS)
    ridx = lambda b, g, c: b * nblk + c
    gidx = lambda b, g, c: g
    conv_w, conv_b, dtb, alog, dsk, nw = ssd_params
    args = [proj] * 5 + [conv_w] * 3 + [conv_b] * 3 + [dtb, alog, dsk, nw, lu, rep]
    return pl.pallas_call(
        _ssd_prompt_body,
        grid=(batch, SSM_N_GROUPS // ng, nblk),
        in_specs=_ssd_token_specs(ridx, gidx, ng) + _ssd_param_specs(gidx, ng),
        out_specs=[
            pl.BlockSpec((ROWS, ng * GROUP_W), lambda b, g, c: (b * nblk + c, g)),
            pl.BlockSpec((1, ng * HEADS_PER_GROUP, SSM_HEAD_DIM, SSM_D_STATE), lambda b, g, c: (b, g, 0, 0)),
        ],
        out_shape=[
            jax.ShapeDtypeStruct((batch * length, SSM_D_INNER), BF16),
            jax.ShapeDtypeStruct((batch, SSM_N_HEADS, SSM_HEAD_DIM, SSM_D_STATE), F32),
        ],
        scratch_shapes=[
            pltpu.VMEM((ng * HEADS_PER_GROUP * SSM_HEAD_DIM, SSM_D_STATE), F32),
            pltpu.VMEM((8, ng * GROUP_W), F32),
            pltpu.VMEM((8, ng * SSM_D_STATE), F32),
            pltpu.VMEM((8, ng * SSM_D_STATE), F32),
            pltpu.VMEM((ROWS, ng * GROUP_W), F32),
        ],
        compiler_params=_params("parallel", "parallel", "arbitrary"),
        name="ssd_prompt",
    )(*args)


def _ssd_sample(proj, carry, state, ssd_params, row0, batch, seq):
    per_blk = ROWS // seq
    nblk = batch // per_blk
    blk0 = row0 // ROWS
    lu, rep = _ssd_consts(seq)
    ridx = lambda m, g: blk0 + m
    gidx = lambda m, g: g
    conv_w, conv_b, dtb, alog, dsk, nw = ssd_params

    def carry_spec(block, off):
        return pl.BlockSpec((ROWS, block), lambda m, g: (m, off // block + g))

    state_spec = pl.BlockSpec((per_blk, HEADS_PER_GROUP, SSM_HEAD_DIM, SSM_D_STATE), lambda m, g: (m, g, 0, 0))
    args = [proj] * 5 + [carry] * 3 + [state] + [conv_w] * 3 + [conv_b] * 3 + [dtb, alog, dsk, nw, lu, rep]
    return pl.pallas_call(
        functools.partial(_ssd_sample_body, seq=seq),
        grid=(nblk, SSM_N_GROUPS),
        in_specs=_ssd_token_specs(ridx, gidx, 1)
        + [carry_spec(GROUP_W, 0), carry_spec(SSM_D_STATE, SSM_D_INNER), carry_spec(SSM_D_STATE, SSM_D_INNER + SSM_BC),
           state_spec]
        + _ssd_param_specs(gidx, 1),
        out_specs=[pl.BlockSpec((ROWS, GROUP_W), lambda m, g: (m, g)), state_spec],
        out_shape=[jax.ShapeDtypeStruct((batch * seq, SSM_D_INNER), BF16), jax.ShapeDtypeStruct(state.shape, F32)],
        scratch_shapes=[
            pltpu.VMEM((ROWS, GROUP_W), F32),
            pltpu.VMEM((ROWS, GROUP_W), F32),
            pltpu.VMEM((ROWS, LANES), F32),
            pltpu.VMEM((ROWS, SSM_D_STATE), BF16),
        ],
        compiler_params=_params("parallel", "parallel"),
        name="ssd_sample",
    )(*args)


def _lower_bound(raw, layer):
    e = jnp.exp(raw - jnp.max(raw, axis=0, keepdims=True))
    return jnp.sum(e[0:layer + 1], axis=0, keepdims=True) / jnp.sum(e, axis=0, keepdims=True)


def _head_cols(ref, hh):
    return ref[:, hh * HG_KEY_DIM:(hh + 1) * HG_KEY_DIM]


def _level_ref(cum, b):
    width = cum.shape[1]
    if b >= 8:
        blocks = [jnp.broadcast_to(cum[p * 2 * b + b - 1:p * 2 * b + b, :], (2 * b, width))
                  for p in range(ROWS // (2 * b))]
        return blocks[0] if len(blocks) == 1 else jnp.concatenate(blocks, axis=0)
    if b == 1:
        odd = (lax.broadcasted_iota(jnp.int32, cum.shape, 0) & 1) == 1
        return jnp.where(odd, pltpu.roll(cum, 1, 0), cum)
    tiles = cum.reshape(ROWS // 8, 8, width)
    if b == 4:
        ref = jnp.broadcast_to(tiles[:, 3:4, :], tiles.shape)
    else:
        sub = lax.broadcasted_iota(jnp.int32, tiles.shape, 1)
        ref = jnp.where(sub < 4, jnp.broadcast_to(tiles[:, 1:2, :], tiles.shape),
                        jnp.broadcast_to(tiles[:, 5:6, :], tiles.shape))
    return ref.reshape(cum.shape)


def _hgrn_blocks(q_ref, f_ref, i_ref, lbraw_ref, lu_ref, m_ref, layer, seq):
    heads = range(HG_HEADS_PER_STEP)
    widths = _level_widths(seq)
    lu = lu_ref[...]
    q = [_head_cols(q_ref, hh) for hh in heads]
    v_bf = [_head_cols(i_ref, hh).astype(BF16) for hh in heads]
    k, log_f = [], []
    for hh in heads:
        hf = _head_cols(f_ref, hh)
        lb = _lower_bound(_head_cols(lbraw_ref, hh), layer)
        log_f.append(jnp.log(lb + (1.0 - lb) * _sigmoid(hf)))
        k.append((1.0 - lb) * _sigmoid(-hf))
    sums = [_dot3_l(lu, p) for p in log_f]
    cum = [s[0:ROWS] for s in sums]
    att = [m_ref[len(widths)] * _dot_nt(q[hh].astype(BF16), k[hh].astype(BF16)) for hh in heads]
    for lvl, b in enumerate(widths):
        for hh in heads:
            e = jnp.exp(-jnp.abs(cum[hh] - _level_ref(cum[hh], b)))
            att[hh] = att[hh] + m_ref[lvl] * _dot_nt((q[hh] * e).astype(BF16), (k[hh] * e).astype(BF16))
    o_intra = [_dot(att[hh].astype(BF16), v_bf[hh]) for hh in heads]
    q_dec = [(q[hh] * jnp.exp(cum[hh])).astype(BF16) for hh in heads]
    k_dec = [(k[hh] * jnp.exp(sums[hh][ROWS:2 * ROWS])).astype(BF16) for hh in heads]
    return o_intra, q_dec, k_dec, v_bf, cum


def _hgrn_gate_norm(o, g, nw):
    return (_rms(o, nw) * _silu(g)).astype(BF16)


def _hgrn_prompt_body(q_ref, f_ref, i_ref, g_ref, lbraw_ref, nw_ref, lu_ref, m_ref,
                      on_ref, sout_ref, s_sc, *, layer):
    c = pl.program_id(2)

    @pl.when(c == 0)
    def _():
        s_sc[...] = jnp.zeros_like(s_sc)

    heads = range(HG_HEADS_PER_STEP)
    o_intra, q_dec, k_dec, v_bf, cum = _hgrn_blocks(q_ref, f_ref, i_ref, lbraw_ref, lu_ref, m_ref, layer, ROWS)
    s = [s_sc[hh] for hh in heads]
    o = [o_intra[hh] + _dot(q_dec[hh], s[hh].astype(BF16)) for hh in heads]
    upd = [_dot_tn(k_dec[hh], v_bf[hh]) for hh in heads]
    for hh in heads:
        on_ref[:, hh * HG_VAL_DIM:(hh + 1) * HG_VAL_DIM] = _hgrn_gate_norm(o[hh], _head_cols(g_ref, hh), nw_ref[...])
        last_col = cum[hh].T[:, ROWS - 1:ROWS]
        s_sc[hh] = jnp.exp(last_col) * s[hh] + upd[hh]

    @pl.when(c == pl.num_programs(2) - 1)
    def _():
        sout_ref[...] = s_sc[...].reshape(sout_ref.shape)


def _hgrn_sample_body(q_ref, f_ref, i_ref, g_ref, st_ref, lbraw_ref, nw_ref, lu_ref, m_ref,
                      on_ref, sout_ref, o_sc, cumt_sc, qd_sc, kdt_sc, *, layer, seq):
    o_intra, q_dec, k_dec, _, cum = _hgrn_blocks(q_ref, f_ref, i_ref, lbraw_ref, lu_ref, m_ref, layer, seq)
    for hh in range(HG_HEADS_PER_STEP):
        o_sc[hh] = o_intra[hh]
        cumt_sc[hh] = cum[hh].T
        qd_sc[hh] = q_dec[hh]
        kdt_sc[hh] = k_dec[hh].astype(F32).T.astype(BF16)
    shift = seq.bit_length() - 1
    per_slab = SLAB // seq
    slab_seq = lax.shift_right_logical(lax.broadcasted_iota(jnp.int32, (SLAB, HG_VAL_DIM), 0), shift)
    seq_of_row = lax.shift_right_logical(lax.broadcasted_iota(jnp.int32, (ROWS, HG_VAL_DIM), 0), shift)
    lane = lax.broadcasted_iota(jnp.int32, (HG_KEY_DIM, ROWS), 1)

    def per_sequence(s, carry):
        own = seq_of_row == s
        is_last = lane == s * seq + seq - 1
        r0 = pl.multiple_of((s // per_slab) * SLAB, SLAB)
        heads = range(HG_HEADS_PER_STEP)
        s0 = [st_ref[s, hh] for hh in heads]
        o_inter = [_dot(qd_sc[hh, pl.ds(r0, SLAB), :], s0[hh].astype(BF16)) for hh in heads]
        upd = [_dot(kdt_sc[hh], jnp.where(own, _head_cols(i_ref, hh), 0.0).astype(BF16)) for hh in heads]
        for hh in heads:
            o_sc[hh, pl.ds(r0, SLAB), :] += jnp.where(slab_seq == s % per_slab, o_inter[hh], 0.0)
            last_col = jnp.sum(jnp.where(is_last, cumt_sc[hh], 0.0), axis=1, keepdims=True)
            sout_ref[s, hh] = jnp.exp(last_col) * s0[hh] + upd[hh]
        return carry

    lax.fori_loop(0, ROWS // seq, per_sequence, 0, unroll=2)
    for hh in range(HG_HEADS_PER_STEP):
        on_ref[:, hh * HG_VAL_DIM:(hh + 1) * HG_VAL_DIM] = _hgrn_gate_norm(o_sc[hh], _head_cols(g_ref, hh), nw_ref[...])


def _hgrn_token_specs(ridx, hidx):
    width = HG_HEADS_PER_STEP * HG_KEY_DIM

    def col(off):
        return lambda *ids: (ridx(*ids), off // width + hidx(*ids))
    return [pl.BlockSpec((ROWS, width), col(off)) for off in (COL_Q, COL_F, COL_I, COL_G)]


def _hgrn_param_specs(hidx, n_rows, n_masks):
    return [
        pl.BlockSpec((n_rows, HG_HEADS_PER_STEP * HG_KEY_DIM), lambda *ids: (0, hidx(*ids))),
        pl.BlockSpec((1, HG_VAL_DIM), lambda *ids: (0, 0)),
        pl.BlockSpec((2 * ROWS, 3 * ROWS), lambda *ids: (0, 0)),
        pl.BlockSpec((n_masks, ROWS, ROWS), lambda *ids: (0, 0, 0)),
    ]


def _hgrn_prompt(proj, hg_lb, hg_norm, layer, batch, length):
    nblk = length // ROWS
    hb = HG_HEADS_PER_STEP
    lu, m_all = _hgrn_consts(ROWS)
    ridx = lambda b, h, c: b * nblk + c
    hidx = lambda b, h, c: h
    return pl.pallas_call(
        functools.partial(_hgrn_prompt_body, layer=layer),
        grid=(batch, HG_N_HEADS // hb, nblk),
        in_specs=_hgrn_token_specs(ridx, hidx) + _hgrn_param_specs(hidx, hg_lb.shape[0], m_all.shape[0]),
        out_specs=[
            pl.BlockSpec((ROWS, hb * HG_VAL_DIM), lambda b, h, c: (b * nblk + c, h)),
            pl.BlockSpec((1, hb, HG_KEY_DIM, HG_VAL_DIM), lambda b, h, c: (b, h, 0, 0)),
        ],
        out_shape=[
            jax.ShapeDtypeStruct((batch * length, D_MODEL), BF16),
            jax.ShapeDtypeStruct((batch, HG_N_HEADS, HG_KEY_DIM, HG_VAL_DIM), F32),
        ],
        scratch_shapes=[pltpu.VMEM((hb, HG_KEY_DIM, HG_VAL_DIM), F32)],
        compiler_params=_params("parallel", "parallel", "arbitrary"),
        name="hgrn_prompt",
    )(*([proj] * 4 + [hg_lb, hg_norm, lu, m_all]))


def _hgrn_sample(proj, state, hg_lb, hg_norm, layer, row0, batch, seq):
    per_blk = ROWS // seq
    nblk = batch // per_blk
    blk0 = row0 // ROWS
    hb = HG_HEADS_PER_STEP
    lu, m_all = _hgrn_consts(seq)
    ridx = lambda m, h: blk0 + m
    hidx = lambda m, h: h
    state_spec = pl.BlockSpec((per_blk, hb, HG_KEY_DIM, HG_VAL_DIM), lambda m, h: (m, h, 0, 0))
    return pl.pallas_call(
        functools.partial(_hgrn_sample_body, layer=layer, seq=seq),
        grid=(nblk, HG_N_HEADS // hb),
        in_specs=_hgrn_token_specs(ridx, hidx) + [state_spec]
        + _hgrn_param_specs(hidx, hg_lb.shape[0], m_all.shape[0]),
        out_specs=[pl.BlockSpec((ROWS, hb * HG_VAL_DIM), lambda m, h: (m, h)), state_spec],
        out_shape=[jax.ShapeDtypeStruct((batch * seq, D_MODEL), BF16), jax.ShapeDtypeStruct(state.shape, F32)],
        scratch_shapes=[
            pltpu.VMEM((hb, ROWS, HG_VAL_DIM), F32),
            pltpu.VMEM((hb, HG_KEY_DIM, ROWS), F32),
            pltpu.VMEM((hb, ROWS, HG_KEY_DIM), BF16),
            pltpu.VMEM((hb, HG_KEY_DIM, ROWS), BF16),
        ],
        compiler_params=_params("parallel", "parallel"),
        name="hgrn_sample",
    )(*([proj] * 4 + [state, hg_lb, hg_norm, lu, m_all]))


def _head_slabs(v):
    v = v.reshape(SSM_N_GROUPS, HEADS_PER_GROUP).astype(F32)
    return jnp.pad(v, ((0, 0), (0, LANES - HEADS_PER_GROUP))).reshape(1, SSM_N_GROUPS * LANES)


def _sample_carry(conv0_t, seq):
    wm1 = conv0_t.shape[0]
    parts = [lax.pad(conv0_t[r], jnp.zeros((), conv0_t.dtype),
                     [(seq - wm1 + r, wm1 - 1 - r, seq - 1), (0, 0, 0)]) for r in range(wm1)]
    return functools.reduce(lambda a, b: a + b, parts)


def kernel(x_prompt, x_sample, p_prompt, p_sample, state_ssm, state_conv, state_hgrn, norm_mix_pre, w_in, conv_w, conv_b, dt_bias, a_log, d_skip, ssm_norm, w_br_a, hg_lb, hg_norm, w_br_b, w_out, norm_mix_post, norm_ffn_pre, w_up, w_down, norm_ffn_post, norm_ple, w_ple_gate, w_ple_proj):
    depth = w_in.shape[0]
    bp, lp, _ = x_prompt.shape
    bs, ls, _ = x_sample.shape
    tp, ts = bp * lp, bs * ls
    wm1 = SSM_CONV_WIDTH - 1
    assert lp % ROWS == 0 and ROWS % ls == 0 and ls & (ls - 1) == 0 and ts % ROWS == 0
    assert lp >= wm1 and ls >= wm1

    xp = x_prompt.reshape(tp, D_MODEL)
    xs = x_sample.reshape(ts, D_MODEL)
    row = lambda a: a.reshape(1, -1).astype(F32)
    outs = {k: [] for k in ("ssm_p", "conv_p", "hg_p", "ssm_s", "conv_s", "hg_s")}
    for li in range(depth):
        h = _rmsnorm_cast(xp, xs, row(norm_mix_pre[li]))
        proj = _in_proj(h, jnp.transpose(w_in[li]))

        tails = [proj[b * lp + lp - wm1:b * lp + lp, COL_X:COL_Q] for b in range(bp)]
        outs["conv_p"].append(jnp.stack(tails))
        tails_s = [proj[tp + ls - wm1 + r::ls, COL_X:COL_Q] for r in range(wm1)]
        outs["conv_s"].append(jnp.transpose(jnp.stack(tails_s), (1, 0, 2)))

        ssd_params = (conv_w[li], row(conv_b[li]), _head_slabs(dt_bias[li]), _head_slabs(a_log[li]),
                      row(jnp.repeat(d_skip[li], SSM_HEAD_DIM)), row(ssm_norm[li]))
        yn_p, ssm_p = _ssd_prompt(proj, ssd_params, bp, lp)
        yn_s, ssm_s = _ssd_sample(proj, _sample_carry(jnp.transpose(state_conv[li], (1, 0, 2)), ls), state_ssm[li], ssd_params, tp, bs, ls)
        on_p, hg_p = _hgrn_prompt(proj, hg_lb, row(hg_norm[li]), li, bp, lp)
        on_s, hg_s = _hgrn_sample(proj, state_hgrn[li], hg_lb, row(hg_norm[li]), li, tp, bs, ls)
        outs["ssm_p"].append(ssm_p)
        outs["ssm_s"].append(ssm_s)
        outs["hg_p"].append(hg_p)
        outs["hg_s"].append(hg_s)

        mixed = _merge(yn_p, yn_s, w_br_a[li].astype(BF16), on_p, on_s, w_br_b[li].astype(BF16), proj)
        x1, h2 = _out_proj(mixed, w_out[li].astype(BF16), xp, xs, row(norm_mix_post[li]), row(norm_ffn_pre[li]))
        u = _matmul(h2, w_up[li].astype(BF16), out_dtype=BF16, act="relu2", name="ffn_up")
        x2, h3 = _ffn_down(u, w_down[li].astype(BF16), x1, row(norm_ffn_post[li]), row(norm_ple[li]))
        xp, xs = _ple(h3, w_ple_gate[li].astype(BF16), p_prompt[li].reshape(tp, -1), p_sample[li].reshape(ts, -1),
                      w_ple_proj[li].astype(BF16), x2)

    stack = lambda k: jnp.stack(outs[k])
    return (xp.reshape(bp, lp, D_MODEL), xs.reshape(bs, ls, D_MODEL),
            stack("ssm_p"), stack("conv_p"), stack("hg_p"), stack("ssm_s"), stack("conv_s"), stack("hg_s"))
```

```python
import functools
import math

import numpy as np
import jax
import jax.numpy as jnp
from jax import lax
from jax.experimental import pallas as pl
from jax.experimental.pallas import tpu as pltpu

F32 = jnp.float32
BF16 = jnp.bfloat16

D_MODEL = 2048
SSM_D_INNER = 2 * D_MODEL
SSM_HEAD_DIM = 64
SSM_N_HEADS = SSM_D_INNER // SSM_HEAD_DIM
SSM_N_GROUPS = 8
SSM_D_STATE = 128
SSM_CONV_WIDTH = 4
SSM_BC = SSM_N_GROUPS * SSM_D_STATE
SSM_CONV_DIM = SSM_D_INNER + 2 * SSM_BC
HG_KEY_DIM = 128
HG_N_HEADS = D_MODEL // HG_KEY_DIM
HG_VAL_DIM = D_MODEL // HG_N_HEADS
FFN_HIDDEN = 4 * D_MODEL
NORM_EPS = 1e-6

OFF_XBC = SSM_D_INNER
OFF_DT = OFF_XBC + SSM_CONV_DIM
OFF_HQ = OFF_DT + SSM_N_HEADS

LANES = 128
ROWS = 128
GROUP_W = SSM_D_INNER // SSM_N_GROUPS
HEADS_PER_GROUP = SSM_N_HEADS // SSM_N_GROUPS
HG_HEADS_PER_STEP = 4
SSD_GROUPS_PER_STEP = 4
LOG2E = 1.4426950408889634

COL_Z = 0
COL_X = SSM_D_INNER
COL_B = COL_X + SSM_D_INNER
COL_C = COL_B + SSM_BC
COL_Q = COL_C + SSM_BC
COL_F = COL_Q + D_MODEL
COL_I = COL_F + D_MODEL
COL_G = COL_I + D_MODEL
COL_GA = COL_G + D_MODEL
COL_GB = COL_GA + D_MODEL
COL_DT = COL_GB + D_MODEL
PROJ_W = COL_DT + SSM_N_GROUPS * LANES

VMEM_LIMIT = 52 * 1024 * 1024


def _params(*sem):
    return pltpu.CompilerParams(dimension_semantics=sem, vmem_limit_bytes=VMEM_LIMIT)


def _pick(n, cands):
    for c in cands:
        if n % c == 0:
            return c
    raise ValueError(f"no tile for {n} in {cands}")


def _dot(a, b):
    return jnp.dot(a, b, preferred_element_type=F32)


def _dot_nt(a, b):
    return lax.dot_general(a, b, (((1,), (1,)), ((), ())), preferred_element_type=F32)


def _dot_tn(a, b):
    return lax.dot_general(a, b, (((0,), (0,)), ((), ())), preferred_element_type=F32)


def _split3(x):
    hi = x.astype(BF16)
    r1 = x - hi.astype(F32)
    mid = r1.astype(BF16)
    lo = (r1 - mid.astype(F32)).astype(BF16)
    return hi, mid, lo


def _dot3_l(a01x3, x):
    return _dot(a01x3, jnp.concatenate(_split3(x), axis=0))


def _dot3_r(x, b01x3):
    return _dot(jnp.concatenate(_split3(x), axis=1), b01x3)


def _sigmoid(x):
    return 1.0 / (1.0 + jnp.exp(-x))


def _silu(x):
    return x * _sigmoid(x)


def _softplus(x):
    return jnp.maximum(x, 0.0) + jnp.log1p(jnp.exp(-jnp.abs(x)))


def _rms(x, w):
    ms = jnp.mean(x * x, axis=-1, keepdims=True)
    return x * lax.rsqrt(ms + NORM_EPS) * w


def _seq_masks(seq):
    t = np.arange(ROWS)
    same = (t[:, None] // seq) == (t[None, :] // seq)
    lower = same & (t[None, :] <= t[:, None])
    upper = same & (t[None, :] > t[:, None])
    return lower, upper


def _ssd_consts(seq):
    lower, upper = _seq_masks(seq)
    lu = np.concatenate([lower, upper], axis=0).astype(np.float32)
    rep = np.zeros((LANES, GROUP_W), np.float32)
    for j in range(HEADS_PER_GROUP):
        rep[j, j * SSM_HEAD_DIM:(j + 1) * SSM_HEAD_DIM] = 1.0
    return jnp.asarray(np.tile(lu, (1, 3)), BF16), jnp.asarray(np.tile(rep, (3, 1)), BF16)


def _level_widths(seq):
    widths, b = [], seq // 2
    while b >= 1:
        widths.append(b)
        b //= 2
    return widths


def _hgrn_consts(seq):
    lower, upper = _seq_masks(seq)
    t = np.arange(ROWS)
    masks = []
    for b in _level_widths(seq):
        is_q = (t // b) % 2 == 1
        pair = (t[:, None] // (2 * b)) == (t[None, :] // (2 * b))
        masks.append(pair & is_q[:, None] & (~is_q)[None, :])
    masks.append(np.eye(ROWS, dtype=bool))
    m_all = np.stack(masks).astype(np.float32)
    assert np.array_equal(m_all.sum(0) > 0, lower) and m_all.sum(0).max() == 1
    lu = np.concatenate([lower, upper], axis=0).astype(np.float32)
    return jnp.asarray(np.tile(lu, (1, 3)), BF16), jnp.asarray(m_all, F32)


def _group_specs(tm, n_prompt_blocks, width, axis=0):
    last = n_prompt_blocks - 1
    return [
        pl.BlockSpec((tm, width), lambda *ids: (jnp.minimum(ids[axis], last), 0)),
        pl.BlockSpec((tm, width), lambda *ids: (jnp.maximum(ids[axis] - n_prompt_blocks, 0), 0)),
    ]


def _group_rows(xp_ref, xs_ref, n_prompt_blocks, axis=0):
    return jnp.where(pl.program_id(axis) < n_prompt_blocks, xp_ref[...], xs_ref[...])


def _rmsnorm_body(xp_ref, xs_ref, w_ref, o_ref, *, n_prompt_blocks):
    o_ref[...] = _rms(_group_rows(xp_ref, xs_ref, n_prompt_blocks), w_ref[...]).astype(o_ref.dtype)


def _rmsnorm_cast(xp, xs, w):
    tp, d = xp.shape
    ts = xs.shape[0]
    tm = _pick(math.gcd(tp, ts), (512, 256, 128))
    return pl.pallas_call(
        functools.partial(_rmsnorm_body, n_prompt_blocks=tp // tm),
        grid=((tp + ts) // tm,),
        in_specs=_group_specs(tm, tp // tm, d) + [pl.BlockSpec((1, d), lambda i: (0, 0))],
        out_specs=pl.BlockSpec((tm, d), lambda i: (i, 0)),
        out_shape=jax.ShapeDtypeStruct((tp + ts, d), BF16),
        compiler_params=_params("parallel"),
        name="rmsnorm_cast",
    )(xp, xs, w)


IN_TN = 1024
IN_SHIFT = OFF_HQ - OFF_DT
N_SEG1 = OFF_DT // IN_TN
N_SEG2 = (COL_DT - OFF_DT) // IN_TN


def _in_proj_body(h_ref, wa_ref, wb_ref, wd_ref, o_ref, w_sc):
    j = pl.program_id(0)

    @pl.when(pl.program_id(1) == 0)
    def _():
        @pl.when(j < N_SEG1)
        def _():
            w_sc[...] = wa_ref[...].astype(BF16)

        @pl.when(jnp.logical_and(j >= N_SEG1, j < N_SEG1 + N_SEG2))
        def _():
            w_sc[0:IN_TN - IN_SHIFT] = wa_ref[IN_SHIFT:IN_TN].astype(BF16)
            w_sc[IN_TN - IN_SHIFT:IN_TN] = wb_ref[...].astype(BF16)

        @pl.when(j >= N_SEG1 + N_SEG2)
        def _():
            pad = jnp.zeros((LANES - IN_SHIFT, wd_ref.shape[1]), F32)
            w_sc[0:LANES] = jnp.concatenate([wd_ref[...], pad], axis=0).astype(BF16)

    @pl.when(j < N_SEG1 + N_SEG2)
    def _():
        o_ref[...] = _dot_nt(h_ref[...], w_sc[...])

    @pl.when(j >= N_SEG1 + N_SEG2)
    def _():
        o_ref[:, 0:LANES] = _dot_nt(h_ref[...], w_sc[0:LANES])
        o_ref[:, LANES:IN_TN] = jnp.zeros((o_ref.shape[0], IN_TN - LANES), F32)


def _in_proj(h, wt):
    t, d = h.shape
    assert OFF_DT % IN_TN == 0 and (COL_DT - OFF_DT) % IN_TN == 0 and PROJ_W - COL_DT == IN_TN
    assert IN_TN % IN_SHIFT == 0 and IN_SHIFT % 8 == 0 and IN_SHIFT == SSM_N_HEADS <= LANES
    tm = _pick(t, (1088, 512, 256, 128))
    last_a = N_SEG1 + N_SEG2 - 1
    per = IN_TN // IN_SHIFT
    return pl.pallas_call(
        _in_proj_body,
        grid=(PROJ_W // IN_TN, t // tm),
        in_specs=[
            pl.BlockSpec((tm, d), lambda j, i: (i, 0)),
            pl.BlockSpec((IN_TN, d), lambda j, i: (jnp.minimum(j, last_a), 0)),
            pl.BlockSpec((IN_SHIFT, d), lambda j, i: (jnp.minimum(j, last_a) * per + per, 0)),
            pl.BlockSpec((IN_SHIFT, d), lambda j, i: (OFF_DT // IN_SHIFT, 0)),
        ],
        out_specs=pl.BlockSpec((tm, IN_TN), lambda j, i: (i, j)),
        out_shape=jax.ShapeDtypeStruct((t, PROJ_W), F32),
        scratch_shapes=[pltpu.VMEM((IN_TN, d), BF16)],
        compiler_params=_params("arbitrary", "arbitrary"),
        name="in_proj",
    )(h, wt, wt, wt)


def _matmul_body(a_ref, b_ref, o_ref, *, act):
    acc = _dot(a_ref[...], b_ref[...])
    if act == "relu2":
        acc = jnp.square(jnp.maximum(acc, 0.0))
    o_ref[...] = acc.astype(o_ref.dtype)


def _matmul(a, b, *, out_dtype, act=None, name):
    m, k = a.shape
    n = b.shape[1]
    tm = _pick(m, (1088, 512, 256, 128))
    tn = _pick(n, (1024, 512))
    return pl.pallas_call(
        functools.partial(_matmul_body, act=act),
        grid=(m // tm, n // tn),
        in_specs=[pl.BlockSpec((tm, k), lambda i, j: (i, 0)), pl.BlockSpec((k, tn), lambda i, j: (0, j))],
        out_specs=pl.BlockSpec((tm, tn), lambda i, j: (i, j)),
        out_shape=jax.ShapeDtypeStruct((m, n), out_dtype),
        compiler_params=_params("parallel", "parallel"),
        name=name,
    )(a, b)


def _merge_body(yap_ref, yas_ref, wa_ref, ybp_ref, ybs_ref, wb_ref, ga_ref, gb_ref, o_ref, *, n_prompt_blocks):
    a = _dot(_group_rows(yap_ref, yas_ref, n_prompt_blocks, axis=1), wa_ref[...])
    b = _dot(_group_rows(ybp_ref, ybs_ref, n_prompt_blocks, axis=1), wb_ref[...])
    o_ref[...] = (_sigmoid(ga_ref[...]) * a + _sigmoid(gb_ref[...]) * b).astype(o_ref.dtype)


def _merge(ya_p, ya_s, wa, yb_p, yb_s, wb, proj):
    tp, ts = ya_p.shape[0], ya_s.shape[0]
    tm = _pick(math.gcd(tp, ts), (256, 128))
    tn = 1024
    npb = tp // tm
    return pl.pallas_call(
        functools.partial(_merge_body, n_prompt_blocks=npb),
        grid=(D_MODEL // tn, (tp + ts) // tm),
        in_specs=_group_specs(tm, npb, SSM_D_INNER, axis=1)
        + [pl.BlockSpec((SSM_D_INNER, tn), lambda j, i: (0, j))]
        + _group_specs(tm, npb, D_MODEL, axis=1)
        + [pl.BlockSpec((D_MODEL, tn), lambda j, i: (0, j)),
           pl.BlockSpec((tm, tn), lambda j, i: (i, COL_GA // tn + j)),
           pl.BlockSpec((tm, tn), lambda j, i: (i, COL_GB // tn + j))],
        out_specs=pl.BlockSpec((tm, tn), lambda j, i: (i, j)),
        out_shape=jax.ShapeDtypeStruct((tp + ts, D_MODEL), BF16),
        compiler_params=_params("parallel", "parallel"),
        name="merge_branches",
    )(ya_p, ya_s, wa, yb_p, yb_s, wb, proj, proj)


def _out_body(a_ref, w_ref, xp_ref, xs_ref, npost_ref, nnext_ref, x1_ref, h_ref, *, n_prompt_blocks):
    acc = _dot(a_ref[...], w_ref[...])
    x1 = _group_rows(xp_ref, xs_ref, n_prompt_blocks) + _rms(acc, npost_ref[...])
    x1_ref[...] = x1
    h_ref[...] = _rms(x1, nnext_ref[...]).astype(h_ref.dtype)


def _out_proj(a, w, xp, xs, npost, nnext):
    t = a.shape[0]
    tp, ts = xp.shape[0], xs.shape[0]
    tm = _pick(math.gcd(tp, ts), (256, 128))
    row = lambda i: (i, 0)
    fixed = lambda i: (0, 0)
    return pl.pallas_call(
        functools.partial(_out_body, n_prompt_blocks=tp // tm),
        grid=(t // tm,),
        in_specs=[pl.BlockSpec((tm, D_MODEL), row), pl.BlockSpec((D_MODEL, D_MODEL), fixed)]
        + _group_specs(tm, tp // tm, D_MODEL)
        + [pl.BlockSpec((1, D_MODEL), fixed), pl.BlockSpec((1, D_MODEL), fixed)],
        out_specs=[pl.BlockSpec((tm, D_MODEL), row), pl.BlockSpec((tm, D_MODEL), row)],
        out_shape=[jax.ShapeDtypeStruct((t, D_MODEL), F32), jax.ShapeDtypeStruct((t, D_MODEL), BF16)],
        compiler_params=_params("parallel"),
        name="out_proj_norm",
    )(a, w, xp, xs, npost, nnext)


DOWN_SUB = 2


def _down_body(u_ref, w_ref, x_ref, npost_ref, nnext_ref, x2_ref, h_ref, acc_ref):
    k = pl.program_id(1)
    sub = pl.program_id(2)

    @pl.when(k == 0)
    def _():
        acc_ref[sub] = jnp.zeros(acc_ref.shape[1:], F32)

    acc_ref[sub] += _dot(u_ref[...], w_ref[...])

    @pl.when(k == pl.num_programs(1) - 1)
    def _():
        x2 = x_ref[...] + _rms(acc_ref[sub], npost_ref[...])
        x2_ref[...] = x2
        h_ref[...] = _rms(x2, nnext_ref[...]).astype(h_ref.dtype)


def _ffn_down(u, w, x, npost, nnext):
    t, f = u.shape
    tm = _pick(t, (544, 512, 256, 128))
    tk = 1024
    nk = f // tk
    nsub = DOWN_SUB if (t // tm) % DOWN_SUB == 0 else 1
    row_u = lambda p, k, s: (p * nsub + s, k)
    parked = lambda p, k, s: (p * nsub + jnp.where(k == nk - 1, s, 0), 0)
    fixed = lambda p, k, s: (0, 0)
    return pl.pallas_call(
        _down_body,
        grid=(t // (nsub * tm), nk, nsub),
        in_specs=[
            pl.BlockSpec((tm, tk), row_u),
            pl.BlockSpec((tk, D_MODEL), lambda p, k, s: (k, 0)),
            pl.BlockSpec((tm, D_MODEL), parked),
            pl.BlockSpec((1, D_MODEL), fixed),
            pl.BlockSpec((1, D_MODEL), fixed),
        ],
        out_specs=[pl.BlockSpec((tm, D_MODEL), parked), pl.BlockSpec((tm, D_MODEL), parked)],
        out_shape=[jax.ShapeDtypeStruct((t, D_MODEL), F32), jax.ShapeDtypeStruct((t, D_MODEL), BF16)],
        scratch_shapes=[pltpu.VMEM((nsub, tm, D_MODEL), F32)],
        compiler_params=_params("arbitrary", "arbitrary", "arbitrary"),
        name="ffn_down_norm",
    )(u, w, x, npost, nnext)


def _ple_body(h_ref, wg_ref, pp_ref, ps_ref, wp_ref, x_ref, yp_ref, ys_ref, *, n_prompt_blocks):
    i = pl.program_id(0)
    g = _sigmoid(_dot(h_ref[...], wg_ref[...]))
    e = _dot(_group_rows(pp_ref, ps_ref, n_prompt_blocks).astype(BF16), wp_ref[...])
    y = x_ref[...] + g * e

    @pl.when(i < n_prompt_blocks)
    def _():
        yp_ref[...] = y

    @pl.when(i >= n_prompt_blocks)
    def _():
        ys_ref[...] = y


def _ple(h, wg, pp, ps, wp, x):
    tp, pd = pp.shape
    ts = ps.shape[0]
    tm = _pick(math.gcd(tp, ts), (256, 128))
    npb = tp // tm
    row = lambda i: (i, 0)
    fixed = lambda i: (0, 0)
    return pl.pallas_call(
        functools.partial(_ple_body, n_prompt_blocks=npb),
        grid=((tp + ts) // tm,),
        in_specs=[pl.BlockSpec((tm, D_MODEL), row), pl.BlockSpec((D_MODEL, D_MODEL), fixed)]
        + _group_specs(tm, npb, pd)
        + [pl.BlockSpec((pd, D_MODEL), fixed), pl.BlockSpec((tm, D_MODEL), row)],
        out_specs=_group_specs(tm, npb, D_MODEL),
        out_shape=[jax.ShapeDtypeStruct((tp, D_MODEL), F32), jax.ShapeDtypeStruct((ts, D_MODEL), F32)],
        compiler_params=_params("arbitrary"),
        name="ple_gate",
    )(h, wg, pp, ps, wp, x)


def _conv_taps(cur, shifted, cw, cb):
    acc = cw[SSM_CONV_WIDTH - 1:SSM_CONV_WIDTH, :] * cur
    for d in range(1, SSM_CONV_WIDTH):
        acc = acc + cw[SSM_CONV_WIDTH - 1 - d:SSM_CONV_WIDTH - d, :] * shifted[d]
    return _silu(acc + cb)


def _conv_prompt(cur_ref, halo_sc, cw_ref, cb_ref):
    cur = cur_ref[...]
    top = cur[0:8]
    halo = halo_sc[...]
    row8 = lax.broadcasted_iota(jnp.int32, top.shape, 0)
    shifted, shifted_top = {}, {}
    for d in range(1, SSM_CONV_WIDTH):
        shifted[d] = pltpu.roll(cur, d, 0)
        shifted_top[d] = jnp.where(row8 < d, pltpu.roll(halo, d, 0), pltpu.roll(top, d, 0))
    cw = cw_ref[...]
    cb = cb_ref[...]
    out = _conv_taps(cur, shifted, cw, cb)
    out_top = _conv_taps(top, shifted_top, cw, cb)
    halo_sc[...] = cur[ROWS - 8:ROWS]
    return jnp.concatenate([out_top, out[8:]], axis=0)


def _conv_sample(cur_ref, carry_ref, cw_ref, cb_ref, seq):
    cur = cur_ref[...]
    carry = carry_ref[...]
    pos = lax.broadcasted_iota(jnp.int32, cur.shape, 0) & (seq - 1)
    shifted = {}
    for d in range(1, SSM_CONV_WIDTH):
        shifted[d] = jnp.where(pos >= d, pltpu.roll(cur, d, 0), pltpu.roll(carry, ROWS - (seq - d), 0))
    return _conv_taps(cur, shifted, cw_ref[...], cb_ref[...])


def _ssd_blocks(xg, bg_bf, cg_bf, dt_raw, dtb, alog, lu_ref, rep_ref, y_sc):
    groups = range(len(xg))
    lu = lu_ref[...]
    mask = lu[0:ROWS, 0:ROWS].astype(F32) > 0.5
    dt = [_softplus(dt_raw[g] + dtb[g]) for g in groups]
    cr = [_dot3_l(lu, dt[g] * (-jnp.exp(alog[g]))) for g in groups]
    cum = [c[0:ROWS] for c in cr]
    col = [c * LOG2E for c in cum]
    row_t = [(col[g] - jnp.log2(dt[g])).T for g in groups]
    x_bf = [x.astype(BF16) for x in xg]
    cb = [_dot_nt(cg_bf[g], bg_bf[g]) for g in groups]
    for j in range(HEADS_PER_GROUP):
        hs = slice(j * SSM_HEAD_DIM, (j + 1) * SSM_HEAD_DIM)
        for g in groups:
            seg = col[g][:, j:j + 1] - row_t[g][j:j + 1, :]
            m = (cb[g] * jnp.where(mask, jnp.exp2(seg), 0.0)).astype(BF16)
            y_sc[:, g * GROUP_W + j * SSM_HEAD_DIM:g * GROUP_W + (j + 1) * SSM_HEAD_DIM] = _dot(m, x_bf[g][:, hs])
    rep = rep_ref[...]
    slabs = [_dot3_r(jnp.concatenate([jnp.exp2(col[g]), dt[g] * jnp.exp(cr[g][ROWS:2 * ROWS])], axis=0), rep)
             for g in groups]
    return cum, [sl[0:ROWS] for sl in slabs], [sl[ROWS:2 * ROWS] for sl in slabs]


def _group_dt(dt_all, group):
    return pltpu.roll(dt_all, (LANES - group * HEADS_PER_GROUP) % LANES, 1)


def _head_rows(e_row):
    return jnp.concatenate(
        [jnp.broadcast_to(e_row[0:1, j:j + 1], (SSM_HEAD_DIM, SSM_D_STATE)) for j in range(HEADS_PER_GROUP)], axis=0)


def _gated_group_norm(y, z, nw):
    return _rms(y * _silu(z), nw)


def _ssd_prompt_body(x_ref, b_ref, c_ref, z_ref, dt_ref, cwx_ref, cwb_ref, cwc_ref, cbx_ref, cbb_ref, cbc_ref,
                     dtb_ref, alog_ref, dsk_ref, nw_ref, lu_ref, rep_ref,
                     yn_ref, hout_ref, h_sc, px_sc, pb_sc, pc_sc, y_sc):
    c = pl.program_id(2)
    groups = range(SSD_GROUPS_PER_STEP)

    @pl.when(c == 0)
    def _():
        h_sc[...] = jnp.zeros_like(h_sc)
        px_sc[...] = jnp.zeros_like(px_sc)
        pb_sc[...] = jnp.zeros_like(pb_sc)
        pc_sc[...] = jnp.zeros_like(pc_sc)

    x_all = _conv_prompt(x_ref, px_sc, cwx_ref, cbx_ref)
    b_all = _conv_prompt(b_ref, pb_sc, cwb_ref, cbb_ref).astype(BF16)
    c_all = _conv_prompt(c_ref, pc_sc, cwc_ref, cbc_ref).astype(BF16)
    wide = [slice(g * GROUP_W, (g + 1) * GROUP_W) for g in groups]
    narrow = [slice(g * SSM_D_STATE, (g + 1) * SSM_D_STATE) for g in groups]
    xg = [x_all[:, wide[g]] for g in groups]
    bg = [b_all[:, narrow[g]] for g in groups]
    cg = [c_all[:, narrow[g]] for g in groups]
    cum, e_rep, w_rep = _ssd_blocks(
        xg, bg, cg, [_group_dt(dt_ref[...], pl.program_id(1) * SSD_GROUPS_PER_STEP + g) for g in groups],
        [dtb_ref[:, narrow[g]] for g in groups],
        [alog_ref[:, narrow[g]] for g in groups], lu_ref, rep_ref, y_sc)
    h = [h_sc[wide[g]] for g in groups]
    y_inter = [_dot_nt(cg[g], h[g].astype(BF16)) for g in groups]
    upd = [_dot_tn((xg[g] * w_rep[g]).astype(BF16), bg[g]) for g in groups]
    for g in groups:
        y = y_sc[:, wide[g]] + y_inter[g] * e_rep[g] + dsk_ref[:, wide[g]] * xg[g]
        yn_ref[:, wide[g]] = _gated_group_norm(y, z_ref[:, wide[g]], nw_ref[:, wide[g]]).astype(yn_ref.dtype)
        h_sc[wide[g]] = _head_rows(jnp.exp(cum[g][ROWS - 1:ROWS, :])) * h[g] + upd[g]

    @pl.when(c == pl.num_programs(2) - 1)
    def _():
        hout_ref[...] = h_sc[...].reshape(hout_ref.shape)


SLAB = 16


def _ssd_sample_body(x_ref, b_ref, c_ref, z_ref, dt_ref, hx_ref, hb_ref, hc_ref, st_ref,
                     cwx_ref, cwb_ref, cwc_ref, cbx_ref, cbb_ref, cbc_ref,
                     dtb_ref, alog_ref, dsk_ref, nw_ref, lu_ref, rep_ref,
                     yn_ref, hout_ref, y_sc, yi_sc, cum_sc, c_sc, *, seq):
    xg = _conv_sample(x_ref, hx_ref, cwx_ref, cbx_ref, seq)
    bg = _conv_sample(b_ref, hb_ref, cwb_ref, cbb_ref, seq)
    cg = _conv_sample(c_ref, hc_ref, cwc_ref, cbc_ref, seq)
    c_bf = cg.astype(BF16)
    cums, e_reps, w_reps = _ssd_blocks([xg], [bg.astype(BF16)], [c_bf], [_group_dt(dt_ref[...], pl.program_id(1))],
                                       [dtb_ref[...]], [alog_ref[...]],
                                       lu_ref, rep_ref, y_sc)
    cum_sc[...] = cums[0]
    c_sc[...] = c_bf
    yi_sc[...] = jnp.zeros_like(yi_sc)
    xw_t = (xg * w_reps[0]).T.astype(BF16)
    shift = seq.bit_length() - 1
    per_slab = SLAB // seq
    slab_seq = lax.shift_right_logical(lax.broadcasted_iota(jnp.int32, (SLAB, GROUP_W), 0), shift)
    seq_of_row_n = lax.shift_right_logical(lax.broadcasted_iota(jnp.int32, (ROWS, SSM_D_STATE), 0), shift)
    state_shape = (HEADS_PER_GROUP * SSM_HEAD_DIM, SSM_D_STATE)

    def per_sequence(s, carry):
        h0 = st_ref[s].reshape(state_shape)
        r0 = pl.multiple_of((s // per_slab) * SLAB, SLAB)
        y_inter = _dot_nt(c_sc[pl.ds(r0, SLAB), :], h0.astype(BF16))
        yi_sc[pl.ds(r0, SLAB), :] += jnp.where(slab_seq == s % per_slab, y_inter, 0.0)
        b_own = jnp.where(seq_of_row_n == s, bg, 0.0).astype(BF16)
        last = cum_sc[pl.ds(s * seq + seq - 1, 1), :]
        h_new = _head_rows(jnp.exp(last)) * h0 + _dot(xw_t, b_own)
        hout_ref[s] = h_new.reshape(hout_ref.shape[1:])
        return carry

    lax.fori_loop(0, ROWS // seq, per_sequence, 0, unroll=2)
    y = y_sc[...] + yi_sc[...] * e_reps[0] + dsk_ref[...] * xg
    yn_ref[...] = _gated_group_norm(y, z_ref[...], nw_ref[...]).astype(yn_ref.dtype)


def _ssd_param_specs(gidx, ng):
    wide, narrow = ng * GROUP_W, ng * SSM_D_STATE

    def col(block, off):
        return lambda *ids: (0, off // block + gidx(*ids))
    return [
        pl.BlockSpec((SSM_CONV_WIDTH, wide), col(wide, 0)),
        pl.BlockSpec((SSM_CONV_WIDTH, narrow), col(narrow, SSM_D_INNER)),
        pl.BlockSpec((SSM_CONV_WIDTH, narrow), col(narrow, SSM_D_INNER + SSM_BC)),
        pl.BlockSpec((1, wide), col(wide, 0)),
        pl.BlockSpec((1, narrow), col(narrow, SSM_D_INNER)),
        pl.BlockSpec((1, narrow), col(narrow, SSM_D_INNER + SSM_BC)),
        pl.BlockSpec((1, narrow), col(narrow, 0)),
        pl.BlockSpec((1, narrow), col(narrow, 0)),
        pl.BlockSpec((1, wide), col(wide, 0)),
        pl.BlockSpec((1, wide), col(wide, 0)),
        pl.BlockSpec((2 * ROWS, 3 * ROWS), lambda *ids: (0, 0)),
        pl.BlockSpec((3 * LANES, GROUP_W), lambda *ids: (0, 0)),
    ]


def _ssd_token_specs(ridx, gidx, ng):
    wide, narrow = ng * GROUP_W, ng * SSM_D_STATE

    def col(block, off):
        return lambda *ids: (ridx(*ids), off // block + gidx(*ids))
    return [
        pl.BlockSpec((ROWS, wide), col(wide, COL_X)),
        pl.BlockSpec((ROWS, narrow), col(narrow, COL_B)),
        pl.BlockSpec((ROWS, narrow), col(narrow, COL_C)),
        pl.BlockSpec((ROWS, wide), col(wide, COL_Z)),
        pl.BlockSpec((ROWS, LANES), lambda *ids: (ridx(*ids), COL_DT // LANES)),
    ]


def _ssd_prompt(proj, ssd_params, batch, length):
    nblk = length // ROWS
    ng = SSD_GROUPS_PER_STEP
    lu, rep = _ssd_consts(ROWS)
    ridx = lambda b, g, c: b * nblk + c
    gidx = lambda b, g, c: g
    conv_w, conv_b, dtb, alog, dsk, nw = ssd_params
    args = [proj] * 5 + [conv_w] * 3 + [conv_b] * 3 + [dtb, alog, dsk, nw, lu, rep]
    return pl.pallas_call(
        _ssd_prompt_body,
        grid=(batch, SSM_N_GROUPS // ng, nblk),
        in_specs=_ssd_token_specs(ridx, gidx, ng) + _ssd_param_specs(gidx, ng),
        out_specs=[
            pl.BlockSpec((ROWS, ng * GROUP_W), lambda b, g, c: (b * nblk + c, g)),
            pl.BlockSpec((1, ng * HEADS_PER_GROUP, SSM_HEAD_DIM, SSM_D_STATE), lambda b, g, c: (b, g, 0, 0)),
        ],
        out_shape=[
            jax.ShapeDtypeStruct((batch * length, SSM_D_INNER), BF16),
            jax.ShapeDtypeStruct((batch, SSM_N_HEADS, SSM_HEAD_DIM, SSM_D_STATE), F32),
        ],
        scratch_shapes=[
            pltpu.VMEM((ng * HEADS_PER_GROUP * SSM_HEAD_DIM, SSM_D_STATE), F32),
            pltpu.VMEM((8, ng * GROUP_W), F32),
            pltpu.VMEM((8, ng * SSM_D_STATE), F32),
            pltpu.VMEM((8, ng * SSM_D_STATE), F32),
            pltpu.VMEM((ROWS, ng * GROUP_W), F32),
        ],
        compiler_params=_params("parallel", "parallel", "arbitrary"),
        name="ssd_prompt",
    )(*args)


def _ssd_sample(proj, carry, state, ssd_params, row0, batch, seq):
    per_blk = ROWS // seq
    nblk = batch // per_blk
    blk0 = row0 // ROWS
    lu, rep = _ssd_consts(seq)
    ridx = lambda m, g: blk0 + m
    gidx = lambda m, g: g
    conv_w, conv_b, dtb, alog, dsk, nw = ssd_params

    def carry_spec(block, off):
        return pl.BlockSpec((ROWS, block), lambda m, g: (m, off // block + g))

    state_spec = pl.BlockSpec((per_blk, HEADS_PER_GROUP, SSM_HEAD_DIM, SSM_D_STATE), lambda m, g: (m, g, 0, 0))
    args = [proj] * 5 + [carry] * 3 + [state] + [conv_w] * 3 + [conv_b] * 3 + [dtb, alog, dsk, nw, lu, rep]
    return pl.pallas_call(
        functools.partial(_ssd_sample_body, seq=seq),
        grid=(nblk, SSM_N_GROUPS),
        in_specs=_ssd_token_specs(ridx, gidx, 1)
        + [carry_spec(GROUP_W, 0), carry_spec(SSM_D_STATE, SSM_D_INNER), carry_spec(SSM_D_STATE, SSM_D_INNER + SSM_BC),
           state_spec]
        + _ssd_param_specs(gidx, 1),
        out_specs=[pl.BlockSpec((ROWS, GROUP_W), lambda m, g: (m, g)), state_spec],
        out_shape=[jax.ShapeDtypeStruct((batch * seq, SSM_D_INNER), BF16), jax.ShapeDtypeStruct(state.shape, F32)],
        scratch_shapes=[
            pltpu.VMEM((ROWS, GROUP_W), F32),
            pltpu.VMEM((ROWS, GROUP_W), F32),
            pltpu.VMEM((ROWS, LANES), F32),
            pltpu.VMEM((ROWS, SSM_D_STATE), BF16),
        ],
        compiler_params=_params("parallel", "parallel"),
        name="ssd_sample",
    )(*args)


def _lower_bound(raw, layer):
    e = jnp.exp(raw - jnp.max(raw, axis=0, keepdims=True))
    return jnp.sum(e[0:layer + 1], axis=0, keepdims=True) / jnp.sum(e, axis=0, keepdims=True)


def _head_cols(ref, hh):
    return ref[:, hh * HG_KEY_DIM:(hh + 1) * HG_KEY_DIM]


def _level_ref(cum, b):
    width = cum.shape[1]
    if b >= 8:
        blocks = [jnp.broadcast_to(cum[p * 2 * b + b - 1:p * 2 * b + b, :], (2 * b, width))
                  for p in range(ROWS // (2 * b))]
        return blocks[0] if len(blocks) == 1 else jnp.concatenate(blocks, axis=0)
    if b == 1:
        odd = (lax.broadcasted_iota(jnp.int32, cum.shape, 0) & 1) == 1
        return jnp.where(odd, pltpu.roll(cum, 1, 0), cum)
    tiles = cum.reshape(ROWS // 8, 8, width)
    if b == 4:
        ref = jnp.broadcast_to(tiles[:, 3:4, :], tiles.shape)
    else:
        sub = lax.broadcasted_iota(jnp.int32, tiles.shape, 1)
        ref = jnp.where(sub < 4, jnp.broadcast_to(tiles[:, 1:2, :], tiles.shape),
                        jnp.broadcast_to(tiles[:, 5:6, :], tiles.shape))
    return ref.reshape(cum.shape)


def _hgrn_blocks(q_ref, f_ref, i_ref, lbraw_ref, lu_ref, m_ref, layer, seq):
    heads = range(HG_HEADS_PER_STEP)
    widths = _level_widths(seq)
    lu = lu_ref[...]
    q = [_head_cols(q_ref, hh) for hh in heads]
    v_bf = [_head_cols(i_ref, hh).astype(BF16) for hh in heads]
    k, log_f = [], []
    for hh in heads:
        hf = _head_cols(f_ref, hh)
        lb = _lower_bound(_head_cols(lbraw_ref, hh), layer)
        log_f.append(jnp.log(lb + (1.0 - lb) * _sigmoid(hf)))
        k.append((1.0 - lb) * _sigmoid(-hf))
    sums = [_dot3_l(lu, p) for p in log_f]
    cum = [s[0:ROWS] for s in sums]
    att = [m_ref[len(widths)] * _dot_nt(q[hh].astype(BF16), k[hh].astype(BF16)) for hh in heads]
    for lvl, b in enumerate(widths):
        for hh in heads:
            e = jnp.exp(-jnp.abs(cum[hh] - _level_ref(cum[hh], b)))
            att[hh] = att[hh] + m_ref[lvl] * _dot_nt((q[hh] * e).astype(BF16), (k[hh] * e).astype(BF16))
    o_intra = [_dot(att[hh].astype(BF16), v_bf[hh]) for hh in heads]
    q_dec = [(q[hh] * jnp.exp(cum[hh])).astype(BF16) for hh in heads]
    k_dec = [(k[hh] * jnp.exp(sums[hh][ROWS:2 * ROWS])).astype(BF16) for hh in heads]
    return o_intra, q_dec, k_dec, v_bf, cum


def _hgrn_gate_norm(o, g, nw):
    return (_rms(o, nw) * _silu(g)).astype(BF16)


def _hgrn_prompt_body(q_ref, f_ref, i_ref, g_ref, lbraw_ref, nw_ref, lu_ref, m_ref,
                      on_ref, sout_ref, s_sc, *, layer):
    c = pl.program_id(2)

    @pl.when(c == 0)
    def _():
        s_sc[...] = jnp.zeros_like(s_sc)

    heads = range(HG_HEADS_PER_STEP)
    o_intra, q_dec, k_dec, v_bf, cum = _hgrn_blocks(q_ref, f_ref, i_ref, lbraw_ref, lu_ref, m_ref, layer, ROWS)
    s = [s_sc[hh] for hh in heads]
    o = [o_intra[hh] + _dot(q_dec[hh], s[hh].astype(BF16)) for hh in heads]
    upd = [_dot_tn(k_dec[hh], v_bf[hh]) for hh in heads]
    for hh in heads:
        on_ref[:, hh * HG_VAL_DIM:(hh + 1) * HG_VAL_DIM] = _hgrn_gate_norm(o[hh], _head_cols(g_ref, hh), nw_ref[...])
        last_col = cum[hh].T[:, ROWS - 1:ROWS]
        s_sc[hh] = jnp.exp(last_col) * s[hh] + upd[hh]

    @pl.when(c == pl.num_programs(2) - 1)
    def _():
        sout_ref[...] = s_sc[...].reshape(sout_ref.shape)


def _hgrn_sample_body(q_ref, f_ref, i_ref, g_ref, st_ref, lbraw_ref, nw_ref, lu_ref, m_ref,
                      on_ref, sout_ref, o_sc, cumt_sc, qd_sc, kdt_sc, *, layer, seq):
    o_intra, q_dec, k_dec, _, cum = _hgrn_blocks(q_ref, f_ref, i_ref, lbraw_ref, lu_ref, m_ref, layer, seq)
    for hh in range(HG_HEADS_PER_STEP):
        o_sc[hh] = o_intra[hh]
        cumt_sc[hh] = cum[hh].T
        qd_sc[hh] = q_dec[hh]
        kdt_sc[hh] = k_dec[hh].astype(F32).T.astype(BF16)
    shift = seq.bit_length() - 1
    per_slab = SLAB // seq
    slab_seq = lax.shift_right_logical(lax.broadcasted_iota(jnp.int32, (SLAB, HG_VAL_DIM), 0), shift)
    seq_of_row = lax.shift_right_logical(lax.broadcasted_iota(jnp.int32, (ROWS, HG_VAL_DIM), 0), shift)
    lane = lax.broadcasted_iota(jnp.int32, (HG_KEY_DIM, ROWS), 1)

    def per_sequence(s, carry):
        own = seq_of_row == s
        is_last = lane == s * seq + seq - 1
        r0 = pl.multiple_of((s // per_slab) * SLAB, SLAB)
        heads = range(HG_HEADS_PER_STEP)
        s0 = [st_ref[s, hh] for hh in heads]
        o_inter = [_dot(qd_sc[hh, pl.ds(r0, SLAB), :], s0[hh].astype(BF16)) for hh in heads]
        upd = [_dot(kdt_sc[hh], jnp.where(own, _head_cols(i_ref, hh), 0.0).astype(BF16)) for hh in heads]
        for hh in heads:
            o_sc[hh, pl.ds(r0, SLAB), :] += jnp.where(slab_seq == s % per_slab, o_inter[hh], 0.0)
            last_col = jnp.sum(jnp.where(is_last, cumt_sc[hh], 0.0), axis=1, keepdims=True)
            sout_ref[s, hh] = jnp.exp(last_col) * s0[hh] + upd[hh]
        return carry

    lax.fori_loop(0, ROWS // seq, per_sequence, 0, unroll=2)
    for hh in range(HG_HEADS_PER_STEP):
        on_ref[:, hh * HG_VAL_DIM:(hh + 1) * HG_VAL_DIM] = _hgrn_gate_norm(o_sc[hh], _head_cols(g_ref, hh), nw_ref[...])


def _hgrn_token_specs(ridx, hidx):
    width = HG_HEADS_PER_STEP * HG_KEY_DIM

    def col(off):
        return lambda *ids: (ridx(*ids), off // width + hidx(*ids))
    return [pl.BlockSpec((ROWS, width), col(off)) for off in (COL_Q, COL_F, COL_I, COL_G)]


def _hgrn_param_specs(hidx, n_rows, n_masks):
    return [
        pl.BlockSpec((n_rows, HG_HEADS_PER_STEP * HG_KEY_DIM), lambda *ids: (0, hidx(*ids))),
        pl.BlockSpec((1, HG_VAL_DIM), lambda *ids: (0, 0)),
        pl.BlockSpec((2 * ROWS, 3 * ROWS), lambda *ids: (0, 0)),
        pl.BlockSpec((n_masks, ROWS, ROWS), lambda *ids: (0, 0, 0)),
    ]


def _hgrn_prompt(proj, hg_lb, hg_norm, layer, batch, length):
    nblk = length // ROWS
    hb = HG_HEADS_PER_STEP
    lu, m_all = _hgrn_consts(ROWS)
    ridx = lambda b, h, c: b * nblk + c
    hidx = lambda b, h, c: h
    return pl.pallas_call(
        functools.partial(_hgrn_prompt_body, layer=layer),
        grid=(batch, HG_N_HEADS // hb, nblk),
        in_specs=_hgrn_token_specs(ridx, hidx) + _hgrn_param_specs(hidx, hg_lb.shape[0], m_all.shape[0]),
        out_specs=[
            pl.BlockSpec((ROWS, hb * HG_VAL_DIM), lambda b, h, c: (b * nblk + c, h)),
            pl.BlockSpec((1, hb, HG_KEY_DIM, HG_VAL_DIM), lambda b, h, c: (b, h, 0, 0)),
        ],
        out_shape=[
            jax.ShapeDtypeStruct((batch * length, D_MODEL), BF16),
            jax.ShapeDtypeStruct((batch, HG_N_HEADS, HG_KEY_DIM, HG_VAL_DIM), F32),
        ],
        scratch_shapes=[pltpu.VMEM((hb, HG_KEY_DIM, HG_VAL_DIM), F32)],
        compiler_params=_params("parallel", "parallel", "arbitrary"),
        name="hgrn_prompt",
    )(*([proj] * 4 + [hg_lb, hg_norm, lu, m_all]))


def _hgrn_sample(proj, state, hg_lb, hg_norm, layer, row0, batch, seq):
    per_blk = ROWS // seq
    nblk = batch // per_blk
    blk0 = row0 // ROWS
    hb = HG_HEADS_PER_STEP
    lu, m_all = _hgrn_consts(seq)
    ridx = lambda m, h: blk0 + m
    hidx = lambda m, h: h
    state_spec = pl.BlockSpec((per_blk, hb, HG_KEY_DIM, HG_VAL_DIM), lambda m, h: (m, h, 0, 0))
    return pl.pallas_call(
        functools.partial(_hgrn_sample_body, layer=layer, seq=seq),
        grid=(nblk, HG_N_HEADS // hb),
        in_specs=_hgrn_token_specs(ridx, hidx) + [state_spec]
        + _hgrn_param_specs(hidx, hg_lb.shape[0], m_all.shape[0]),
        out_specs=[pl.BlockSpec((ROWS, hb * HG_VAL_DIM), lambda m, h: (m, h)), state_spec],
        out_shape=[jax.ShapeDtypeStruct((batch * seq, D_MODEL), BF16), jax.ShapeDtypeStruct(state.shape, F32)],
        scratch_shapes=[
            pltpu.VMEM((hb, ROWS, HG_VAL_DIM), F32),
            pltpu.VMEM((hb, HG_KEY_DIM, ROWS), F32),
            pltpu.VMEM((hb, ROWS, HG_KEY_DIM), BF16),
            pltpu.VMEM((hb, HG_KEY_DIM, ROWS), BF16),
        ],
        compiler_params=_params("parallel", "parallel"),
        name="hgrn_sample",
    )(*([proj] * 4 + [state, hg_lb, hg_norm, lu, m_all]))


def _head_slabs(v):
    v = v.reshape(SSM_N_GROUPS, HEADS_PER_GROUP).astype(F32)
    return jnp.pad(v, ((0, 0), (0, LANES - HEADS_PER_GROUP))).reshape(1, SSM_N_GROUPS * LANES)


def _sample_carry(conv0, seq):
    bsz, wm1, cd = conv0.shape
    return jnp.pad(conv0, ((0, 0), (seq - wm1, 0), (0, 0))).reshape(bsz * seq, cd)


def kernel(x_prompt, x_sample, p_prompt, p_sample, state_ssm, state_conv, state_hgrn, norm_mix_pre, w_in, conv_w, conv_b, dt_bias, a_log, d_skip, ssm_norm, w_br_a, hg_lb, hg_norm, w_br_b, w_out, norm_mix_post, norm_ffn_pre, w_up, w_down, norm_ffn_post, norm_ple, w_ple_gate, w_ple_proj):
    depth = w_in.shape[0]
    bp, lp, _ = x_prompt.shape
    bs, ls, _ = x_sample.shape
    tp, ts = bp * lp, bs * ls
    wm1 = SSM_CONV_WIDTH - 1
    assert lp % ROWS == 0 and ROWS % ls == 0 and ls & (ls - 1) == 0 and ts % ROWS == 0
    assert lp >= wm1 and ls >= wm1

    xp = x_prompt.reshape(tp, D_MODEL)
    xs = x_sample.reshape(ts, D_MODEL)
    row = lambda a: a.reshape(1, -1).astype(F32)
    outs = {k: [] for k in ("ssm_p", "conv_p", "hg_p", "ssm_s", "conv_s", "hg_s")}
    for li in range(depth):
        h = _rmsnorm_cast(xp, xs, row(norm_mix_pre[li]))
        proj = _in_proj(h, jnp.transpose(w_in[li]))

        tails = [proj[b * lp + lp - wm1:b * lp + lp, COL_X:COL_Q] for b in range(bp)]
        outs["conv_p"].append(jnp.stack(tails))
        tails_s = [proj[tp + ls - wm1 + r::ls, COL_X:COL_Q] for r in range(wm1)]
        outs["conv_s"].append(jnp.transpose(jnp.stack(tails_s), (1, 0, 2)))

        ssd_params = (conv_w[li], row(conv_b[li]), _head_slabs(dt_bias[li]), _head_slabs(a_log[li]),
                      row(jnp.repeat(d_skip[li], SSM_HEAD_DIM)), row(ssm_norm[li]))
        yn_p, ssm_p = _ssd_prompt(proj, ssd_params, bp, lp)
        yn_s, ssm_s = _ssd_sample(proj, _sample_carry(state_conv[li], ls), state_ssm[li], ssd_params, tp, bs, ls)
        on_p, hg_p = _hgrn_prompt(proj, hg_lb, row(hg_norm[li]), li, bp, lp)
        on_s, hg_s = _hgrn_sample(proj, state_hgrn[li], hg_lb, row(hg_norm[li]), li, tp, bs, ls)
        outs["ssm_p"].append(ssm_p)
        outs["ssm_s"].append(ssm_s)
        outs["hg_p"].append(hg_p)
        outs["hg_s"].append(hg_s)

        mixed = _merge(yn_p, yn_s, w_br_a[li].astype(BF16), on_p, on_s, w_br_b[li].astype(BF16), proj)
        x1, h2 = _out_proj(mixed, w_out[li].astype(BF16), xp, xs, row(norm_mix_post[li]), row(norm_ffn_pre[li]))
        u = _matmul(h2, w_up[li].astype(BF16), out_dtype=BF16, act="relu2", name="ffn_up")
        x2, h3 = _ffn_down(u, w_down[li].astype(BF16), x1, row(norm_ffn_post[li]), row(norm_ple[li]))
        xp, xs = _ple(h3, w_ple_gate[li].astype(BF16), p_prompt[li].reshape(tp, -1), p_sample[li].reshape(ts, -1),
                      w_ple_proj[li].astype(BF16), x2)

    stack = lambda k: jnp.stack(outs[k])
    return (xp.reshape(bp, lp, D_MODEL), xs.reshape(bs, ls, D_MODEL),
            stack("ssm_p"), stack("conv_p"), stack("hg_p"), stack("ssm_s"), stack("conv_s"), stack("hg_s"))
```

```python
import functools
import math

import numpy as np
import jax
import jax.numpy as jnp
from jax import lax
from jax.experimental import pallas as pl
from jax.experimental.pallas import tpu as pltpu

F32 = jnp.float32
BF16 = jnp.bfloat16

D_MODEL = 2048
SSM_D_INNER = 2 * D_MODEL
SSM_HEAD_DIM = 64
SSM_N_HEADS = SSM_D_INNER // SSM_HEAD_DIM
SSM_N_GROUPS = 8
SSM_D_STATE = 128
SSM_CONV_WIDTH = 4
SSM_BC = SSM_N_GROUPS * SSM_D_STATE
SSM_CONV_DIM = SSM_D_INNER + 2 * SSM_BC
HG_KEY_DIM = 128
HG_N_HEADS = D_MODEL // HG_KEY_DIM
HG_VAL_DIM = D_MODEL // HG_N_HEADS
FFN_HIDDEN = 4 * D_MODEL
NORM_EPS = 1e-6

OFF_XBC = SSM_D_INNER
OFF_DT = OFF_XBC + SSM_CONV_DIM
OFF_HQ = OFF_DT + SSM_N_HEADS

LANES = 128
ROWS = 128
GROUP_W = SSM_D_INNER // SSM_N_GROUPS
HEADS_PER_GROUP = SSM_N_HEADS // SSM_N_GROUPS
HG_HEADS_PER_STEP = 4
SSD_GROUPS_PER_STEP = 4
LOG2E = 1.4426950408889634

COL_Z = 0
COL_X = SSM_D_INNER
COL_B = COL_X + SSM_D_INNER
COL_C = COL_B + SSM_BC
COL_Q = COL_C + SSM_BC
COL_F = COL_Q + D_MODEL
COL_I = COL_F + D_MODEL
COL_G = COL_I + D_MODEL
COL_GA = COL_G + D_MODEL
COL_GB = COL_GA + D_MODEL
COL_DT = COL_GB + D_MODEL
PROJ_W = COL_DT + SSM_N_GROUPS * LANES

VMEM_LIMIT = 52 * 1024 * 1024


def _params(*sem):
    return pltpu.CompilerParams(dimension_semantics=sem, vmem_limit_bytes=VMEM_LIMIT)


def _pick(n, cands):
    for c in cands:
        if n % c == 0:
            return c
    raise ValueError(f"no tile for {n} in {cands}")


def _dot(a, b):
    return jnp.dot(a, b, preferred_element_type=F32)


def _dot_nt(a, b):
    return lax.dot_general(a, b, (((1,), (1,)), ((), ())), preferred_element_type=F32)


def _dot_tn(a, b):
    return lax.dot_general(a, b, (((0,), (0,)), ((), ())), preferred_element_type=F32)


def _split3(x):
    hi = x.astype(BF16)
    r1 = x - hi.astype(F32)
    mid = r1.astype(BF16)
    lo = (r1 - mid.astype(F32)).astype(BF16)
    return hi, mid, lo


def _dot3_l(a01x3, x):
    return _dot(a01x3, jnp.concatenate(_split3(x), axis=0))


def _dot3_r(x, b01x3):
    return _dot(jnp.concatenate(_split3(x), axis=1), b01x3)


def _sigmoid(x):
    return 1.0 / (1.0 + jnp.exp(-x))


def _silu(x):
    return x * _sigmoid(x)


def _softplus(x):
    return jnp.maximum(x, 0.0) + jnp.log1p(jnp.exp(-jnp.abs(x)))


def _rms(x, w):
    ms = jnp.mean(x * x, axis=-1, keepdims=True)
    return x * lax.rsqrt(ms + NORM_EPS) * w


def _seq_masks(seq):
    t = np.arange(ROWS)
    same = (t[:, None] // seq) == (t[None, :] // seq)
    lower = same & (t[None, :] <= t[:, None])
    upper = same & (t[None, :] > t[:, None])
    return lower, upper


def _ssd_consts(seq):
    lower, upper = _seq_masks(seq)
    lu = np.concatenate([lower, upper], axis=0).astype(np.float32)
    rep = np.zeros((LANES, GROUP_W), np.float32)
    for j in range(HEADS_PER_GROUP):
        rep[j, j * SSM_HEAD_DIM:(j + 1) * SSM_HEAD_DIM] = 1.0
    return jnp.asarray(np.tile(lu, (1, 3)), BF16), jnp.asarray(np.tile(rep, (3, 1)), BF16)


def _level_widths(seq):
    widths, b = [], seq // 2
    while b >= 1:
        widths.append(b)
        b //= 2
    return widths


def _hgrn_consts(seq):
    lower, upper = _seq_masks(seq)
    t = np.arange(ROWS)
    masks = []
    for b in _level_widths(seq):
        is_q = (t // b) % 2 == 1
        pair = (t[:, None] // (2 * b)) == (t[None, :] // (2 * b))
        masks.append(pair & is_q[:, None] & (~is_q)[None, :])
    masks.append(np.eye(ROWS, dtype=bool))
    m_all = np.stack(masks).astype(np.float32)
    assert np.array_equal(m_all.sum(0) > 0, lower) and m_all.sum(0).max() == 1
    lu = np.concatenate([lower, upper], axis=0).astype(np.float32)
    return jnp.asarray(np.tile(lu, (1, 3)), BF16), jnp.asarray(m_all, F32)


def _group_specs(tm, n_prompt_blocks, width, axis=0):
    last = n_prompt_blocks - 1
    return [
        pl.BlockSpec((tm, width), lambda *ids: (jnp.minimum(ids[axis], last), 0)),
        pl.BlockSpec((tm, width), lambda *ids: (jnp.maximum(ids[axis] - n_prompt_blocks, 0), 0)),
    ]


def _group_rows(xp_ref, xs_ref, n_prompt_blocks, axis=0):
    return jnp.where(pl.program_id(axis) < n_prompt_blocks, xp_ref[...], xs_ref[...])


def _rmsnorm_body(xp_ref, xs_ref, w_ref, o_ref, *, n_prompt_blocks):
    o_ref[...] = _rms(_group_rows(xp_ref, xs_ref, n_prompt_blocks), w_ref[...]).astype(o_ref.dtype)


def _rmsnorm_cast(xp, xs, w):
    tp, d = xp.shape
    ts = xs.shape[0]
    tm = _pick(math.gcd(tp, ts), (512, 256, 128))
    return pl.pallas_call(
        functools.partial(_rmsnorm_body, n_prompt_blocks=tp // tm),
        grid=((tp + ts) // tm,),
        in_specs=_group_specs(tm, tp // tm, d) + [pl.BlockSpec((1, d), lambda i: (0, 0))],
        out_specs=pl.BlockSpec((tm, d), lambda i: (i, 0)),
        out_shape=jax.ShapeDtypeStruct((tp + ts, d), BF16),
        compiler_params=_params("parallel"),
        name="rmsnorm_cast",
    )(xp, xs, w)


IN_TN = 1024
IN_SHIFT = OFF_HQ - OFF_DT
N_SEG1 = OFF_DT // IN_TN
N_SEG2 = (COL_DT - OFF_DT) // IN_TN


def _in_proj_body(h_ref, wa_ref, wb_ref, wd_ref, o_ref, w_sc):
    j = pl.program_id(0)

    @pl.when(pl.program_id(1) == 0)
    def _():
        @pl.when(j < N_SEG1)
        def _():
            w_sc[...] = wa_ref[...].astype(BF16)

        @pl.when(jnp.logical_and(j >= N_SEG1, j < N_SEG1 + N_SEG2))
        def _():
            w_sc[0:IN_TN - IN_SHIFT] = wa_ref[IN_SHIFT:IN_TN].astype(BF16)
            w_sc[IN_TN - IN_SHIFT:IN_TN] = wb_ref[...].astype(BF16)

        @pl.when(j >= N_SEG1 + N_SEG2)
        def _():
            pad = jnp.zeros((LANES - IN_SHIFT, wd_ref.shape[1]), F32)
            w_sc[0:LANES] = jnp.concatenate([wd_ref[...], pad], axis=0).astype(BF16)

    @pl.when(j < N_SEG1 + N_SEG2)
    def _():
        o_ref[...] = _dot_nt(h_ref[...], w_sc[...])

    @pl.when(j >= N_SEG1 + N_SEG2)
    def _():
        o_ref[:, 0:LANES] = _dot_nt(h_ref[...], w_sc[0:LANES])
        o_ref[:, LANES:IN_TN] = jnp.zeros((o_ref.shape[0], IN_TN - LANES), F32)


def _in_proj(h, wt):
    t, d = h.shape
    assert OFF_DT % IN_TN == 0 and (COL_DT - OFF_DT) % IN_TN == 0 and PROJ_W - COL_DT == IN_TN
    assert IN_TN % IN_SHIFT == 0 and IN_SHIFT % 8 == 0 and IN_SHIFT == SSM_N_HEADS <= LANES
    tm = _pick(t, (1088, 512, 256, 128))
    last_a = N_SEG1 + N_SEG2 - 1
    per = IN_TN // IN_SHIFT
    return pl.pallas_call(
        _in_proj_body,
        grid=(PROJ_W // IN_TN, t // tm),
        in_specs=[
            pl.BlockSpec((tm, d), lambda j, i: (i, 0)),
            pl.BlockSpec((IN_TN, d), lambda j, i: (jnp.minimum(j, last_a), 0)),
            pl.BlockSpec((IN_SHIFT, d), lambda j, i: (jnp.minimum(j, last_a) * per + per, 0)),
            pl.BlockSpec((IN_SHIFT, d), lambda j, i: (OFF_DT // IN_SHIFT, 0)),
        ],
        out_specs=pl.BlockSpec((tm, IN_TN), lambda j, i: (i, j)),
        out_shape=jax.ShapeDtypeStruct((t, PROJ_W), F32),
        scratch_shapes=[pltpu.VMEM((IN_TN, d), BF16)],
        compiler_params=_params("arbitrary", "arbitrary"),
        name="in_proj",
    )(h, wt, wt, wt)


def _matmul_body(a_ref, b_ref, o_ref, *, act):
    acc = _dot(a_ref[...], b_ref[...])
    if act == "relu2":
        acc = jnp.square(jnp.maximum(acc, 0.0))
    o_ref[...] = acc.astype(o_ref.dtype)


def _matmul(a, b, *, out_dtype, act=None, name):
    m, k = a.shape
    n = b.shape[1]
    tm = _pick(m, (1088, 512, 256, 128))
    tn = _pick(n, (2048, 1024, 512))
    return pl.pallas_call(
        functools.partial(_matmul_body, act=act),
        grid=(m // tm, n // tn),
        in_specs=[pl.BlockSpec((tm, k), lambda i, j: (i, 0)), pl.BlockSpec((k, tn), lambda i, j: (0, j))],
        out_specs=pl.BlockSpec((tm, tn), lambda i, j: (i, j)),
        out_shape=jax.ShapeDtypeStruct((m, n), out_dtype),
        compiler_params=_params("parallel", "parallel"),
        name=name,
    )(a, b)


def _merge_body(yap_ref, yas_ref, wa_ref, ybp_ref, ybs_ref, wb_ref, ga_ref, gb_ref, o_ref, *, n_prompt_blocks):
    a = _dot(_group_rows(yap_ref, yas_ref, n_prompt_blocks, axis=1), wa_ref[...])
    b = _dot(_group_rows(ybp_ref, ybs_ref, n_prompt_blocks, axis=1), wb_ref[...])
    o_ref[...] = (_sigmoid(ga_ref[...]) * a + _sigmoid(gb_ref[...]) * b).astype(o_ref.dtype)


def _merge(ya_p, ya_s, wa, yb_p, yb_s, wb, proj):
    tp, ts = ya_p.shape[0], ya_s.shape[0]
    tm = _pick(math.gcd(tp, ts), (256, 128))
    tn = 1024
    npb = tp // tm
    return pl.pallas_call(
        functools.partial(_merge_body, n_prompt_blocks=npb),
        grid=(D_MODEL // tn, (tp + ts) // tm),
        in_specs=_group_specs(tm, npb, SSM_D_INNER, axis=1)
        + [pl.BlockSpec((SSM_D_INNER, tn), lambda j, i: (0, j))]
        + _group_specs(tm, npb, D_MODEL, axis=1)
        + [pl.BlockSpec((D_MODEL, tn), lambda j, i: (0, j)),
           pl.BlockSpec((tm, tn), lambda j, i: (i, COL_GA // tn + j)),
           pl.BlockSpec((tm, tn), lambda j, i: (i, COL_GB // tn + j))],
        out_specs=pl.BlockSpec((tm, tn), lambda j, i: (i, j)),
        out_shape=jax.ShapeDtypeStruct((tp + ts, D_MODEL), BF16),
        compiler_params=_params("parallel", "parallel"),
        name="merge_branches",
    )(ya_p, ya_s, wa, yb_p, yb_s, wb, proj, proj)


def _out_body(a_ref, w_ref, xp_ref, xs_ref, npost_ref, nnext_ref, x1_ref, h_ref, *, n_prompt_blocks):
    acc = _dot(a_ref[...], w_ref[...])
    x1 = _group_rows(xp_ref, xs_ref, n_prompt_blocks) + _rms(acc, npost_ref[...])
    x1_ref[...] = x1
    h_ref[...] = _rms(x1, nnext_ref[...]).astype(h_ref.dtype)


def _out_proj(a, w, xp, xs, npost, nnext):
    t = a.shape[0]
    tp, ts = xp.shape[0], xs.shape[0]
    tm = _pick(math.gcd(tp, ts), (256, 128))
    row = lambda i: (i, 0)
    fixed = lambda i: (0, 0)
    return pl.pallas_call(
        functools.partial(_out_body, n_prompt_blocks=tp // tm),
        grid=(t // tm,),
        in_specs=[pl.BlockSpec((tm, D_MODEL), row), pl.BlockSpec((D_MODEL, D_MODEL), fixed)]
        + _group_specs(tm, tp // tm, D_MODEL)
        + [pl.BlockSpec((1, D_MODEL), fixed), pl.BlockSpec((1, D_MODEL), fixed)],
        out_specs=[pl.BlockSpec((tm, D_MODEL), row), pl.BlockSpec((tm, D_MODEL), row)],
        out_shape=[jax.ShapeDtypeStruct((t, D_MODEL), F32), jax.ShapeDtypeStruct((t, D_MODEL), BF16)],
        compiler_params=_params("parallel"),
        name="out_proj_norm",
    )(a, w, xp, xs, npost, nnext)


def _down_body(u_ref, w_ref, x_ref, npost_ref, nnext_ref, x2_ref, h_ref, acc_ref):
    k = pl.program_id(1)

    @pl.when(k == 0)
    def _():
        acc_ref[...] = jnp.zeros_like(acc_ref)

    acc_ref[...] += _dot(u_ref[...], w_ref[...])

    @pl.when(k == pl.num_programs(1) - 1)
    def _():
        x2 = x_ref[...] + _rms(acc_ref[...], npost_ref[...])
        x2_ref[...] = x2
        h_ref[...] = _rms(x2, nnext_ref[...]).astype(h_ref.dtype)


def _ffn_down(u, w, x, npost, nnext):
    t, f = u.shape
    tm = _pick(t, (544, 512, 256, 128))
    tk = 2048
    row = lambda i, k: (i, 0)
    fixed = lambda i, k: (0, 0)
    return pl.pallas_call(
        _down_body,
        grid=(t // tm, f // tk),
        in_specs=[
            pl.BlockSpec((tm, tk), lambda i, k: (i, k)),
            pl.BlockSpec((tk, D_MODEL), lambda i, k: (k, 0)),
            pl.BlockSpec((tm, D_MODEL), row),
            pl.BlockSpec((1, D_MODEL), fixed),
            pl.BlockSpec((1, D_MODEL), fixed),
        ],
        out_specs=[pl.BlockSpec((tm, D_MODEL), row), pl.BlockSpec((tm, D_MODEL), row)],
        out_shape=[jax.ShapeDtypeStruct((t, D_MODEL), F32), jax.ShapeDtypeStruct((t, D_MODEL), BF16)],
        scratch_shapes=[pltpu.VMEM((tm, D_MODEL), F32)],
        compiler_params=_params("parallel", "arbitrary"),
        name="ffn_down_norm",
    )(u, w, x, npost, nnext)


def _ple_body(h_ref, wg_ref, pp_ref, ps_ref, wp_ref, x_ref, yp_ref, ys_ref, *, n_prompt_blocks):
    i = pl.program_id(0)
    g = _sigmoid(_dot(h_ref[...], wg_ref[...]))
    e = _dot(_group_rows(pp_ref, ps_ref, n_prompt_blocks).astype(BF16), wp_ref[...])
    y = x_ref[...] + g * e

    @pl.when(i < n_prompt_blocks)
    def _():
        yp_ref[...] = y

    @pl.when(i >= n_prompt_blocks)
    def _():
        ys_ref[...] = y


def _ple(h, wg, pp, ps, wp, x):
    tp, pd = pp.shape
    ts = ps.shape[0]
    tm = _pick(math.gcd(tp, ts), (256, 128))
    npb = tp // tm
    row = lambda i: (i, 0)
    fixed = lambda i: (0, 0)
    return pl.pallas_call(
        functools.partial(_ple_body, n_prompt_blocks=npb),
        grid=((tp + ts) // tm,),
        in_specs=[pl.BlockSpec((tm, D_MODEL), row), pl.BlockSpec((D_MODEL, D_MODEL), fixed)]
        + _group_specs(tm, npb, pd)
        + [pl.BlockSpec((pd, D_MODEL), fixed), pl.BlockSpec((tm, D_MODEL), row)],
        out_specs=_group_specs(tm, npb, D_MODEL),
        out_shape=[jax.ShapeDtypeStruct((tp, D_MODEL), F32), jax.ShapeDtypeStruct((ts, D_MODEL), F32)],
        compiler_params=_params("arbitrary"),
        name="ple_gate",
    )(h, wg, pp, ps, wp, x)


def _conv_taps(cur, shifted, cw, cb):
    acc = cw[SSM_CONV_WIDTH - 1:SSM_CONV_WIDTH, :] * cur
    for d in range(1, SSM_CONV_WIDTH):
        acc = acc + cw[SSM_CONV_WIDTH - 1 - d:SSM_CONV_WIDTH - d, :] * shifted[d]
    return _silu(acc + cb)


def _conv_prompt(cur_ref, halo_sc, cw_ref, cb_ref):
    cur = cur_ref[...]
    top = cur[0:8]
    halo = halo_sc[...]
    row8 = lax.broadcasted_iota(jnp.int32, top.shape, 0)
    shifted, shifted_top = {}, {}
    for d in range(1, SSM_CONV_WIDTH):
        shifted[d] = pltpu.roll(cur, d, 0)
        shifted_top[d] = jnp.where(row8 < d, pltpu.roll(halo, d, 0), pltpu.roll(top, d, 0))
    cw = cw_ref[...]
    cb = cb_ref[...]
    out = _conv_taps(cur, shifted, cw, cb)
    out_top = _conv_taps(top, shifted_top, cw, cb)
    halo_sc[...] = cur[ROWS - 8:ROWS]
    return jnp.concatenate([out_top, out[8:]], axis=0)


def _conv_sample(cur_ref, carry_ref, cw_ref, cb_ref, seq):
    cur = cur_ref[...]
    carry = carry_ref[...]
    pos = lax.broadcasted_iota(jnp.int32, cur.shape, 0) & (seq - 1)
    shifted = {}
    for d in range(1, SSM_CONV_WIDTH):
        shifted[d] = jnp.where(pos >= d, pltpu.roll(cur, d, 0), pltpu.roll(carry, ROWS - (seq - d), 0))
    return _conv_taps(cur, shifted, cw_ref[...], cb_ref[...])


def _ssd_blocks(xg, bg_bf, cg_bf, dt_raw, dtb, alog, lu_ref, rep_ref, y_sc):
    groups = range(len(xg))
    lu = lu_ref[...]
    mask = lu[0:ROWS, 0:ROWS].astype(F32) > 0.5
    dt = [_softplus(dt_raw[g] + dtb[g]) for g in groups]
    cr = [_dot3_l(lu, dt[g] * (-jnp.exp(alog[g]))) for g in groups]
    cum = [c[0:ROWS] for c in cr]
    col = [c * LOG2E for c in cum]
    row_t = [(col[g] - jnp.log2(dt[g])).T for g in groups]
    x_bf = [x.astype(BF16) for x in xg]
    cb = [_dot_nt(cg_bf[g], bg_bf[g]) for g in groups]
    for j in range(HEADS_PER_GROUP):
        hs = slice(j * SSM_HEAD_DIM, (j + 1) * SSM_HEAD_DIM)
        for g in groups:
            seg = col[g][:, j:j + 1] - row_t[g][j:j + 1, :]
            m = (cb[g] * jnp.where(mask, jnp.exp2(seg), 0.0)).astype(BF16)
            y_sc[:, g * GROUP_W + j * SSM_HEAD_DIM:g * GROUP_W + (j + 1) * SSM_HEAD_DIM] = _dot(m, x_bf[g][:, hs])
    rep = rep_ref[...]
    slabs = [_dot3_r(jnp.concatenate([jnp.exp2(col[g]), dt[g] * jnp.exp(cr[g][ROWS:2 * ROWS])], axis=0), rep)
             for g in groups]
    return cum, [sl[0:ROWS] for sl in slabs], [sl[ROWS:2 * ROWS] for sl in slabs]


def _group_dt(dt_all, group):
    return pltpu.roll(dt_all, (LANES - group * HEADS_PER_GROUP) % LANES, 1)


def _head_rows(e_row):
    return jnp.concatenate(
        [jnp.broadcast_to(e_row[0:1, j:j + 1], (SSM_HEAD_DIM, SSM_D_STATE)) for j in range(HEADS_PER_GROUP)], axis=0)


def _gated_group_norm(y, z, nw):
    return _rms(y * _silu(z), nw)


def _ssd_prompt_body(x_ref, b_ref, c_ref, z_ref, dt_ref, cwx_ref, cwb_ref, cwc_ref, cbx_ref, cbb_ref, cbc_ref,
                     dtb_ref, alog_ref, dsk_ref, nw_ref, lu_ref, rep_ref,
                     yn_ref, hout_ref, h_sc, px_sc, pb_sc, pc_sc, y_sc):
    c = pl.program_id(2)
    groups = range(SSD_GROUPS_PER_STEP)

    @pl.when(c == 0)
    def _():
        h_sc[...] = jnp.zeros_like(h_sc)
        px_sc[...] = jnp.zeros_like(px_sc)
        pb_sc[...] = jnp.zeros_like(pb_sc)
        pc_sc[...] = jnp.zeros_like(pc_sc)

    x_all = _conv_prompt(x_ref, px_sc, cwx_ref, cbx_ref)
    b_all = _conv_prompt(b_ref, pb_sc, cwb_ref, cbb_ref).astype(BF16)
    c_all = _conv_prompt(c_ref, pc_sc, cwc_ref, cbc_ref).astype(BF16)
    wide = [slice(g * GROUP_W, (g + 1) * GROUP_W) for g in groups]
    narrow = [slice(g * SSM_D_STATE, (g + 1) * SSM_D_STATE) for g in groups]
    xg = [x_all[:, wide[g]] for g in groups]
    bg = [b_all[:, narrow[g]] for g in groups]
    cg = [c_all[:, narrow[g]] for g in groups]
    cum, e_rep, w_rep = _ssd_blocks(
        xg, bg, cg, [_group_dt(dt_ref[...], pl.program_id(1) * SSD_GROUPS_PER_STEP + g) for g in groups],
        [dtb_ref[:, narrow[g]] for g in groups],
        [alog_ref[:, narrow[g]] for g in groups], lu_ref, rep_ref, y_sc)
    h = [h_sc[wide[g]] for g in groups]
    y_inter = [_dot_nt(cg[g], h[g].astype(BF16)) for g in groups]
    upd = [_dot_tn((xg[g] * w_rep[g]).astype(BF16), bg[g]) for g in groups]
    for g in groups:
        y = y_sc[:, wide[g]] + y_inter[g] * e_rep[g] + dsk_ref[:, wide[g]] * xg[g]
        yn_ref[:, wide[g]] = _gated_group_norm(y, z_ref[:, wide[g]], nw_ref[:, wide[g]]).astype(yn_ref.dtype)
        h_sc[wide[g]] = _head_rows(jnp.exp(cum[g][ROWS - 1:ROWS, :])) * h[g] + upd[g]

    @pl.when(c == pl.num_programs(2) - 1)
    def _():
        hout_ref[...] = h_sc[...].reshape(hout_ref.shape)


SLAB = 16


def _ssd_sample_body(x_ref, b_ref, c_ref, z_ref, dt_ref, hx_ref, hb_ref, hc_ref, st_ref,
                     cwx_ref, cwb_ref, cwc_ref, cbx_ref, cbb_ref, cbc_ref,
                     dtb_ref, alog_ref, dsk_ref, nw_ref, lu_ref, rep_ref,
                     yn_ref, hout_ref, y_sc, yi_sc, cum_sc, c_sc, *, seq):
    xg = _conv_sample(x_ref, hx_ref, cwx_ref, cbx_ref, seq)
    bg = _conv_sample(b_ref, hb_ref, cwb_ref, cbb_ref, seq)
    cg = _conv_sample(c_ref, hc_ref, cwc_ref, cbc_ref, seq)
    c_bf = cg.astype(BF16)
    cums, e_reps, w_reps = _ssd_blocks([xg], [bg.astype(BF16)], [c_bf], [_group_dt(dt_ref[...], pl.program_id(1))],
                                       [dtb_ref[...]], [alog_ref[...]],
                                       lu_ref, rep_ref, y_sc)
    cum_sc[...] = cums[0]
    c_sc[...] = c_bf
    yi_sc[...] = jnp.zeros_like(yi_sc)
    xw_t = (xg * w_reps[0]).T.astype(BF16)
    shift = seq.bit_length() - 1
    per_slab = SLAB // seq
    slab_seq = lax.shift_right_logical(lax.broadcasted_iota(jnp.int32, (SLAB, GROUP_W), 0), shift)
    seq_of_row_n = lax.shift_right_logical(lax.broadcasted_iota(jnp.int32, (ROWS, SSM_D_STATE), 0), shift)
    state_shape = (HEADS_PER_GROUP * SSM_HEAD_DIM, SSM_D_STATE)

    def per_sequence(s, carry):
        h0 = st_ref[s].reshape(state_shape)
        r0 = pl.multiple_of((s // per_slab) * SLAB, SLAB)
        y_inter = _dot_nt(c_sc[pl.ds(r0, SLAB), :], h0.astype(BF16))
        yi_sc[pl.ds(r0, SLAB), :] += jnp.where(slab_seq == s % per_slab, y_inter, 0.0)
        b_own = jnp.where(seq_of_row_n == s, bg, 0.0).astype(BF16)
        last = cum_sc[pl.ds(s * seq + seq - 1, 1), :]
        h_new = _head_rows(jnp.exp(last)) * h0 + _dot(xw_t, b_own)
        hout_ref[s] = h_new.reshape(hout_ref.shape[1:])
        return carry

    lax.fori_loop(0, ROWS // seq, per_sequence, 0, unroll=2)
    y = y_sc[...] + yi_sc[...] * e_reps[0] + dsk_ref[...] * xg
    yn_ref[...] = _gated_group_norm(y, z_ref[...], nw_ref[...]).astype(yn_ref.dtype)


def _ssd_param_specs(gidx, ng):
    wide, narrow = ng * GROUP_W, ng * SSM_D_STATE

    def col(block, off):
        return lambda *ids: (0, off // block + gidx(*ids))
    return [
        pl.BlockSpec((SSM_CONV_WIDTH, wide), col(wide, 0)),
        pl.BlockSpec((SSM_CONV_WIDTH, narrow), col(narrow, SSM_D_INNER)),
        pl.BlockSpec((SSM_CONV_WIDTH, narrow), col(narrow, SSM_D_INNER + SSM_BC)),
        pl.BlockSpec((1, wide), col(wide, 0)),
        pl.BlockSpec((1, narrow), col(narrow, SSM_D_INNER)),
        pl.BlockSpec((1, narrow), col(narrow, SSM_D_INNER + SSM_BC)),
        pl.BlockSpec((1, narrow), col(narrow, 0)),
        pl.BlockSpec((1, narrow), col(narrow, 0)),
        pl.BlockSpec((1, wide), col(wide, 0)),
        pl.BlockSpec((1, wide), col(wide, 0)),
        pl.BlockSpec((2 * ROWS, 3 * ROWS), lambda *ids: (0, 0)),
        pl.BlockSpec((3 * LANES, GROUP_W), lambda *ids: (0, 0)),
    ]


def _ssd_token_specs(ridx, gidx, ng):
    wide, narrow = ng * GROUP_W, ng * SSM_D_STATE

    def col(block, off):
        return lambda *ids: (ridx(*ids), off // block + gidx(*ids))
    return [
        pl.BlockSpec((ROWS, wide), col(wide, COL_X)),
        pl.BlockSpec((ROWS, narrow), col(narrow, COL_B)),
        pl.BlockSpec((ROWS, narrow), col(narrow, COL_C)),
        pl.BlockSpec((ROWS, wide), col(wide, COL_Z)),
        pl.BlockSpec((ROWS, LANES), lambda *ids: (ridx(*ids), COL_DT // LANES)),
    ]


def _ssd_prompt(proj, ssd_params, batch, length):
    nblk = length // ROWS
    ng = SSD_GROUPS_PER_STEP
    lu, rep = _ssd_consts(ROWS)
    ridx = lambda b, g, c: b * nblk + c
    gidx = lambda b, g, c: g
    conv_w, conv_b, dtb, alog, dsk, nw = ssd_params
    args = [proj] * 5 + [conv_w] * 3 + [conv_b] * 3 + [dtb, alog, dsk, nw, lu, rep]
    return pl.pallas_call(
        _ssd_prompt_body,
        grid=(batch, SSM_N_GROUPS // ng, nblk),
        in_specs=_ssd_token_specs(ridx, gidx, ng) + _ssd_param_specs(gidx, ng),
        out_specs=[
            pl.BlockSpec((ROWS, ng * GROUP_W), lambda b, g, c: (b * nblk + c, g)),
            pl.BlockSpec((1, ng * HEADS_PER_GROUP, SSM_HEAD_DIM, SSM_D_STATE), lambda b, g, c: (b, g, 0, 0)),
        ],
        out_shape=[
            jax.ShapeDtypeStruct((batch * length, SSM_D_INNER), BF16),
            jax.ShapeDtypeStruct((batch, SSM_N_HEADS, SSM_HEAD_DIM, SSM_D_STATE), F32),
        ],
        scratch_shapes=[
            pltpu.VMEM((ng * HEADS_PER_GROUP * SSM_HEAD_DIM, SSM_D_STATE), F32),
            pltpu.VMEM((8, ng * GROUP_W), F32),
            pltpu.VMEM((8, ng * SSM_D_STATE), F32),
            pltpu.VMEM((8, ng * SSM_D_STATE), F32),
            pltpu.VMEM((ROWS, ng * GROUP_W), F32),
        ],
        compiler_params=_params("parallel", "parallel", "arbitrary"),
        name="ssd_prompt",
    )(*args)


def _ssd_sample(proj, carry, state, ssd_params, row0, batch, seq):
    per_blk = ROWS // seq
    nblk = batch // per_blk
    blk0 = row0 // ROWS
    lu, rep = _ssd_consts(seq)
    ridx = lambda m, g: blk0 + m
    gidx = lambda m, g: g
    conv_w, conv_b, dtb, alog, dsk, nw = ssd_params

    def carry_spec(block, off):
        return pl.BlockSpec((ROWS, block), lambda m, g: (m, off // block + g))

    state_spec = pl.BlockSpec((per_blk, HEADS_PER_GROUP, SSM_HEAD_DIM, SSM_D_STATE), lambda m, g: (m, g, 0, 0))
    args = [proj] * 5 + [carry] * 3 + [state] + [conv_w] * 3 + [conv_b] * 3 + [dtb, alog, dsk, nw, lu, rep]
    return pl.pallas_call(
        functools.partial(_ssd_sample_body, seq=seq),
        grid=(nblk, SSM_N_GROUPS),
        in_specs=_ssd_token_specs(ridx, gidx, 1)
        + [carry_spec(GROUP_W, 0), carry_spec(SSM_D_STATE, SSM_D_INNER), carry_spec(SSM_D_STATE, SSM_D_INNER + SSM_BC),
           state_spec]
        + _ssd_param_specs(gidx, 1),
        out_specs=[pl.BlockSpec((ROWS, GROUP_W), lambda m, g: (m, g)), state_spec],
        out_shape=[jax.ShapeDtypeStruct((batch * seq, SSM_D_INNER), BF16), jax.ShapeDtypeStruct(state.shape, F32)],
        scratch_shapes=[
            pltpu.VMEM((ROWS, GROUP_W), F32),
            pltpu.VMEM((ROWS, GROUP_W), F32),
            pltpu.VMEM((ROWS, LANES), F32),
            pltpu.VMEM((ROWS, SSM_D_STATE), BF16),
        ],
        compiler_params=_params("parallel", "parallel"),
        name="ssd_sample",
    )(*args)


def _lower_bound(raw, layer):
    e = jnp.exp(raw - jnp.max(raw, axis=0, keepdims=True))
    return jnp.sum(e[0:layer + 1], axis=0, keepdims=True) / jnp.sum(e, axis=0, keepdims=True)


def _head_cols(ref, hh):
    return ref[:, hh * HG_KEY_DIM:(hh + 1) * HG_KEY_DIM]


def _level_ref(cum, b):
    width = cum.shape[1]
    if b >= 8:
        blocks = [jnp.broadcast_to(cum[p * 2 * b + b - 1:p * 2 * b + b, :], (2 * b, width))
                  for p in range(ROWS // (2 * b))]
        return blocks[0] if len(blocks) == 1 else jnp.concatenate(blocks, axis=0)
    if b == 1:
        odd = (lax.broadcasted_iota(jnp.int32, cum.shape, 0) & 1) == 1
        return jnp.where(odd, pltpu.roll(cum, 1, 0), cum)
    tiles = cum.reshape(ROWS // 8, 8, width)
    if b == 4:
        ref = jnp.broadcast_to(tiles[:, 3:4, :], tiles.shape)
    else:
        sub = lax.broadcasted_iota(jnp.int32, tiles.shape, 1)
        ref = jnp.where(sub < 4, jnp.broadcast_to(tiles[:, 1:2, :], tiles.shape),
                        jnp.broadcast_to(tiles[:, 5:6, :], tiles.shape))
    return ref.reshape(cum.shape)


def _hgrn_blocks(q_ref, f_ref, i_ref, lbraw_ref, lu_ref, m_ref, layer, seq):
    heads = range(HG_HEADS_PER_STEP)
    widths = _level_widths(seq)
    lu = lu_ref[...]
    q = [_head_cols(q_ref, hh) for hh in heads]
    v_bf = [_head_cols(i_ref, hh).astype(BF16) for hh in heads]
    k, log_f = [], []
    for hh in heads:
        hf = _head_cols(f_ref, hh)
        lb = _lower_bound(_head_cols(lbraw_ref, hh), layer)
        log_f.append(jnp.log(lb + (1.0 - lb) * _sigmoid(hf)))
        k.append((1.0 - lb) * _sigmoid(-hf))
    sums = [_dot3_l(lu, p) for p in log_f]
    cum = [s[0:ROWS] for s in sums]
    att = [m_ref[len(widths)] * _dot_nt(q[hh].astype(BF16), k[hh].astype(BF16)) for hh in heads]
    for lvl, b in enumerate(widths):
        for hh in heads:
            e = jnp.exp(-jnp.abs(cum[hh] - _level_ref(cum[hh], b)))
            att[hh] = att[hh] + m_ref[lvl] * _dot_nt((q[hh] * e).astype(BF16), (k[hh] * e).astype(BF16))
    o_intra = [_dot(att[hh].astype(BF16), v_bf[hh]) for hh in heads]
    q_dec = [(q[hh] * jnp.exp(cum[hh])).astype(BF16) for hh in heads]
    k_dec = [(k[hh] * jnp.exp(sums[hh][ROWS:2 * ROWS])).astype(BF16) for hh in heads]
    return o_intra, q_dec, k_dec, v_bf, cum


def _hgrn_gate_norm(o, g, nw):
    return (_rms(o, nw) * _silu(g)).astype(BF16)


def _hgrn_prompt_body(q_ref, f_ref, i_ref, g_ref, lbraw_ref, nw_ref, lu_ref, m_ref,
                      on_ref, sout_ref, s_sc, *, layer):
    c = pl.program_id(2)

    @pl.when(c == 0)
    def _():
        s_sc[...] = jnp.zeros_like(s_sc)

    heads = range(HG_HEADS_PER_STEP)
    o_intra, q_dec, k_dec, v_bf, cum = _hgrn_blocks(q_ref, f_ref, i_ref, lbraw_ref, lu_ref, m_ref, layer, ROWS)
    s = [s_sc[hh] for hh in heads]
    o = [o_intra[hh] + _dot(q_dec[hh], s[hh].astype(BF16)) for hh in heads]
    upd = [_dot_tn(k_dec[hh], v_bf[hh]) for hh in heads]
    for hh in heads:
        on_ref[:, hh * HG_VAL_DIM:(hh + 1) * HG_VAL_DIM] = _hgrn_gate_norm(o[hh], _head_cols(g_ref, hh), nw_ref[...])
        last_col = cum[hh].T[:, ROWS - 1:ROWS]
        s_sc[hh] = jnp.exp(last_col) * s[hh] + upd[hh]

    @pl.when(c == pl.num_programs(2) - 1)
    def _():
        sout_ref[...] = s_sc[...].reshape(sout_ref.shape)


def _hgrn_sample_body(q_ref, f_ref, i_ref, g_ref, st_ref, lbraw_ref, nw_ref, lu_ref, m_ref,
                      on_ref, sout_ref, o_sc, cumt_sc, qd_sc, kdt_sc, *, layer, seq):
    o_intra, q_dec, k_dec, _, cum = _hgrn_blocks(q_ref, f_ref, i_ref, lbraw_ref, lu_ref, m_ref, layer, seq)
    for hh in range(HG_HEADS_PER_STEP):
        o_sc[hh] = o_intra[hh]
        cumt_sc[hh] = cum[hh].T
        qd_sc[hh] = q_dec[hh]
        kdt_sc[hh] = k_dec[hh].astype(F32).T.astype(BF16)
    shift = seq.bit_length() - 1
    per_slab = SLAB // seq
    slab_seq = lax.shift_right_logical(lax.broadcasted_iota(jnp.int32, (SLAB, HG_VAL_DIM), 0), shift)
    seq_of_row = lax.shift_right_logical(lax.broadcasted_iota(jnp.int32, (ROWS, HG_VAL_DIM), 0), shift)
    lane = lax.broadcasted_iota(jnp.int32, (HG_KEY_DIM, ROWS), 1)

    def per_sequence(s, carry):
        own = seq_of_row == s
        is_last = lane == s * seq + seq - 1
        r0 = pl.multiple_of((s // per_slab) * SLAB, SLAB)
        heads = range(HG_HEADS_PER_STEP)
        s0 = [st_ref[s, hh] for hh in heads]
        o_inter = [_dot(qd_sc[hh, pl.ds(r0, SLAB), :], s0[hh].astype(BF16)) for hh in heads]
        upd = [_dot(kdt_sc[hh], jnp.where(own, _head_cols(i_ref, hh), 0.0).astype(BF16)) for hh in heads]
        for hh in heads:
            o_sc[hh, pl.ds(r0, SLAB), :] += jnp.where(slab_seq == s % per_slab, o_inter[hh], 0.0)
            last_col = jnp.sum(jnp.where(is_last, cumt_sc[hh], 0.0), axis=1, keepdims=True)
            sout_ref[s, hh] = jnp.exp(last_col) * s0[hh] + upd[hh]
        return carry

    lax.fori_loop(0, ROWS // seq, per_sequence, 0, unroll=2)
    for hh in range(HG_HEADS_PER_STEP):
        on_ref[:, hh * HG_VAL_DIM:(hh + 1) * HG_VAL_DIM] = _hgrn_gate_norm(o_sc[hh], _head_cols(g_ref, hh), nw_ref[...])


def _hgrn_token_specs(ridx, hidx):
    width = HG_HEADS_PER_STEP * HG_KEY_DIM

    def col(off):
        return lambda *ids: (ridx(*ids), off // width + hidx(*ids))
    return [pl.BlockSpec((ROWS, width), col(off)) for off in (COL_Q, COL_F, COL_I, COL_G)]


def _hgrn_param_specs(hidx, n_rows, n_masks):
    return [
        pl.BlockSpec((n_rows, HG_HEADS_PER_STEP * HG_KEY_DIM), lambda *ids: (0, hidx(*ids))),
        pl.BlockSpec((1, HG_VAL_DIM), lambda *ids: (0, 0)),
        pl.BlockSpec((2 * ROWS, 3 * ROWS), lambda *ids: (0, 0)),
        pl.BlockSpec((n_masks, ROWS, ROWS), lambda *ids: (0, 0, 0)),
    ]


def _hgrn_prompt(proj, hg_lb, hg_norm, layer, batch, length):
    nblk = length // ROWS
    hb = HG_HEADS_PER_STEP
    lu, m_all = _hgrn_consts(ROWS)
    ridx = lambda b, h, c: b * nblk + c
    hidx = lambda b, h, c: h
    return pl.pallas_call(
        functools.partial(_hgrn_prompt_body, layer=layer),
        grid=(batch, HG_N_HEADS // hb, nblk),
        in_specs=_hgrn_token_specs(ridx, hidx) + _hgrn_param_specs(hidx, hg_lb.shape[0], m_all.shape[0]),
        out_specs=[
            pl.BlockSpec((ROWS, hb * HG_VAL_DIM), lambda b, h, c: (b * nblk + c, h)),
            pl.BlockSpec((1, hb, HG_KEY_DIM, HG_VAL_DIM), lambda b, h, c: (b, h, 0, 0)),
        ],
        out_shape=[
            jax.ShapeDtypeStruct((batch * length, D_MODEL), BF16),
            jax.ShapeDtypeStruct((batch, HG_N_HEADS, HG_KEY_DIM, HG_VAL_DIM), F32),
        ],
        scratch_shapes=[pltpu.VMEM((hb, HG_KEY_DIM, HG_VAL_DIM), F32)],
        compiler_params=_params("parallel", "parallel", "arbitrary"),
        name="hgrn_prompt",
    )(*([proj] * 4 + [hg_lb, hg_norm, lu, m_all]))


def _hgrn_sample(proj, state, hg_lb, hg_norm, layer, row0, batch, seq):
    per_blk = ROWS // seq
    nblk = batch // per_blk
    blk0 = row0 // ROWS
    hb = HG_HEADS_PER_STEP
    lu, m_all = _hgrn_consts(seq)
    ridx = lambda m, h: blk0 + m
    hidx = lambda m, h: h
    state_spec = pl.BlockSpec((per_blk, hb, HG_KEY_DIM, HG_VAL_DIM), lambda m, h: (m, h, 0, 0))
    return pl.pallas_call(
        functools.partial(_hgrn_sample_body, layer=layer, seq=seq),
        grid=(nblk, HG_N_HEADS // hb),
        in_specs=_hgrn_token_specs(ridx, hidx) + [state_spec]
        + _hgrn_param_specs(hidx, hg_lb.shape[0], m_all.shape[0]),
        out_specs=[pl.BlockSpec((ROWS, hb * HG_VAL_DIM), lambda m, h: (m, h)), state_spec],
        out_shape=[jax.ShapeDtypeStruct((batch * seq, D_MODEL), BF16), jax.ShapeDtypeStruct(state.shape, F32)],
        scratch_shapes=[
            pltpu.VMEM((hb, ROWS, HG_VAL_DIM), F32),
            pltpu.VMEM((hb, HG_KEY_DIM, ROWS), F32),
            pltpu.VMEM((hb, ROWS, HG_KEY_DIM), BF16),
            pltpu.VMEM((hb, HG_KEY_DIM, ROWS), BF16),
        ],
        compiler_params=_params("parallel", "parallel"),
        name="hgrn_sample",
    )(*([proj] * 4 + [state, hg_lb, hg_norm, lu, m_all]))


def _head_slabs(v):
    v = v.reshape(SSM_N_GROUPS, HEADS_PER_GROUP).astype(F32)
    return jnp.pad(v, ((0, 0), (0, LANES - HEADS_PER_GROUP))).reshape(1, SSM_N_GROUPS * LANES)


def _sample_carry(conv0, seq):
    bsz, wm1, cd = conv0.shape
    return jnp.pad(conv0, ((0, 0), (seq - wm1, 0), (0, 0))).reshape(bsz * seq, cd)


def kernel(x_prompt, x_sample, p_prompt, p_sample, state_ssm, state_conv, state_hgrn, norm_mix_pre, w_in, conv_w, conv_b, dt_bias, a_log, d_skip, ssm_norm, w_br_a, hg_lb, hg_norm, w_br_b, w_out, norm_mix_post, norm_ffn_pre, w_up, w_down, norm_ffn_post, norm_ple, w_ple_gate, w_ple_proj):
    depth = w_in.shape[0]
    bp, lp, _ = x_prompt.shape
    bs, ls, _ = x_sample.shape
    tp, ts = bp * lp, bs * ls
    wm1 = SSM_CONV_WIDTH - 1
    assert lp % ROWS == 0 and ROWS % ls == 0 and ls & (ls - 1) == 0 and ts % ROWS == 0
    assert lp >= wm1 and ls >= wm1

    xp = x_prompt.reshape(tp, D_MODEL)
    xs = x_sample.reshape(ts, D_MODEL)
    row = lambda a: a.reshape(1, -1).astype(F32)
    outs = {k: [] for k in ("ssm_p", "conv_p", "hg_p", "ssm_s", "conv_s", "hg_s")}
    for li in range(depth):
        h = _rmsnorm_cast(xp, xs, row(norm_mix_pre[li]))
        proj = _in_proj(h, jnp.transpose(w_in[li]))

        tails = [proj[b * lp + lp - wm1:b * lp + lp, COL_X:COL_Q] for b in range(bp)]
        outs["conv_p"].append(jnp.stack(tails))
        tails_s = [proj[tp + ls - wm1 + r::ls, COL_X:COL_Q] for r in range(wm1)]
        outs["conv_s"].append(jnp.transpose(jnp.stack(tails_s), (1, 0, 2)))

        ssd_params = (conv_w[li], row(conv_b[li]), _head_slabs(dt_bias[li]), _head_slabs(a_log[li]),
                      row(jnp.repeat(d_skip[li], SSM_HEAD_DIM)), row(ssm_norm[li]))
        yn_p, ssm_p = _ssd_prompt(proj, ssd_params, bp, lp)
        yn_s, ssm_s = _ssd_sample(proj, _sample_carry(state_conv[li], ls), state_ssm[li], ssd_params, tp, bs, ls)
        on_p, hg_p = _hgrn_prompt(proj, hg_lb, row(hg_norm[li]), li, bp, lp)
        on_s, hg_s = _hgrn_sample(proj, state_hgrn[li], hg_lb, row(hg_norm[li]), li, tp, bs, ls)
        outs["ssm_p"].append(ssm_p)
        outs["ssm_s"].append(ssm_s)
        outs["hg_p"].append(hg_p)
        outs["hg_s"].append(hg_s)

        mixed = _merge(yn_p, yn_s, w_br_a[li].astype(BF16), on_p, on_s, w_br_b[li].astype(BF16), proj)
        x1, h2 = _out_proj(mixed, w_out[li].astype(BF16), xp, xs, row(norm_mix_post[li]), row(norm_ffn_pre[li]))
        u = _matmul(h2, w_up[li].astype(BF16), out_dtype=BF16, act="relu2", name="ffn_up")
        x2, h3 = _ffn_down(u, w_down[li].astype(BF16), x1, row(norm_ffn_post[li]), row(norm_ple[li]))
        xp, xs = _ple(h3, w_ple_gate[li].astype(BF16), p_prompt[li].reshape(tp, -1), p_sample[li].reshape(ts, -1),
                      w_ple_proj[li].astype(BF16), x2)

    stack = lambda k: jnp.stack(outs[k])
    return (xp.reshape(bp, lp, D_MODEL), xs.reshape(bs, ls, D_MODEL),
            stack("ssm_p"), stack("conv_p"), stack("hg_p"), stack("ssm_s"), stack("conv_s"), stack("hg_s"))
```

```python
import functools
import math

import numpy as np
import jax
import jax.numpy as jnp
from jax import lax
from jax.experimental import pallas as pl
from jax.experimental.pallas import tpu as pltpu

F32 = jnp.float32
BF16 = jnp.bfloat16

D_MODEL = 2048
SSM_D_INNER = 2 * D_MODEL
SSM_HEAD_DIM = 64
SSM_N_HEADS = SSM_D_INNER // SSM_HEAD_DIM
SSM_N_GROUPS = 8
SSM_D_STATE = 128
SSM_CONV_WIDTH = 4
SSM_BC = SSM_N_GROUPS * SSM_D_STATE
SSM_CONV_DIM = SSM_D_INNER + 2 * SSM_BC
HG_KEY_DIM = 128
HG_N_HEADS = D_MODEL // HG_KEY_DIM
HG_VAL_DIM = D_MODEL // HG_N_HEADS
FFN_HIDDEN = 4 * D_MODEL
NORM_EPS = 1e-6

OFF_XBC = SSM_D_INNER
OFF_DT = OFF_XBC + SSM_CONV_DIM
OFF_HQ = OFF_DT + SSM_N_HEADS

LANES = 128
ROWS = 128
GROUP_W = SSM_D_INNER // SSM_N_GROUPS
HEADS_PER_GROUP = SSM_N_HEADS // SSM_N_GROUPS
HG_HEADS_PER_STEP = 4
SSD_GROUPS_PER_STEP = 4
LOG2E = 1.4426950408889634

COL_Z = 0
COL_X = SSM_D_INNER
COL_B = COL_X + SSM_D_INNER
COL_C = COL_B + SSM_BC
COL_Q = COL_C + SSM_BC
COL_F = COL_Q + D_MODEL
COL_I = COL_F + D_MODEL
COL_G = COL_I + D_MODEL
COL_GA = COL_G + D_MODEL
COL_GB = COL_GA + D_MODEL
COL_DT = COL_GB + D_MODEL
PROJ_W = COL_DT + SSM_N_GROUPS * LANES

VMEM_LIMIT = 52 * 1024 * 1024


def _params(*sem):
    return pltpu.CompilerParams(dimension_semantics=sem, vmem_limit_bytes=VMEM_LIMIT)


def _pick(n, cands):
    for c in cands:
        if n % c == 0:
            return c
    raise ValueError(f"no tile for {n} in {cands}")


def _dot(a, b):
    return jnp.dot(a, b, preferred_element_type=F32)


def _dot_nt(a, b):
    return lax.dot_general(a, b, (((1,), (1,)), ((), ())), preferred_element_type=F32)


def _dot_tn(a, b):
    return lax.dot_general(a, b, (((0,), (0,)), ((), ())), preferred_element_type=F32)


def _split3(x):
    hi = x.astype(BF16)
    r1 = x - hi.astype(F32)
    mid = r1.astype(BF16)
    lo = (r1 - mid.astype(F32)).astype(BF16)
    return hi, mid, lo


def _dot3_l(a01x3, x):
    return _dot(a01x3, jnp.concatenate(_split3(x), axis=0))


def _dot3_r(x, b01x3):
    return _dot(jnp.concatenate(_split3(x), axis=1), b01x3)


def _sigmoid(x):
    return 1.0 / (1.0 + jnp.exp(-x))


def _silu(x):
    return x * _sigmoid(x)


def _softplus(x):
    return jnp.maximum(x, 0.0) + jnp.log1p(jnp.exp(-jnp.abs(x)))


def _rms(x, w):
    ms = jnp.mean(x * x, axis=-1, keepdims=True)
    return x * lax.rsqrt(ms + NORM_EPS) * w


def _seq_masks(seq):
    t = np.arange(ROWS)
    same = (t[:, None] // seq) == (t[None, :] // seq)
    lower = same & (t[None, :] <= t[:, None])
    upper = same & (t[None, :] > t[:, None])
    return lower, upper


def _ssd_consts(seq):
    lower, upper = _seq_masks(seq)
    lu = np.concatenate([lower, upper], axis=0).astype(np.float32)
    rep = np.zeros((LANES, GROUP_W), np.float32)
    for j in range(HEADS_PER_GROUP):
        rep[j, j * SSM_HEAD_DIM:(j + 1) * SSM_HEAD_DIM] = 1.0
    return jnp.asarray(np.tile(lu, (1, 3)), BF16), jnp.asarray(np.tile(rep, (3, 1)), BF16)


def _level_widths(seq):
    widths, b = [], seq // 2
    while b >= 1:
        widths.append(b)
        b //= 2
    return widths


def _hgrn_consts(seq):
    lower, upper = _seq_masks(seq)
    t = np.arange(ROWS)
    masks = []
    for b in _level_widths(seq):
        is_q = (t // b) % 2 == 1
        pair = (t[:, None] // (2 * b)) == (t[None, :] // (2 * b))
        masks.append(pair & is_q[:, None] & (~is_q)[None, :])
    masks.append(np.eye(ROWS, dtype=bool))
    m_all = np.stack(masks).astype(np.float32)
    assert np.array_equal(m_all.sum(0) > 0, lower) and m_all.sum(0).max() == 1
    lu = np.concatenate([lower, upper], axis=0).astype(np.float32)
    return jnp.asarray(np.tile(lu, (1, 3)), BF16), jnp.asarray(m_all, F32)


def _group_specs(tm, n_prompt_blocks, width, axis=0):
    last = n_prompt_blocks - 1
    return [
        pl.BlockSpec((tm, width), lambda *ids: (jnp.minimum(ids[axis], last), 0)),
        pl.BlockSpec((tm, width), lambda *ids: (jnp.maximum(ids[axis] - n_prompt_blocks, 0), 0)),
    ]


def _group_rows(xp_ref, xs_ref, n_prompt_blocks, axis=0):
    return jnp.where(pl.program_id(axis) < n_prompt_blocks, xp_ref[...], xs_ref[...])


def _rmsnorm_body(xp_ref, xs_ref, w_ref, o_ref, *, n_prompt_blocks):
    o_ref[...] = _rms(_group_rows(xp_ref, xs_ref, n_prompt_blocks), w_ref[...]).astype(o_ref.dtype)


def _rmsnorm_cast(xp, xs, w):
    tp, d = xp.shape
    ts = xs.shape[0]
    tm = _pick(math.gcd(tp, ts), (512, 256, 128))
    return pl.pallas_call(
        functools.partial(_rmsnorm_body, n_prompt_blocks=tp // tm),
        grid=((tp + ts) // tm,),
        in_specs=_group_specs(tm, tp // tm, d) + [pl.BlockSpec((1, d), lambda i: (0, 0))],
        out_specs=pl.BlockSpec((tm, d), lambda i: (i, 0)),
        out_shape=jax.ShapeDtypeStruct((tp + ts, d), BF16),
        compiler_params=_params("parallel"),
        name="rmsnorm_cast",
    )(xp, xs, w)


IN_TN = 1024
IN_SHIFT = OFF_HQ - OFF_DT
N_SEG1 = OFF_DT // IN_TN
N_SEG2 = (COL_DT - OFF_DT) // IN_TN


def _in_proj_body(h_ref, wa_ref, wb_ref, wd_ref, o_ref, w_sc):
    j = pl.program_id(0)

    @pl.when(pl.program_id(1) == 0)
    def _():
        @pl.when(j < N_SEG1)
        def _():
            w_sc[...] = wa_ref[...].astype(BF16)

        @pl.when(jnp.logical_and(j >= N_SEG1, j < N_SEG1 + N_SEG2))
        def _():
            w_sc[0:IN_TN - IN_SHIFT] = wa_ref[IN_SHIFT:IN_TN].astype(BF16)
            w_sc[IN_TN - IN_SHIFT:IN_TN] = wb_ref[...].astype(BF16)

        @pl.when(j >= N_SEG1 + N_SEG2)
        def _():
            pad = jnp.zeros((LANES - IN_SHIFT, wd_ref.shape[1]), F32)
            w_sc[0:LANES] = jnp.concatenate([wd_ref[...], pad], axis=0).astype(BF16)

    @pl.when(j < N_SEG1 + N_SEG2)
    def _():
        o_ref[...] = _dot_nt(h_ref[...], w_sc[...])

    @pl.when(j >= N_SEG1 + N_SEG2)
    def _():
        o_ref[:, 0:LANES] = _dot_nt(h_ref[...], w_sc[0:LANES])
        o_ref[:, LANES:IN_TN] = jnp.zeros((o_ref.shape[0], IN_TN - LANES), F32)


def _in_proj(h, wt):
    t, d = h.shape
    assert OFF_DT % IN_TN == 0 and (COL_DT - OFF_DT) % IN_TN == 0 and PROJ_W - COL_DT == IN_TN
    assert IN_TN % IN_SHIFT == 0 and IN_SHIFT % 8 == 0 and IN_SHIFT == SSM_N_HEADS <= LANES
    tm = _pick(t, (1088, 512, 256, 128))
    last_a = N_SEG1 + N_SEG2 - 1
    per = IN_TN // IN_SHIFT
    return pl.pallas_call(
        _in_proj_body,
        grid=(PROJ_W // IN_TN, t // tm),
        in_specs=[
            pl.BlockSpec((tm, d), lambda j, i: (i, 0)),
            pl.BlockSpec((IN_TN, d), lambda j, i: (jnp.minimum(j, last_a), 0)),
            pl.BlockSpec((IN_SHIFT, d), lambda j, i: (jnp.minimum(j, last_a) * per + per, 0)),
            pl.BlockSpec((IN_SHIFT, d), lambda j, i: (OFF_DT // IN_SHIFT, 0)),
        ],
        out_specs=pl.BlockSpec((tm, IN_TN), lambda j, i: (i, j)),
        out_shape=jax.ShapeDtypeStruct((t, PROJ_W), F32),
        scratch_shapes=[pltpu.VMEM((IN_TN, d), BF16)],
        compiler_params=_params("arbitrary", "arbitrary"),
        name="in_proj",
    )(h, wt, wt, wt)


def _matmul_body(a_ref, b_ref, o_ref, *, act):
    acc = _dot(a_ref[...], b_ref[...])
    if act == "relu2":
        acc = jnp.square(jnp.maximum(acc, 0.0))
    o_ref[...] = acc.astype(o_ref.dtype)


def _matmul(a, b, *, out_dtype, act=None, name):
    m, k = a.shape
    n = b.shape[1]
    tm = _pick(m, (1088, 512, 256, 128))
    tn = _pick(n, (2048, 1024, 512))
    return pl.pallas_call(
        functools.partial(_matmul_body, act=act),
        grid=(m // tm, n // tn),
        in_specs=[pl.BlockSpec((tm, k), lambda i, j: (i, 0)), pl.BlockSpec((k, tn), lambda i, j: (0, j))],
        out_specs=pl.BlockSpec((tm, tn), lambda i, j: (i, j)),
        out_shape=jax.ShapeDtypeStruct((m, n), out_dtype),
        compiler_params=_params("parallel", "parallel"),
        name=name,
    )(a, b)


def _merge_body(yap_ref, yas_ref, wa_ref, ybp_ref, ybs_ref, wb_ref, ga_ref, gb_ref, o_ref, *, n_prompt_blocks):
    a = _dot(_group_rows(yap_ref, yas_ref, n_prompt_blocks, axis=1), wa_ref[...])
    b = _dot(_group_rows(ybp_ref, ybs_ref, n_prompt_blocks, axis=1), wb_ref[...])
    o_ref[...] = (_sigmoid(ga_ref[...]) * a + _sigmoid(gb_ref[...]) * b).astype(o_ref.dtype)


def _merge(ya_p, ya_s, wa, yb_p, yb_s, wb, proj):
    tp, ts = ya_p.shape[0], ya_s.shape[0]
    tm = _pick(math.gcd(tp, ts), (256, 128))
    tn = 1024
    npb = tp // tm
    return pl.pallas_call(
        functools.partial(_merge_body, n_prompt_blocks=npb),
        grid=(D_MODEL // tn, (tp + ts) // tm),
        in_specs=_group_specs(tm, npb, SSM_D_INNER, axis=1)
        + [pl.BlockSpec((SSM_D_INNER, tn), lambda j, i: (0, j))]
        + _group_specs(tm, npb, D_MODEL, axis=1)
        + [pl.BlockSpec((D_MODEL, tn), lambda j, i: (0, j)),
           pl.BlockSpec((tm, tn), lambda j, i: (i, COL_GA // tn + j)),
           pl.BlockSpec((tm, tn), lambda j, i: (i, COL_GB // tn + j))],
        out_specs=pl.BlockSpec((tm, tn), lambda j, i: (i, j)),
        out_shape=jax.ShapeDtypeStruct((tp + ts, D_MODEL), BF16),
        compiler_params=_params("parallel", "parallel"),
        name="merge_branches",
    )(ya_p, ya_s, wa, yb_p, yb_s, wb, proj, proj)


def _out_body(a_ref, w_ref, xp_ref, xs_ref, npost_ref, nnext_ref, x1_ref, h_ref, *, n_prompt_blocks):
    acc = _dot(a_ref[...], w_ref[...])
    x1 = _group_rows(xp_ref, xs_ref, n_prompt_blocks) + _rms(acc, npost_ref[...])
    x1_ref[...] = x1
    h_ref[...] = _rms(x1, nnext_ref[...]).astype(h_ref.dtype)


def _out_proj(a, w, xp, xs, npost, nnext):
    t = a.shape[0]
    tp, ts = xp.shape[0], xs.shape[0]
    tm = _pick(math.gcd(tp, ts), (256, 128))
    row = lambda i: (i, 0)
    fixed = lambda i: (0, 0)
    return pl.pallas_call(
        functools.partial(_out_body, n_prompt_blocks=tp // tm),
        grid=(t // tm,),
        in_specs=[pl.BlockSpec((tm, D_MODEL), row), pl.BlockSpec((D_MODEL, D_MODEL), fixed)]
        + _group_specs(tm, tp // tm, D_MODEL)
        + [pl.BlockSpec((1, D_MODEL), fixed), pl.BlockSpec((1, D_MODEL), fixed)],
        out_specs=[pl.BlockSpec((tm, D_MODEL), row), pl.BlockSpec((tm, D_MODEL), row)],
        out_shape=[jax.ShapeDtypeStruct((t, D_MODEL), F32), jax.ShapeDtypeStruct((t, D_MODEL), BF16)],
        compiler_params=_params("parallel"),
        name="out_proj_norm",
    )(a, w, xp, xs, npost, nnext)


def _down_body(u_ref, w_ref, x_ref, npost_ref, nnext_ref, x2_ref, h_ref, acc_ref):
    k = pl.program_id(1)

    @pl.when(k == 0)
    def _():
        acc_ref[...] = jnp.zeros_like(acc_ref)

    acc_ref[...] += _dot(u_ref[...], w_ref[...])

    @pl.when(k == pl.num_programs(1) - 1)
    def _():
        x2 = x_ref[...] + _rms(acc_ref[...], npost_ref[...])
        x2_ref[...] = x2
        h_ref[...] = _rms(x2, nnext_ref[...]).astype(h_ref.dtype)


def _ffn_down(u, w, x, npost, nnext):
    t, f = u.shape
    tm = _pick(t, (544, 512, 256, 128))
    tk = 2048
    row = lambda i, k: (i, 0)
    fixed = lambda i, k: (0, 0)
    return pl.pallas_call(
        _down_body,
        grid=(t // tm, f // tk),
        in_specs=[
            pl.BlockSpec((tm, tk), lambda i, k: (i, k)),
            pl.BlockSpec((tk, D_MODEL), lambda i, k: (k, 0)),
            pl.BlockSpec((tm, D_MODEL), row),
            pl.BlockSpec((1, D_MODEL), fixed),
            pl.BlockSpec((1, D_MODEL), fixed),
        ],
        out_specs=[pl.BlockSpec((tm, D_MODEL), row), pl.BlockSpec((tm, D_MODEL), row)],
        out_shape=[jax.ShapeDtypeStruct((t, D_MODEL), F32), jax.ShapeDtypeStruct((t, D_MODEL), BF16)],
        scratch_shapes=[pltpu.VMEM((tm, D_MODEL), F32)],
        compiler_params=_params("parallel", "arbitrary"),
        name="ffn_down_norm",
    )(u, w, x, npost, nnext)


def _ple_body(h_ref, wg_ref, pp_ref, ps_ref, wp_ref, x_ref, yp_ref, ys_ref, *, n_prompt_blocks):
    i = pl.program_id(0)
    g = _sigmoid(_dot(h_ref[...], wg_ref[...]))
    e = _dot(_group_rows(pp_ref, ps_ref, n_prompt_blocks).astype(BF16), wp_ref[...])
    y = x_ref[...] + g * e

    @pl.when(i < n_prompt_blocks)
    def _():
        yp_ref[...] = y

    @pl.when(i >= n_prompt_blocks)
    def _():
        ys_ref[...] = y


def _ple(h, wg, pp, ps, wp, x):
    tp, pd = pp.shape
    ts = ps.shape[0]
    tm = _pick(math.gcd(tp, ts), (256, 128))
    npb = tp // tm
    row = lambda i: (i, 0)
    fixed = lambda i: (0, 0)
    return pl.pallas_call(
        functools.partial(_ple_body, n_prompt_blocks=npb),
        grid=((tp + ts) // tm,),
        in_specs=[pl.BlockSpec((tm, D_MODEL), row), pl.BlockSpec((D_MODEL, D_MODEL), fixed)]
        + _group_specs(tm, npb, pd)
        + [pl.BlockSpec((pd, D_MODEL), fixed), pl.BlockSpec((tm, D_MODEL), row)],
        out_specs=_group_specs(tm, npb, D_MODEL),
        out_shape=[jax.ShapeDtypeStruct((tp, D_MODEL), F32), jax.ShapeDtypeStruct((ts, D_MODEL), F32)],
        compiler_params=_params("arbitrary"),
        name="ple_gate",
    )(h, wg, pp, ps, wp, x)


def _conv_taps(cur, shifted, cw, cb):
    acc = cw[SSM_CONV_WIDTH - 1:SSM_CONV_WIDTH, :] * cur
    for d in range(1, SSM_CONV_WIDTH):
        acc = acc + cw[SSM_CONV_WIDTH - 1 - d:SSM_CONV_WIDTH - d, :] * shifted[d]
    return _silu(acc + cb)


def _conv_prompt(cur_ref, halo_sc, cw_ref, cb_ref):
    cur = cur_ref[...]
    top = cur[0:8]
    halo = halo_sc[...]
    row8 = lax.broadcasted_iota(jnp.int32, top.shape, 0)
    shifted, shifted_top = {}, {}
    for d in range(1, SSM_CONV_WIDTH):
        shifted[d] = pltpu.roll(cur, d, 0)
        shifted_top[d] = jnp.where(row8 < d, pltpu.roll(halo, d, 0), pltpu.roll(top, d, 0))
    cw = cw_ref[...]
    cb = cb_ref[...]
    out = _conv_taps(cur, shifted, cw, cb)
    out_top = _conv_taps(top, shifted_top, cw, cb)
    halo_sc[...] = cur[ROWS - 8:ROWS]
    return jnp.concatenate([out_top, out[8:]], axis=0)


def _conv_sample(cur_ref, carry_ref, cw_ref, cb_ref, seq):
    cur = cur_ref[...]
    carry = carry_ref[...]
    pos = lax.broadcasted_iota(jnp.int32, cur.shape, 0) & (seq - 1)
    shifted = {}
    for d in range(1, SSM_CONV_WIDTH):
        shifted[d] = jnp.where(pos >= d, pltpu.roll(cur, d, 0), pltpu.roll(carry, ROWS - (seq - d), 0))
    return _conv_taps(cur, shifted, cw_ref[...], cb_ref[...])


def _ssd_blocks(xg, bg_bf, cg_bf, dt_raw, dtb, alog, lu_ref, rep_ref, y_sc):
    groups = range(len(xg))
    lu = lu_ref[...]
    mask = lu[0:ROWS, 0:ROWS].astype(F32) > 0.5
    dt = [_softplus(dt_raw[g] + dtb[g]) for g in groups]
    cr = [_dot3_l(lu, dt[g] * (-jnp.exp(alog[g]))) for g in groups]
    cum = [c[0:ROWS] for c in cr]
    col = [c * LOG2E for c in cum]
    row_t = [(col[g] - jnp.log2(dt[g])).T for g in groups]
    x_bf = [x.astype(BF16) for x in xg]
    cb = [_dot_nt(cg_bf[g], bg_bf[g]) for g in groups]
    for j in range(HEADS_PER_GROUP):
        hs = slice(j * SSM_HEAD_DIM, (j + 1) * SSM_HEAD_DIM)
        for g in groups:
            seg = col[g][:, j:j + 1] - row_t[g][j:j + 1, :]
            m = (cb[g] * jnp.where(mask, jnp.exp2(seg), 0.0)).astype(BF16)
            y_sc[:, g * GROUP_W + j * SSM_HEAD_DIM:g * GROUP_W + (j + 1) * SSM_HEAD_DIM] = _dot(m, x_bf[g][:, hs])
    rep = rep_ref[...]
    slabs = [_dot3_r(jnp.concatenate([jnp.exp2(col[g]), dt[g] * jnp.exp(cr[g][ROWS:2 * ROWS])], axis=0), rep)
             for g in groups]
    return cum, [sl[0:ROWS] for sl in slabs], [sl[ROWS:2 * ROWS] for sl in slabs]


def _group_dt(dt_all, group):
    return pltpu.roll(dt_all, (LANES - group * HEADS_PER_GROUP) % LANES, 1)


def _head_rows(e_row):
    return jnp.concatenate(
        [jnp.broadcast_to(e_row[0:1, j:j + 1], (SSM_HEAD_DIM, SSM_D_STATE)) for j in range(HEADS_PER_GROUP)], axis=0)


def _gated_group_norm(y, z, nw):
    return _rms(y * _silu(z), nw)


def _ssd_prompt_body(x_ref, b_ref, c_ref, z_ref, dt_ref, cwx_ref, cwb_ref, cwc_ref, cbx_ref, cbb_ref, cbc_ref,
                     dtb_ref, alog_ref, dsk_ref, nw_ref, lu_ref, rep_ref,
                     yn_ref, hout_ref, h_sc, px_sc, pb_sc, pc_sc, y_sc):
    c = pl.program_id(2)
    groups = range(SSD_GROUPS_PER_STEP)

    @pl.when(c == 0)
    def _():
        h_sc[...] = jnp.zeros_like(h_sc)
        px_sc[...] = jnp.zeros_like(px_sc)
        pb_sc[...] = jnp.zeros_like(pb_sc)
        pc_sc[...] = jnp.zeros_like(pc_sc)

    x_all = _conv_prompt(x_ref, px_sc, cwx_ref, cbx_ref)
    b_all = _conv_prompt(b_ref, pb_sc, cwb_ref, cbb_ref).astype(BF16)
    c_all = _conv_prompt(c_ref, pc_sc, cwc_ref, cbc_ref).astype(BF16)
    wide = [slice(g * GROUP_W, (g + 1) * GROUP_W) for g in groups]
    narrow = [slice(g * SSM_D_STATE, (g + 1) * SSM_D_STATE) for g in groups]
    xg = [x_all[:, wide[g]] for g in groups]
    bg = [b_all[:, narrow[g]] for g in groups]
    cg = [c_all[:, narrow[g]] for g in groups]
    cum, e_rep, w_rep = _ssd_blocks(
        xg, bg, cg, [_group_dt(dt_ref[...], pl.program_id(1) * SSD_GROUPS_PER_STEP + g) for g in groups],
        [dtb_ref[:, narrow[g]] for g in groups],
        [alog_ref[:, narrow[g]] for g in groups], lu_ref, rep_ref, y_sc)
    h = [h_sc[wide[g]] for g in groups]
    y_inter = [_dot_nt(cg[g], h[g].astype(BF16)) for g in groups]
    upd = [_dot_tn((xg[g] * w_rep[g]).astype(BF16), bg[g]) for g in groups]
    for g in groups:
        y = y_sc[:, wide[g]] + y_inter[g] * e_rep[g] + dsk_ref[:, wide[g]] * xg[g]
        yn_ref[:, wide[g]] = _gated_group_norm(y, z_ref[:, wide[g]], nw_ref[:, wide[g]]).astype(yn_ref.dtype)
        h_sc[wide[g]] = _head_rows(jnp.exp(cum[g][ROWS - 1:ROWS, :])) * h[g] + upd[g]

    @pl.when(c == pl.num_programs(2) - 1)
    def _():
        hout_ref[...] = h_sc[...].reshape(hout_ref.shape)


SLAB = 16


def _ssd_sample_body(x_ref, b_ref, c_ref, z_ref, dt_ref, hx_ref, hb_ref, hc_ref, st_ref,
                     cwx_ref, cwb_ref, cwc_ref, cbx_ref, cbb_ref, cbc_ref,
                     dtb_ref, alog_ref, dsk_ref, nw_ref, lu_ref, rep_ref,
                     yn_ref, hout_ref, y_sc, yi_sc, cum_sc, c_sc, *, seq):
    xg = _conv_sample(x_ref, hx_ref, cwx_ref, cbx_ref, seq)
    bg = _conv_sample(b_ref, hb_ref, cwb_ref, cbb_ref, seq)
    cg = _conv_sample(c_ref, hc_ref, cwc_ref, cbc_ref, seq)
    c_bf = cg.astype(BF16)
    cums, e_reps, w_reps = _ssd_blocks([xg], [bg.astype(BF16)], [c_bf], [_group_dt(dt_ref[...], pl.program_id(1))],
                                       [dtb_ref[...]], [alog_ref[...]],
                                       lu_ref, rep_ref, y_sc)
    cum_sc[...] = cums[0]
    c_sc[...] = c_bf
    yi_sc[...] = jnp.zeros_like(yi_sc)
    xw_t = (xg * w_reps[0]).T.astype(BF16)
    shift = seq.bit_length() - 1
    per_slab = SLAB // seq
    slab_seq = lax.shift_right_logical(lax.broadcasted_iota(jnp.int32, (SLAB, GROUP_W), 0), shift)
    seq_of_row_n = lax.shift_right_logical(lax.broadcasted_iota(jnp.int32, (ROWS, SSM_D_STATE), 0), shift)
    state_shape = (HEADS_PER_GROUP * SSM_HEAD_DIM, SSM_D_STATE)

    def per_sequence(s, carry):
        h0 = st_ref[s].reshape(state_shape)
        r0 = pl.multiple_of((s // per_slab) * SLAB, SLAB)
        y_inter = _dot_nt(c_sc[pl.ds(r0, SLAB), :], h0.astype(BF16))
        yi_sc[pl.ds(r0, SLAB), :] += jnp.where(slab_seq == s % per_slab, y_inter, 0.0)
        b_own = jnp.where(seq_of_row_n == s, bg, 0.0).astype(BF16)
        last = cum_sc[pl.ds(s * seq + seq - 1, 1), :]
        h_new = _head_rows(jnp.exp(last)) * h0 + _dot(xw_t, b_own)
        hout_ref[s] = h_new.reshape(hout_ref.shape[1:])
        return carry

    lax.fori_loop(0, ROWS // seq, per_sequence, 0, unroll=4)
    y = y_sc[...] + yi_sc[...] * e_reps[0] + dsk_ref[...] * xg
    yn_ref[...] = _gated_group_norm(y, z_ref[...], nw_ref[...]).astype(yn_ref.dtype)


def _ssd_param_specs(gidx, ng):
    wide, narrow = ng * GROUP_W, ng * SSM_D_STATE

    def col(block, off):
        return lambda *ids: (0, off // block + gidx(*ids))
    return [
        pl.BlockSpec((SSM_CONV_WIDTH, wide), col(wide, 0)),
        pl.BlockSpec((SSM_CONV_WIDTH, narrow), col(narrow, SSM_D_INNER)),
        pl.BlockSpec((SSM_CONV_WIDTH, narrow), col(narrow, SSM_D_INNER + SSM_BC)),
        pl.BlockSpec((1, wide), col(wide, 0)),
        pl.BlockSpec((1, narrow), col(narrow, SSM_D_INNER)),
        pl.BlockSpec((1, narrow), col(narrow, SSM_D_INNER + SSM_BC)),
        pl.BlockSpec((1, narrow), col(narrow, 0)),
        pl.BlockSpec((1, narrow), col(narrow, 0)),
        pl.BlockSpec((1, wide), col(wide, 0)),
        pl.BlockSpec((1, wide), col(wide, 0)),
        pl.BlockSpec((2 * ROWS, 3 * ROWS), lambda *ids: (0, 0)),
        pl.BlockSpec((3 * LANES, GROUP_W), lambda *ids: (0, 0)),
    ]


def _ssd_token_specs(ridx, gidx, ng):
    wide, narrow = ng * GROUP_W, ng * SSM_D_STATE

    def col(block, off):
        return lambda *ids: (ridx(*ids), off // block + gidx(*ids))
    return [
        pl.BlockSpec((ROWS, wide), col(wide, COL_X)),
        pl.BlockSpec((ROWS, narrow), col(narrow, COL_B)),
        pl.BlockSpec((ROWS, narrow), col(narrow, COL_C)),
        pl.BlockSpec((ROWS, wide), col(wide, COL_Z)),
        pl.BlockSpec((ROWS, LANES), lambda *ids: (ridx(*ids), COL_DT // LANES)),
    ]


def _ssd_prompt(proj, ssd_params, batch, length):
    nblk = length // ROWS
    ng = SSD_GROUPS_PER_STEP
    lu, rep = _ssd_consts(ROWS)
    ridx = lambda b, g, c: b * nblk + c
    gidx = lambda b, g, c: g
    conv_w, conv_b, dtb, alog, dsk, nw = ssd_params
    args = [proj] * 5 + [conv_w] * 3 + [conv_b] * 3 + [dtb, alog, dsk, nw, lu, rep]
    return pl.pallas_call(
        _ssd_prompt_body,
        grid=(batch, SSM_N_GROUPS // ng, nblk),
        in_specs=_ssd_token_specs(ridx, gidx, ng) + _ssd_param_specs(gidx, ng),
        out_specs=[
            pl.BlockSpec((ROWS, ng * GROUP_W), lambda b, g, c: (b * nblk + c, g)),
            pl.BlockSpec((1, ng * HEADS_PER_GROUP, SSM_HEAD_DIM, SSM_D_STATE), lambda b, g, c: (b, g, 0, 0)),
        ],
        out_shape=[
            jax.ShapeDtypeStruct((batch * length, SSM_D_INNER), BF16),
            jax.ShapeDtypeStruct((batch, SSM_N_HEADS, SSM_HEAD_DIM, SSM_D_STATE), F32),
        ],
        scratch_shapes=[
            pltpu.VMEM((ng * HEADS_PER_GROUP * SSM_HEAD_DIM, SSM_D_STATE), F32),
            pltpu.VMEM((8, ng * GROUP_W), F32),
            pltpu.VMEM((8, ng * SSM_D_STATE), F32),
            pltpu.VMEM((8, ng * SSM_D_STATE), F32),
            pltpu.VMEM((ROWS, ng * GROUP_W), F32),
        ],
        compiler_params=_params("parallel", "parallel", "arbitrary"),
        name="ssd_prompt",
    )(*args)


def _ssd_sample(proj, carry, state, ssd_params, row0, batch, seq):
    per_blk = ROWS // seq
    nblk = batch // per_blk
    blk0 = row0 // ROWS
    lu, rep = _ssd_consts(seq)
    ridx = lambda m, g: blk0 + m
    gidx = lambda m, g: g
    conv_w, conv_b, dtb, alog, dsk, nw = ssd_params

    def carry_spec(block, off):
        return pl.BlockSpec((ROWS, block), lambda m, g: (m, off // block + g))

    state_spec = pl.BlockSpec((per_blk, HEADS_PER_GROUP, SSM_HEAD_DIM, SSM_D_STATE), lambda m, g: (m, g, 0, 0))
    args = [proj] * 5 + [carry] * 3 + [state] + [conv_w] * 3 + [conv_b] * 3 + [dtb, alog, dsk, nw, lu, rep]
    return pl.pallas_call(
        functools.partial(_ssd_sample_body, seq=seq),
        grid=(nblk, SSM_N_GROUPS),
        in_specs=_ssd_token_specs(ridx, gidx, 1)
        + [carry_spec(GROUP_W, 0), carry_spec(SSM_D_STATE, SSM_D_INNER), carry_spec(SSM_D_STATE, SSM_D_INNER + SSM_BC),
           state_spec]
        + _ssd_param_specs(gidx, 1),
        out_specs=[pl.BlockSpec((ROWS, GROUP_W), lambda m, g: (m, g)), state_spec],
        out_shape=[jax.ShapeDtypeStruct((batch * seq, SSM_D_INNER), BF16), jax.ShapeDtypeStruct(state.shape, F32)],
        scratch_shapes=[
            pltpu.VMEM((ROWS, GROUP_W), F32),
            pltpu.VMEM((ROWS, GROUP_W), F32),
            pltpu.VMEM((ROWS, LANES), F32),
            pltpu.VMEM((ROWS, SSM_D_STATE), BF16),
        ],
        compiler_params=_params("parallel", "parallel"),
        name="ssd_sample",
    )(*args)


def _lower_bound(raw, layer):
    e = jnp.exp(raw - jnp.max(raw, axis=0, keepdims=True))
    return jnp.sum(e[0:layer + 1], axis=0, keepdims=True) / jnp.sum(e, axis=0, keepdims=True)


def _head_cols(ref, hh):
    return ref[:, hh * HG_KEY_DIM:(hh + 1) * HG_KEY_DIM]


def _level_ref(cum, b):
    width = cum.shape[1]
    if b >= 8:
        blocks = [jnp.broadcast_to(cum[p * 2 * b + b - 1:p * 2 * b + b, :], (2 * b, width))
                  for p in range(ROWS // (2 * b))]
        return blocks[0] if len(blocks) == 1 else jnp.concatenate(blocks, axis=0)
    if b == 1:
        odd = (lax.broadcasted_iota(jnp.int32, cum.shape, 0) & 1) == 1
        return jnp.where(odd, pltpu.roll(cum, 1, 0), cum)
    tiles = cum.reshape(ROWS // 8, 8, width)
    if b == 4:
        ref = jnp.broadcast_to(tiles[:, 3:4, :], tiles.shape)
    else:
        sub = lax.broadcasted_iota(jnp.int32, tiles.shape, 1)
        ref = jnp.where(sub < 4, jnp.broadcast_to(tiles[:, 1:2, :], tiles.shape),
                        jnp.broadcast_to(tiles[:, 5:6, :], tiles.shape))
    return ref.reshape(cum.shape)


def _hgrn_blocks(q_ref, f_ref, i_ref, lbraw_ref, lu_ref, m_ref, layer, seq):
    heads = range(HG_HEADS_PER_STEP)
    widths = _level_widths(seq)
    lu = lu_ref[...]
    q = [_head_cols(q_ref, hh) for hh in heads]
    v_bf = [_head_cols(i_ref, hh).astype(BF16) for hh in heads]
    k, log_f = [], []
    for hh in heads:
        hf = _head_cols(f_ref, hh)
        lb = _lower_bound(_head_cols(lbraw_ref, hh), layer)
        log_f.append(jnp.log(lb + (1.0 - lb) * _sigmoid(hf)))
        k.append((1.0 - lb) * _sigmoid(-hf))
    sums = [_dot3_l(lu, p) for p in log_f]
    cum = [s[0:ROWS] for s in sums]
    att = [m_ref[len(widths)] * _dot_nt(q[hh].astype(BF16), k[hh].astype(BF16)) for hh in heads]
    for lvl, b in enumerate(widths):
        for hh in heads:
            e = jnp.exp(-jnp.abs(cum[hh] - _level_ref(cum[hh], b)))
            att[hh] = att[hh] + m_ref[lvl] * _dot_nt((q[hh] * e).astype(BF16), (k[hh] * e).astype(BF16))
    o_intra = [_dot(att[hh].astype(BF16), v_bf[hh]) for hh in heads]
    q_dec = [(q[hh] * jnp.exp(cum[hh])).astype(BF16) for hh in heads]
    k_dec = [(k[hh] * jnp.exp(sums[hh][ROWS:2 * ROWS])).astype(BF16) for hh in heads]
    return o_intra, q_dec, k_dec, v_bf, cum


def _hgrn_gate_norm(o, g, nw):
    return (_rms(o, nw) * _silu(g)).astype(BF16)


def _hgrn_prompt_body(q_ref, f_ref, i_ref, g_ref, lbraw_ref, nw_ref, lu_ref, m_ref,
                      on_ref, sout_ref, s_sc, *, layer):
    c = pl.program_id(2)

    @pl.when(c == 0)
    def _():
        s_sc[...] = jnp.zeros_like(s_sc)

    heads = range(HG_HEADS_PER_STEP)
    o_intra, q_dec, k_dec, v_bf, cum = _hgrn_blocks(q_ref, f_ref, i_ref, lbraw_ref, lu_ref, m_ref, layer, ROWS)
    s = [s_sc[hh] for hh in heads]
    o = [o_intra[hh] + _dot(q_dec[hh], s[hh].astype(BF16)) for hh in heads]
    upd = [_dot_tn(k_dec[hh], v_bf[hh]) for hh in heads]
    for hh in heads:
        on_ref[:, hh * HG_VAL_DIM:(hh + 1) * HG_VAL_DIM] = _hgrn_gate_norm(o[hh], _head_cols(g_ref, hh), nw_ref[...])
        last_col = cum[hh].T[:, ROWS - 1:ROWS]
        s_sc[hh] = jnp.exp(last_col) * s[hh] + upd[hh]

    @pl.when(c == pl.num_programs(2) - 1)
    def _():
        sout_ref[...] = s_sc[...].reshape(sout_ref.shape)


def _hgrn_sample_body(q_ref, f_ref, i_ref, g_ref, st_ref, lbraw_ref, nw_ref, lu_ref, m_ref,
                      on_ref, sout_ref, o_sc, cumt_sc, qd_sc, kdt_sc, *, layer, seq):
    o_intra, q_dec, k_dec, _, cum = _hgrn_blocks(q_ref, f_ref, i_ref, lbraw_ref, lu_ref, m_ref, layer, seq)
    for hh in range(HG_HEADS_PER_STEP):
        o_sc[hh] = o_intra[hh]
        cumt_sc[hh] = cum[hh].T
        qd_sc[hh] = q_dec[hh]
        kdt_sc[hh] = k_dec[hh].astype(F32).T.astype(BF16)
    shift = seq.bit_length() - 1
    per_slab = SLAB // seq
    slab_seq = lax.shift_right_logical(lax.broadcasted_iota(jnp.int32, (SLAB, HG_VAL_DIM), 0), shift)
    seq_of_row = lax.shift_right_logical(lax.broadcasted_iota(jnp.int32, (ROWS, HG_VAL_DIM), 0), shift)
    lane = lax.broadcasted_iota(jnp.int32, (HG_KEY_DIM, ROWS), 1)

    def per_sequence(s, carry):
        own = seq_of_row == s
        is_last = lane == s * seq + seq - 1
        r0 = pl.multiple_of((s // per_slab) * SLAB, SLAB)
        heads = range(HG_HEADS_PER_STEP)
        s0 = [st_ref[s, hh] for hh in heads]
        o_inter = [_dot(qd_sc[hh, pl.ds(r0, SLAB), :], s0[hh].astype(BF16)) for hh in heads]
        upd = [_dot(kdt_sc[hh], jnp.where(own, _head_cols(i_ref, hh), 0.0).astype(BF16)) for hh in heads]
        for hh in heads:
            o_sc[hh, pl.ds(r0, SLAB), :] += jnp.where(slab_seq == s % per_slab, o_inter[hh], 0.0)
            last_col = jnp.sum(jnp.where(is_last, cumt_sc[hh], 0.0), axis=1, keepdims=True)
            sout_ref[s, hh] = jnp.exp(last_col) * s0[hh] + upd[hh]
        return carry

    lax.fori_loop(0, ROWS // seq, per_sequence, 0, unroll=4)
    for hh in range(HG_HEADS_PER_STEP):
        on_ref[:, hh * HG_VAL_DIM:(hh + 1) * HG_VAL_DIM] = _hgrn_gate_norm(o_sc[hh], _head_cols(g_ref, hh), nw_ref[...])


def _hgrn_token_specs(ridx, hidx):
    width = HG_HEADS_PER_STEP * HG_KEY_DIM

    def col(off):
        return lambda *ids: (ridx(*ids), off // width + hidx(*ids))
    return [pl.BlockSpec((ROWS, width), col(off)) for off in (COL_Q, COL_F, COL_I, COL_G)]


def _hgrn_param_specs(hidx, n_rows, n_masks):
    return [
        pl.BlockSpec((n_rows, HG_HEADS_PER_STEP * HG_KEY_DIM), lambda *ids: (0, hidx(*ids))),
        pl.BlockSpec((1, HG_VAL_DIM), lambda *ids: (0, 0)),
        pl.BlockSpec((2 * ROWS, 3 * ROWS), lambda *ids: (0, 0)),
        pl.BlockSpec((n_masks, ROWS, ROWS), lambda *ids: (0, 0, 0)),
    ]


def _hgrn_prompt(proj, hg_lb, hg_norm, layer, batch, length):
    nblk = length // ROWS
    hb = HG_HEADS_PER_STEP
    lu, m_all = _hgrn_consts(ROWS)
    ridx = lambda b, h, c: b * nblk + c
    hidx = lambda b, h, c: h
    return pl.pallas_call(
        functools.partial(_hgrn_prompt_body, layer=layer),
        grid=(batch, HG_N_HEADS // hb, nblk),
        in_specs=_hgrn_token_specs(ridx, hidx) + _hgrn_param_specs(hidx, hg_lb.shape[0], m_all.shape[0]),
        out_specs=[
            pl.BlockSpec((ROWS, hb * HG_VAL_DIM), lambda b, h, c: (b * nblk + c, h)),
            pl.BlockSpec((1, hb, HG_KEY_DIM, HG_VAL_DIM), lambda b, h, c: (b, h, 0, 0)),
        ],
        out_shape=[
            jax.ShapeDtypeStruct((batch * length, D_MODEL), BF16),
            jax.ShapeDtypeStruct((batch, HG_N_HEADS, HG_KEY_DIM, HG_VAL_DIM), F32),
        ],
        scratch_shapes=[pltpu.VMEM((hb, HG_KEY_DIM, HG_VAL_DIM), F32)],
        compiler_params=_params("parallel", "parallel", "arbitrary"),
        name="hgrn_prompt",
    )(*([proj] * 4 + [hg_lb, hg_norm, lu, m_all]))


def _hgrn_sample(proj, state, hg_lb, hg_norm, layer, row0, batch, seq):
    per_blk = ROWS // seq
    nblk = batch // per_blk
    blk0 = row0 // ROWS
    hb = HG_HEADS_PER_STEP
    lu, m_all = _hgrn_consts(seq)
    ridx = lambda m, h: blk0 + m
    hidx = lambda m, h: h
    state_spec = pl.BlockSpec((per_blk, hb, HG_KEY_DIM, HG_VAL_DIM), lambda m, h: (m, h, 0, 0))
    return pl.pallas_call(
        functools.partial(_hgrn_sample_body, layer=layer, seq=seq),
        grid=(nblk, HG_N_HEADS // hb),
        in_specs=_hgrn_token_specs(ridx, hidx) + [state_spec]
        + _hgrn_param_specs(hidx, hg_lb.shape[0], m_all.shape[0]),
        out_specs=[pl.BlockSpec((ROWS, hb * HG_VAL_DIM), lambda m, h: (m, h)), state_spec],
        out_shape=[jax.ShapeDtypeStruct((batch * seq, D_MODEL), BF16), jax.ShapeDtypeStruct(state.shape, F32)],
        scratch_shapes=[
            pltpu.VMEM((hb, ROWS, HG_VAL_DIM), F32),
            pltpu.VMEM((hb, HG_KEY_DIM, ROWS), F32),
            pltpu.VMEM((hb, ROWS, HG_KEY_DIM), BF16),
            pltpu.VMEM((hb, HG_KEY_DIM, ROWS), BF16),
        ],
        compiler_params=_params("parallel", "parallel"),
        name="hgrn_sample",
    )(*([proj] * 4 + [state, hg_lb, hg_norm, lu, m_all]))


def _head_slabs(v):
    v = v.reshape(SSM_N_GROUPS, HEADS_PER_GROUP).astype(F32)
    return jnp.pad(v, ((0, 0), (0, LANES - HEADS_PER_GROUP))).reshape(1, SSM_N_GROUPS * LANES)


def _sample_carry(conv0, seq):
    bsz, wm1, cd = conv0.shape
    return jnp.pad(conv0, ((0, 0), (seq - wm1, 0), (0, 0))).reshape(bsz * seq, cd)


def kernel(x_prompt, x_sample, p_prompt, p_sample, state_ssm, state_conv, state_hgrn, norm_mix_pre, w_in, conv_w, conv_b, dt_bias, a_log, d_skip, ssm_norm, w_br_a, hg_lb, hg_norm, w_br_b, w_out, norm_mix_post, norm_ffn_pre, w_up, w_down, norm_ffn_post, norm_ple, w_ple_gate, w_ple_proj):
    depth = w_in.shape[0]
    bp, lp, _ = x_prompt.shape
    bs, ls, _ = x_sample.shape
    tp, ts = bp * lp, bs * ls
    wm1 = SSM_CONV_WIDTH - 1
    assert lp % ROWS == 0 and ROWS % ls == 0 and ls & (ls - 1) == 0 and ts % ROWS == 0
    assert lp >= wm1 and ls >= wm1

    xp = x_prompt.reshape(tp, D_MODEL)
    xs = x_sample.reshape(ts, D_MODEL)
    row = lambda a: a.reshape(1, -1).astype(F32)
    outs = {k: [] for k in ("ssm_p", "conv_p", "hg_p", "ssm_s", "conv_s", "hg_s")}
    for li in range(depth):
        h = _rmsnorm_cast(xp, xs, row(norm_mix_pre[li]))
        proj = _in_proj(h, jnp.transpose(w_in[li]))

        tails = [proj[b * lp + lp - wm1:b * lp + lp, COL_X:COL_Q] for b in range(bp)]
        outs["conv_p"].append(jnp.stack(tails))
        tails_s = [proj[tp + ls - wm1 + r::ls, COL_X:COL_Q] for r in range(wm1)]
        outs["conv_s"].append(jnp.transpose(jnp.stack(tails_s), (1, 0, 2)))

        ssd_params = (conv_w[li], row(conv_b[li]), _head_slabs(dt_bias[li]), _head_slabs(a_log[li]),
                      row(jnp.repeat(d_skip[li], SSM_HEAD_DIM)), row(ssm_norm[li]))
        yn_p, ssm_p = _ssd_prompt(proj, ssd_params, bp, lp)
        yn_s, ssm_s = _ssd_sample(proj, _sample_carry(state_conv[li], ls), state_ssm[li], ssd_params, tp, bs, ls)
        on_p, hg_p = _hgrn_prompt(proj, hg_lb, row(hg_norm[li]), li, bp, lp)
        on_s, hg_s = _hgrn_sample(proj, state_hgrn[li], hg_lb, row(hg_norm[li]), li, tp, bs, ls)
        outs["ssm_p"].append(ssm_p)
        outs["ssm_s"].append(ssm_s)
        outs["hg_p"].append(hg_p)
        outs["hg_s"].append(hg_s)

        mixed = _merge(yn_p, yn_s, w_br_a[li].astype(BF16), on_p, on_s, w_br_b[li].astype(BF16), proj)
        x1, h2 = _out_proj(mixed, w_out[li].astype(BF16), xp, xs, row(norm_mix_post[li]), row(norm_ffn_pre[li]))
        u = _matmul(h2, w_up[li].astype(BF16), out_dtype=BF16, act="relu2", name="ffn_up")
        x2, h3 = _ffn_down(u, w_down[li].astype(BF16), x1, row(norm_ffn_post[li]), row(norm_ple[li]))
        xp, xs = _ple(h3, w_ple_gate[li].astype(BF16), p_prompt[li].reshape(tp, -1), p_sample[li].reshape(ts, -1),
                      w_ple_proj[li].astype(BF16), x2)

    stack = lambda k: jnp.stack(outs[k])
    return (xp.reshape(bp, lp, D_MODEL), xs.reshape(bs, ls, D_MODEL),
            stack("ssm_p"), stack("conv_p"), stack("hg_p"), stack("ssm_s"), stack("conv_s"), stack("hg_s"))
```
